```python
import jax, jax.numpy as jnp
from jax import lax
import numpy as np

D_MODEL = 1024
BATCH = 16
SEQ = 2048
DEPTH = 1

HEAD_DIM = 64
NSA_HEADS = 8
NSA_KV_GROUPS = 2
FOX_HEADS = 8
ROT_DIM = HEAD_DIM // 4
ROPE_THETA = 500000.0
CMP_BLOCK = 32
CMP_STRIDE = 16
SEL_BLOCK = 64
N_SEL = 8
WINDOW = 512
Q_BLOCK = 128
N_EXPERTS = 32
TOP_K = 4
D_FF = D_MODEL
SWIGLU_ALPHA = 1.702
SWIGLU_LIMIT = 7.0
MOE_BLOCK = 128
NORM_EPS = 1e-6
NEG = -1e30
FORCE_BONUS = 1e4

NSA_Q = NSA_HEADS * HEAD_DIM
NSA_KV = NSA_KV_GROUPS * HEAD_DIM
NSA_GATE = NSA_HEADS * 3
FOX_W = FOX_HEADS * HEAD_DIM
IN_SIZES = [NSA_Q, NSA_KV, NSA_KV, NSA_KV, NSA_KV, NSA_KV, NSA_KV, NSA_GATE, FOX_W, FOX_W, FOX_W, FOX_HEADS, D_MODEL, D_MODEL]
IN_COLS = sum(IN_SIZES)
IN_SPLITS = [int(v) for v in np.cumsum(IN_SIZES)[:-1]]

kernel_name = 'hybrid_nsa_fox_moe_block'


def rms_norm(x, g):
    xf = x.astype(jnp.float32)
    y = xf * lax.rsqrt(jnp.mean(xf * xf, axis=-1, keepdims=True) + NORM_EPS)
    return (y * g.astype(jnp.float32)).astype(x.dtype)


def masked_softmax(logits, mask):
    l = jnp.where(mask, logits.astype(jnp.float32), NEG)
    return jnp.where(mask, jax.nn.softmax(l, axis=-1), 0.0)


def partial_rope(x, positions):
    freqs = ROPE_THETA ** (-jnp.arange(0, ROT_DIM, 2, dtype=jnp.float32) / ROT_DIM)
    ang = positions.astype(jnp.float32)[..., None] * freqs
    cos, sin = jnp.cos(ang)[:, :, None, :], jnp.sin(ang)[:, :, None, :]
    xr = x[..., :ROT_DIM].astype(jnp.float32)
    x1, x2 = xr[..., :ROT_DIM // 2], xr[..., ROT_DIM // 2:]
    rot = jnp.concatenate([x1 * cos - x2 * sin, x2 * cos + x1 * sin], axis=-1)
    return jnp.concatenate([rot.astype(x.dtype), x[..., ROT_DIM:]], axis=-1)


def compress_blocks(kv, pos_emb, w1, w2):
    B, S, G, DH = kv.shape
    n_c = (S - CMP_BLOCK) // CMP_STRIDE + 1
    idx = np.arange(n_c)[:, None] * CMP_STRIDE + np.arange(CMP_BLOCK)[None, :]
    blk = kv[:, idx] + pos_emb[None, None, :, None, :]
    blk = jnp.moveaxis(blk, 3, 2).reshape(B, n_c, G, CMP_BLOCK * DH)
    return jax.nn.gelu(blk @ w1) @ w2


def nsa_mixer(q, k_c, v_c, k_s, v_s, k_w, v_w, gate_logits, positions, cmp_pos_emb, cmp_w1, cmp_w2):
    B, S = q.shape[:2]
    G, HPG, DH = NSA_KV_GROUPS, NSA_HEADS // NSA_KV_GROUPS, HEAD_DIM
    scale = DH ** -0.5
    n_q = S // Q_BLOCK
    q = q.reshape(B, S, NSA_HEADS, DH)
    k_c, v_c, k_s, v_s, k_w, v_w = [a.reshape(B, S, G, DH) for a in (k_c, v_c, k_s, v_s, k_w, v_w)]
    t = jnp.arange(S)

    kcmp = compress_blocks(k_c, cmp_pos_emb[0], cmp_w1[0], cmp_w2[0])
    vcmp = compress_blocks(v_c, cmp_pos_emb[1], cmp_w1[1], cmp_w2[1])
    n_c = kcmp.shape[1]
    cmp_end = jnp.arange(n_c) * CMP_STRIDE + CMP_BLOCK - 1
    cmp_mask = cmp_end[None, :] <= t[:, None]
    qg = q.reshape(B, S, G, HPG, DH)
    logits = jnp.einsum('bsghd,bcgd->bghsc', qg, kcmp) * scale
    p_cmp = masked_softmax(logits, cmp_mask)
    o_c = jnp.einsum('bghsc,bcgd->bsghd', p_cmp.astype(vcmp.dtype), vcmp).reshape(B, S, NSA_HEADS, DH)

    n_sb = S // SEL_BLOCK
    n_sel = min(N_SEL, n_sb)
    ci = np.arange(n_c)[:, None] * CMP_STRIDE
    sj = np.arange(n_sb)[None, :] * SEL_BLOCK
    overlap = jnp.asarray(((ci < sj + SEL_BLOCK) & (ci + CMP_BLOCK > sj)).astype(np.float32))
    imp = jnp.einsum('bghsc,cj->bgsj', p_cmp, overlap)
    blk_id = jnp.arange(n_sb)[None, :]
    cur = (t // SEL_BLOCK)[:, None]
    valid = blk_id * SEL_BLOCK <= t[:, None]
    forced = (blk_id == 0) | (blk_id == cur) | (blk_id == cur - 1)
    score = jnp.where(forced, imp + FORCE_BONUS, jnp.where(valid, imp, NEG))
    _, sel_idx = lax.top_k(score, n_sel)

    q_r = partial_rope(q, positions).reshape(B, S, G, HPG, DH)
    k_s = partial_rope(k_s, positions)
    k_w = partial_rope(k_w, positions)
    q_blk = jnp.moveaxis(q_r.reshape(B, n_q, Q_BLOCK, G, HPG, DH), 1, 0)

    ks_blk = k_s.reshape(B, n_sb, SEL_BLOCK, G, DH).transpose(0, 3, 1, 2, 4)
    vs_blk = v_s.reshape(B, n_sb, SEL_BLOCK, G, DH).transpose(0, 3, 1, 2, 4)
    idx_blk = jnp.moveaxis(sel_idx.reshape(B, G, n_q, Q_BLOCK, n_sel), 2, 0)
    bi = jnp.arange(B)[:, None, None, None]
    gi = jnp.arange(G)[None, :, None, None]

    def sel_step(args):
        qb, ib, i = args
        tq = i * Q_BLOCK + jnp.arange(Q_BLOCK)
        kg = ks_blk[bi, gi, ib]
        vg = vs_blk[bi, gi, ib]
        lg = jnp.einsum('bqghd,bgqnkd->bgqhnk', qb, kg) * scale
        kpos = ib[..., None] * SEL_BLOCK + jnp.arange(SEL_BLOCK)
        mask = (kpos <= tq[:, None, None])[:, :, :, None].reshape(B, G, Q_BLOCK, 1, n_sel * SEL_BLOCK)
        p = masked_softmax(lg.reshape(B, G, Q_BLOCK, HPG, n_sel * SEL_BLOCK), mask)
        p = p.reshape(B, G, Q_BLOCK, HPG, n_sel, SEL_BLOCK).astype(vg.dtype)
        return jnp.einsum('bgqhnk,bgqnkd->bqghd', p, vg)

    o_s = lax.map(sel_step, (q_blk, idx_blk, jnp.arange(n_q)))
    o_s = jnp.moveaxis(o_s, 0, 1).reshape(B, S, NSA_HEADS, DH)

    kw_pad = jnp.pad(k_w, ((0, 0), (WINDOW, 0), (0, 0), (0, 0)))
    vw_pad = jnp.pad(v_w, ((0, 0), (WINDOW, 0), (0, 0), (0, 0)))

    def win_step(args):
        qb, i = args
        start = i * Q_BLOCK
        kb = lax.dynamic_slice_in_dim(kw_pad, start, WINDOW + Q_BLOCK, axis=1)
        vb = lax.dynamic_slice_in_dim(vw_pad, start, WINDOW + Q_BLOCK, axis=1)
        kpos = start - WINDOW + jnp.arange(WINDOW + Q_BLOCK)
        tq = start + jnp.arange(Q_BLOCK)
        mask = (kpos[None, :] <= tq[:, None]) & (kpos[None, :] > tq[:, None] - WINDOW) & (kpos[None, :] >= 0)
        lg = jnp.einsum('bqghd,bkgd->bgqhk', qb, kb) * scale
        p = masked_softmax(lg, mask[:, None, :]).astype(vb.dtype)
        return jnp.einsum('bgqhk,bkgd->bqghd', p, vb)

    o_w = lax.map(win_step, (q_blk, jnp.arange(n_q)))
    o_w = jnp.moveaxis(o_w, 0, 1).reshape(B, S, NSA_HEADS, DH)

    g = jax.nn.sigmoid(gate_logits.reshape(B, S, NSA_HEADS, 3)).astype(q.dtype)
    o = g[..., 0:1] * o_c + g[..., 1:2] * o_s + g[..., 2:3] * o_w
    return o.reshape(B, S, NSA_HEADS * DH)


def fox_mixer(q, k, v, f_logit, f_bias):
    B, S = q.shape[:2]
    H, DH = FOX_HEADS, HEAD_DIM
    scale = DH ** -0.5
    n_q = S // Q_BLOCK
    q, k, v = [a.reshape(B, S, H, DH) for a in (q, k, v)]
    log_f = jax.nn.log_sigmoid(f_logit.astype(jnp.float32) + f_bias.astype(jnp.float32))
    c = lax.cumsum(log_f, axis=1).transpose(0, 2, 1)
    q_blk = jnp.moveaxis(q.reshape(B, n_q, Q_BLOCK, H, DH), 1, 0)
    c_blk = jnp.moveaxis(c.reshape(B, H, n_q, Q_BLOCK), 2, 0)
    kpos = jnp.arange(S)

    def step(args):
        qb, cq, i = args
        tq = i * Q_BLOCK + jnp.arange(Q_BLOCK)
        lg = jnp.einsum('bqhd,bshd->bhqs', qb, k).astype(jnp.float32) * scale + (cq[..., None] - c[:, :, None, :])
        p = masked_softmax(lg, kpos[None, :] <= tq[:, None]).astype(v.dtype)
        return jnp.einsum('bhqs,bshd->bqhd', p, v)

    o = lax.map(step, (q_blk, c_blk, jnp.arange(n_q)))
    return jnp.moveaxis(o, 0, 1).reshape(B, S, H * DH)


def moe(h, router_w, router_b, w1, b1, w2, b2):
    B, S, D = h.shape
    T = B * S
    xt = h.reshape(T, D)
    logits = (xt @ router_w + router_b).astype(jnp.float32)
    top_v, top_i = lax.top_k(logits, TOP_K)
    gate = jax.nn.softmax(top_v, axis=-1)
    A = T * TOP_K
    flat_e = top_i.reshape(A)
    flat_tok = jnp.repeat(jnp.arange(T, dtype=jnp.int32), TOP_K)
    flat_w = gate.reshape(A)
    order = jnp.argsort(flat_e, stable=True)
    se, stok, sw = flat_e[order], flat_tok[order], flat_w[order]
    counts = jnp.zeros((N_EXPERTS,), jnp.int32).at[flat_e].add(1)
    padded = (counts + MOE_BLOCK - 1) // MOE_BLOCK * MOE_BLOCK
    start = jnp.cumsum(counts) - counts
    pend = jnp.cumsum(padded)
    pstart = pend - padded
    dest = pstart[se] + jnp.arange(A, dtype=jnp.int32) - start[se]
    nb = -(-A // MOE_BLOCK) + N_EXPERTS
    buf_tok = jnp.full((nb * MOE_BLOCK,), T, jnp.int32).at[dest].set(stok)
    buf_w = jnp.zeros((nb * MOE_BLOCK,), jnp.float32).at[dest].set(sw)
    blk_e = jnp.minimum(jnp.searchsorted(pend, jnp.arange(nb, dtype=jnp.int32) * MOE_BLOCK, side='right'), N_EXPERTS - 1)
    xin = jnp.concatenate([xt, jnp.zeros((1, D), xt.dtype)], axis=0)[buf_tok].reshape(nb, MOE_BLOCK, D)

    def expert_block(args):
        xb, e = args
        hc = xb @ w1[e] + b1[e]
        gt = jnp.minimum(hc[:, :D_FF], SWIGLU_LIMIT)
        up = jnp.clip(hc[:, D_FF:], -SWIGLU_LIMIT, SWIGLU_LIMIT)
        glu = gt * jax.nn.sigmoid(SWIGLU_ALPHA * gt)
        return ((up + 1.0) * glu) @ w2[e] + b2[e]

    y = lax.map(expert_block, (xin, blk_e)).reshape(nb * MOE_BLOCK, D)
    y = y * buf_w[:, None].astype(y.dtype)
    out = jnp.zeros((T + 1, D), y.dtype).at[buf_tok].add(y)[:T]
    return out.reshape(B, S, D)


def setup_inputs(seed: int = 0) -> dict:
    key = jax.random.key(seed)
    ks = jax.random.split(key, 20)
    L, D, E, F = DEPTH, D_MODEL, N_EXPERTS, D_FF
    nrm = lambda k, shape: jax.random.normal(k, shape, jnp.float32)
    return {
        'x': nrm(ks[0], (BATCH, SEQ, D)),
        'positions': jax.random.randint(ks[1], (BATCH, 1), 0, 1024, jnp.int32) + jnp.arange(SEQ, dtype=jnp.int32)[None, :],
        'norm1_g': 1.0 + 0.02 * nrm(ks[2], (L, D)),
        'w_in': nrm(ks[3], (L, D, IN_COLS)) * D ** -0.5,
        'cmp_pos_emb': 0.1 * nrm(ks[4], (L, 2, CMP_BLOCK, HEAD_DIM)),
        'cmp_w1': nrm(ks[5], (L, 2, CMP_BLOCK * HEAD_DIM, HEAD_DIM)) * (CMP_BLOCK * HEAD_DIM) ** -0.5,
        'cmp_w2': nrm(ks[6], (L, 2, HEAD_DIM, HEAD_DIM)) * HEAD_DIM ** -0.5,
        'fox_f_bias': 2.0 + 0.5 * nrm(ks[7], (L, FOX_HEADS)),
        'w_proj_nsa': nrm(ks[8], (L, NSA_Q, D)) * NSA_Q ** -0.5,
        'w_proj_fox': nrm(ks[9], (L, FOX_W, D)) * FOX_W ** -0.5,
        'w_out': nrm(ks[10], (L, D, D)) * D ** -0.5,
        'norm2_g': 1.0 + 0.02 * nrm(ks[11], (L, D)),
        'router_w': nrm(ks[12], (L, D, E)) * D ** -0.5,
        'router_b': 0.01 * nrm(ks[13], (L, E)),
        'expert_w1': nrm(ks[14], (L, E, D, 2 * F)) * D ** -0.5,
        'expert_b1': 0.01 * nrm(ks[15], (L, E, 2 * F)),
        'expert_w2': nrm(ks[16], (L, E, F, D)) * F ** -0.5,
        'expert_b2': 0.01 * nrm(ks[17], (L, E, D)),
        'norm_f_g': 1.0 + 0.02 * nrm(ks[18], (D,)),
    }


def reference(x, positions, norm1_g, w_in, cmp_pos_emb, cmp_w1, cmp_w2, fox_f_bias, w_proj_nsa, w_proj_fox, w_out, norm2_g, router_w, router_b, expert_w1, expert_b1, expert_w2, expert_b2, norm_f_g):
    for l in range(DEPTH):
        h = rms_norm(x, norm1_g[l])
        (nq, nkc, nvc, nks, nvs, nkw, nvw, ngate, fq, fk, fv, ff, gate_nsa, gate_fox) = jnp.split(h @ w_in[l], IN_SPLITS, axis=-1)
        o_nsa = nsa_mixer(nq, nkc, nvc, nks, nvs, nkw, nvw, ngate, positions, cmp_pos_emb[l], cmp_w1[l], cmp_w2[l])
        o_fox = fox_mixer(fq, fk, fv, ff, fox_f_bias[l])
        mixed = jax.nn.sigmoid(gate_nsa) * (o_nsa @ w_proj_nsa[l]) + jax.nn.sigmoid(gate_fox) * (o_fox @ w_proj_fox[l])
        x = x + mixed @ w_out[l]
        x = x + moe(rms_norm(x, norm2_g[l]), router_w[l], router_b[l], expert_w1[l], expert_b1[l], expert_w2[l], expert_b2[l])
    return rms_norm(x, norm_f_g)
```

```python
import functools

import numpy as np
import jax
import jax.numpy as jnp
from jax import lax
from jax.experimental import pallas as pl
from jax.experimental.pallas import tpu as pltpu

HEAD_DIM = 64
NSA_HEADS = 8
NSA_KV_GROUPS = 2
HEADS_PER_GROUP = NSA_HEADS // NSA_KV_GROUPS
FOX_HEADS = 8
ROT_DIM = HEAD_DIM // 4
ROPE_THETA = 500000.0
CMP_BLOCK = 32
CMP_STRIDE = 16
SEL_BLOCK = 64
N_SEL = 8
WINDOW = 512
N_EXPERTS = 32
TOP_K = 4
SWIGLU_ALPHA = 1.702
SWIGLU_LIMIT = 7.0
NORM_EPS = 1e-6
NEG = -1e30
BELOW_NEG = -3e38
FORCE_BONUS = 1e4

LANES = 128
NSA_Q_TILE = 128
KV_CHUNK = 512
FOX_Q_TILE = 256
MOE_ROWS = 256
VMEM_LIMIT = 56 * 1024 * 1024

F32 = jnp.float32
BF16 = jnp.bfloat16
I32 = jnp.int32


def _dot(a, b):
    return jnp.dot(a, b, preferred_element_type=F32)


def _dot_nt(a, b):
    return lax.dot_general(a, b, (((1,), (1,)), ((), ())), preferred_element_type=F32)


def _params(*sem):
    return pltpu.CompilerParams(dimension_semantics=sem, vmem_limit_bytes=VMEM_LIMIT)


def _rms(x, g):
    return x * lax.rsqrt(jnp.mean(x * x, axis=-1, keepdims=True) + NORM_EPS) * g


def _inproj_kernel(x_ref, g_ref, w_ref, rc_ref, rs1_ref, rs2_ref, fb_ref,
                   qraw_ref, qrot_ref, kvc_ref, ks_ref, vs_ref, kw_ref, vw_ref,
                   fq_ref, fk_ref, fv_ref, sgn_ref, sgf_ref, misc_ref, *, d_model):
    scale = HEAD_DIM ** -0.5
    xn = _rms(x_ref[0], g_ref[...]).astype(BF16)
    rc, rs1, rs2 = rc_ref[0], rs1_ref[0], rs2_ref[0]

    def rope(slab):
        half = ROT_DIM // 2
        return slab * rc + pltpu.roll(slab, LANES - half, 1) * rs1 + pltpu.roll(slab, half, 1) * rs2

    def put_pair(ref, lead, slab):
        ref[lead + (0,)] = slab[:, :HEAD_DIM].astype(BF16)
        ref[lead + (1,)] = slab[:, HEAD_DIM:].astype(BF16)

    nq = NSA_HEADS * HEAD_DIM
    r = _dot(xn, w_ref[:, 0:nq]) * scale
    for s in range(nq // LANES):
        slab = r[:, LANES * s:LANES * (s + 1)]
        qraw_ref[0, 2 * s] = slab[:, :HEAD_DIM].astype(BF16)
        qraw_ref[0, 2 * s + 1] = slab[:, HEAD_DIM:].astype(BF16)
        rot = rope(slab)
        qrot_ref[0, 2 * s] = rot[:, :HEAD_DIM].astype(BF16)
        qrot_ref[0, 2 * s + 1] = rot[:, HEAD_DIM:].astype(BF16)
    off = nq

    r = _dot(xn, w_ref[:, off:off + 6 * LANES])
    put_pair(kvc_ref, (0, 0), r[:, 0:LANES])
    put_pair(kvc_ref, (0, 1), r[:, LANES:2 * LANES])
    put_pair(ks_ref, (0,), rope(r[:, 2 * LANES:3 * LANES]))
    put_pair(vs_ref, (0,), r[:, 3 * LANES:4 * LANES])
    put_pair(kw_ref, (0,), rope(r[:, 4 * LANES:5 * LANES]))
    put_pair(vw_ref, (0,), r[:, 5 * LANES:6 * LANES])
    off += 6 * LANES

    fw = FOX_HEADS * HEAD_DIM
    for ref, mul in ((fq_ref, scale), (fk_ref, 1.0), (fv_ref, 1.0)):
        r = _dot(xn, w_ref[:, off:off + fw]) * mul
        for h in range(FOX_HEADS):
            ref[0, h] = r[:, HEAD_DIM * h:HEAD_DIM * (h + 1)].astype(BF16)
        off += fw

    step = min(512, d_model)
    for ref in (sgn_ref, sgf_ref):
        for c in range(0, d_model, step):
            r = _dot(xn, w_ref[:, off + c:off + c + step])
            ref[0, :, c:c + step] = jax.nn.sigmoid(r).astype(BF16)
        off += d_model

    r = _dot(xn, w_ref[:, off:off + 2 * LANES])
    m0 = r[:, :LANES]
    lane = lax.broadcasted_iota(I32, m0.shape, 1)
    z = m0 + fb_ref[...]
    log_sig = jnp.minimum(z, 0.0) - jnp.log(1.0 + jnp.exp(-jnp.abs(z)))
    misc_ref[0, 0] = jnp.where(lane < 32, jax.nn.sigmoid(m0), log_sig)
    misc_ref[0, 1] = jax.nn.sigmoid(r[:, LANES:])


def _inproj(x, norm_g, w_all, rope_c, rope_s1, rope_s2, f_bias_row, tm):
    B, S, D = x.shape
    ncol = w_all.shape[1]
    qshape = jax.ShapeDtypeStruct((B, NSA_HEADS, S, HEAD_DIM), BF16)
    kvshape = jax.ShapeDtypeStruct((B, NSA_KV_GROUPS, S, HEAD_DIM), BF16)
    fshape = jax.ShapeDtypeStruct((B, FOX_HEADS, S, HEAD_DIM), BF16)
    qspec = pl.BlockSpec((1, NSA_HEADS, tm, HEAD_DIM), lambda b, i: (b, 0, i, 0))
    kvspec = pl.BlockSpec((1, NSA_KV_GROUPS, tm, HEAD_DIM), lambda b, i: (b, 0, i, 0))
    fspec = pl.BlockSpec((1, FOX_HEADS, tm, HEAD_DIM), lambda b, i: (b, 0, i, 0))
    tabspec = pl.BlockSpec((1, tm, LANES), lambda b, i: (b, i, 0))
    gspec = pl.BlockSpec((1, tm, D), lambda b, i: (b, i, 0))
    return pl.pallas_call(
        functools.partial(_inproj_kernel, d_model=D),
        grid=(B, S // tm),
        in_specs=[
            pl.BlockSpec((1, tm, D), lambda b, i: (b, i, 0)),
            pl.BlockSpec((1, D), lambda b, i: (0, 0)),
            pl.BlockSpec((D, ncol), lambda b, i: (0, 0)),
            tabspec, tabspec, tabspec,
            pl.BlockSpec((1, LANES), lambda b, i: (0, 0)),
        ],
        out_specs=[
            qspec, qspec,
            pl.BlockSpec((1, 2, NSA_KV_GROUPS, tm, HEAD_DIM), lambda b, i: (b, 0, 0, i, 0)),
            kvspec, kvspec, kvspec, kvspec,
            fspec, fspec, fspec,
            gspec, gspec,
            pl.BlockSpec((1, 2, tm, LANES), lambda b, i: (b, 0, i, 0)),
        ],
        out_shape=[
            qshape, qshape,
            jax.ShapeDtypeStruct((B, 2, NSA_KV_GROUPS, S, HEAD_DIM), BF16),
            kvshape, kvshape, kvshape, kvshape,
            fshape, fshape, fshape,
            jax.ShapeDtypeStruct((B, S, D), BF16), jax.ShapeDtypeStruct((B, S, D), BF16),
            jax.ShapeDtypeStruct((B, 2, S, LANES), F32),
        ],
        compiler_params=_params("parallel", "parallel"),
        name="inproj",
    )(x, norm_g, w_all, rope_c, rope_s1, rope_s2, f_bias_row)


def _compress_kernel(r_ref, pe_ref, w1_ref, w2_ref, o_ref):
    half = CMP_STRIDE * HEAD_DIM
    rows = r_ref[0, 0, 0].astype(F32)
    lo = (rows + pe_ref[0, :, :half]).astype(BF16)
    hi = (rows + pe_ref[0, :, half:]).astype(BF16)
    a = _dot(lo, w1_ref[0, :half])
    b = _dot(hi, w1_ref[0, half:])
    n = a.shape[0]
    pre = a + pltpu.roll(b, n - 1, 0)
    act = jax.nn.gelu(pre, approximate=True).astype(BF16)
    o_ref[0, 0, 0] = _dot(act, w2_ref[0]).astype(BF16)


def _compress(kv_rows, pe, w1, w2):
    B, _, G, n_rows, width = kv_rows.shape
    return pl.pallas_call(
        _compress_kernel,
        grid=(B, 2, G),
        in_specs=[
            pl.BlockSpec((1, 1, 1, n_rows, width), lambda b, t, g: (b, t, g, 0, 0)),
            pl.BlockSpec((1, 1, 2 * width), lambda b, t, g: (t, 0, 0)),
            pl.BlockSpec((1, 2 * width, LANES), lambda b, t, g: (t, 0, 0)),
            pl.BlockSpec((1, LANES, HEAD_DIM), lambda b, t, g: (t, 0, 0)),
        ],
        out_specs=pl.BlockSpec((1, 1, 1, n_rows, HEAD_DIM), lambda b, t, g: (b, t, g, 0, 0)),
        out_shape=jax.ShapeDtypeStruct((B, 2, G, n_rows, HEAD_DIM), BF16),
        compiler_params=_params("parallel", "parallel", "parallel"),
        name="nsa_compress",
    )(kv_rows, pe, w1, w2)


def _softmax_rows(logits, ok):
    l = jnp.where(ok, logits, NEG)
    m = jnp.max(l, axis=-1, keepdims=True)
    e = jnp.where(ok, jnp.exp(l - m), 0.0)
    s = jnp.sum(e, axis=-1, keepdims=True)
    return e / jnp.where(s > 0.0, s, 1.0)


def _nsa_kernel(qraw_ref, qrot_ref, kcmp_ref, vcmp_ref, ks_ref, vs_ref, kw_ref, vw_ref,
                misc_ref, expand_ref, overlap_ref, o_ref, *, seq, chunk):
    tq_n = NSA_Q_TILE
    hpg = HEADS_PER_GROUP
    n_sb = seq // SEL_BLOCK
    n_sel = min(N_SEL, n_sb)
    qi = pl.program_id(2)
    t0 = qi * tq_n
    tq = t0 + lax.broadcasted_iota(I32, (tq_n, 1), 0)

    q = qraw_ref[0].reshape(hpg * tq_n, HEAD_DIM)
    n_c = kcmp_ref.shape[3]
    lg = _dot_nt(q, kcmp_ref[0, 0, 0])
    cidx = lax.broadcasted_iota(I32, (tq_n, n_c), 1)
    cmask = cidx * CMP_STRIDE + (CMP_BLOCK - 1) <= tq
    vcmp = vcmp_ref[0, 0, 0]
    o_c = []
    p_sum = jnp.zeros((tq_n, n_c), F32)
    for h in range(hpg):
        p = _softmax_rows(lg[h * tq_n:(h + 1) * tq_n], cmask)
        p_sum = p_sum + p
        o_c.append(_dot(p.astype(BF16), vcmp))

    p_hi = p_sum.astype(BF16)
    p_lo = (p_sum - p_hi.astype(F32)).astype(BF16)
    imp = _dot(p_hi, overlap_ref[...]) + _dot(p_lo, overlap_ref[...])
    j = lax.broadcasted_iota(I32, (tq_n, LANES), 1)
    cur = tq >> (SEL_BLOCK.bit_length() - 1)
    forced = (j == 0) | (j == cur) | (j == cur - 1)
    valid = j * SEL_BLOCK <= tq
    score = jnp.where(forced, imp + FORCE_BONUS, jnp.where(valid, imp, NEG))
    score = jnp.where(j < n_sb, score, BELOW_NEG)
    beaten = jnp.zeros((tq_n, LANES), F32)
    for jp in range(n_sb):
        col = score[:, jp:jp + 1]
        beats = (col > score) | ((col == score) & (j > jp))
        beaten = beaten + jnp.where(beats, 1.0, 0.0)
    sel = jnp.where(beaten < n_sel, 1.0, 0.0).astype(BF16)

    qr = qrot_ref[0].reshape(hpg * tq_n, HEAD_DIM)
    n_chunk = (t0 + tq_n + chunk - 1) // chunk

    def sel_body(c, carry):
        ms, ls, accs = carry
        k0 = pl.multiple_of(c * chunk, chunk)
        k = ks_ref[0, 0, pl.ds(k0, chunk), :]
        v = vs_ref[0, 0, pl.ds(k0, chunk), :]
        s = _dot_nt(qr, k)
        picked = _dot(sel, expand_ref[:, pl.ds(k0, chunk)])
        kpos = k0 + lax.broadcasted_iota(I32, (tq_n, chunk), 1)
        ok = (picked > 0.5) & (kpos <= tq)
        new_ms, new_ls, new_accs = [], [], []
        for h in range(hpg):
            s_h = jnp.where(ok, s[h * tq_n:(h + 1) * tq_n], NEG)
            m_new = jnp.maximum(ms[h], jnp.max(s_h, axis=-1, keepdims=True))
            alpha = jnp.exp(ms[h] - m_new)
            p = jnp.where(ok, jnp.exp(s_h - m_new), 0.0)
            new_ls.append(alpha * ls[h] + jnp.sum(p, axis=-1, keepdims=True))
            new_accs.append(alpha * accs[h] + _dot(p.astype(BF16), v))
            new_ms.append(m_new)
        return tuple(new_ms), tuple(new_ls), tuple(new_accs)

    init = (tuple(jnp.full((tq_n, 1), NEG, F32) for _ in range(hpg)),
            tuple(jnp.zeros((tq_n, 1), F32) for _ in range(hpg)),
            tuple(jnp.zeros((tq_n, HEAD_DIM), F32) for _ in range(hpg)))
    _, ls, accs = lax.fori_loop(0, n_chunk, sel_body, init)

    wlen = WINDOW + tq_n
    w0 = pl.multiple_of(jnp.maximum(t0 - WINDOW, 0), tq_n)
    kw = kw_ref[0, 0, pl.ds(w0, wlen), :]
    vw = vw_ref[0, 0, pl.ds(w0, wlen), :]
    sw = _dot_nt(qr, kw)
    kpos = w0 + lax.broadcasted_iota(I32, (tq_n, wlen), 1)
    wok = (kpos <= tq) & (kpos > tq - WINDOW)

    g = misc_ref[0, 0]
    for h in range(hpg):
        pw = _softmax_rows(sw[h * tq_n:(h + 1) * tq_n], wok)
        o_w = _dot(pw.astype(BF16), vw)
        o_s = accs[h] / ls[h]
        o_h = g[:, 3 * h:3 * h + 1] * o_c[h] + g[:, 3 * h + 1:3 * h + 2] * o_s + g[:, 3 * h + 2:3 * h + 3] * o_w
        o_ref[0, :, HEAD_DIM * h:HEAD_DIM * (h + 1)] = o_h.astype(BF16)


def _nsa(qraw, qrot, cmp_kv, ks, vs, kw, vw, misc, expand, overlap):
    B, _, S, _ = qraw.shape
    G = NSA_KV_GROUPS
    n_c = cmp_kv.shape[3]
    chunk = min(KV_CHUNK, S)
    qspec = pl.BlockSpec((1, HEADS_PER_GROUP, NSA_Q_TILE, HEAD_DIM), lambda b, g, i: (b, g, i, 0))
    kvspec = pl.BlockSpec((1, 1, S, HEAD_DIM), lambda b, g, i: (b, g, 0, 0))
    return pl.pallas_call(
        functools.partial(_nsa_kernel, seq=S, chunk=chunk),
        grid=(B, G, S // NSA_Q_TILE),
        in_specs=[
            qspec, qspec,
            pl.BlockSpec((1, 1, 1, n_c, HEAD_DIM), lambda b, g, i: (b, 0, g, 0, 0)),
            pl.BlockSpec((1, 1, 1, n_c, HEAD_DIM), lambda b, g, i: (b, 1, g, 0, 0)),
            kvspec, kvspec, kvspec, kvspec,
            pl.BlockSpec((1, 1, NSA_Q_TILE, LANES), lambda b, g, i: (b, g, i, 0)),
            pl.BlockSpec((LANES, S), lambda b, g, i: (0, 0)),
            pl.BlockSpec((n_c, LANES), lambda b, g, i: (0, 0)),
        ],
        out_specs=pl.BlockSpec((1, NSA_Q_TILE, HEADS_PER_GROUP * HEAD_DIM), lambda b, g, i: (b, i, g)),
        out_shape=jax.ShapeDtypeStruct((B, S, NSA_HEADS * HEAD_DIM), BF16),
        compiler_params=_params("parallel", "parallel", "arbitrary"),
        name="nsa_attention",
    )(qraw, qrot, cmp_kv, cmp_kv, ks, vs, kw, vw, misc, expand, overlap)


def _cumsum_kernel(f_ref, c_ref, *, seq, width):
    row = lax.broadcasted_iota(I32, (width, width), 0)
    col = lax.broadcasted_iota(I32, (width, width), 1)
    upper = jnp.where(row <= col, 1.0, 0.0).astype(BF16)
    carry = jnp.zeros((FOX_HEADS, 1), F32)
    for s in range(seq // width):
        blk = f_ref[0, :, s * width:(s + 1) * width]
        t1 = blk.astype(BF16)
        r1 = blk - t1.astype(F32)
        t2 = r1.astype(BF16)
        t3 = (r1 - t2.astype(F32)).astype(BF16)
        cs = _dot(t1, upper) + _dot(t2, upper) + _dot(t3, upper) + carry
        c_ref[0, :, s * width:(s + 1) * width] = cs
        carry = cs[:, width - 1:width]


def _fox_cumsum(log_f):
    B, H, S = log_f.shape
    width = min(256, S)
    return pl.pallas_call(
        functools.partial(_cumsum_kernel, seq=S, width=width),
        grid=(B,),
        in_specs=[pl.BlockSpec((1, H, S), lambda b: (b, 0, 0))],
        out_specs=pl.BlockSpec((1, H, S), lambda b: (b, 0, 0)),
        out_shape=jax.ShapeDtypeStruct((B, H, S), F32),
        compiler_params=_params("parallel"),
        name="fox_cumsum",
    )(log_f)


def _fox_kernel(q_ref, k_ref, v_ref, c_ref, o_ref, o_scr, *, chunk):
    tq_n = q_ref.shape[2]
    qi = pl.program_id(1)
    t0 = pl.multiple_of(qi * tq_n, tq_n)
    tq = t0 + lax.broadcasted_iota(I32, (tq_n, 1), 0)
    n_chunk = (t0 + tq_n + chunk - 1) // chunk

    def head(h, _):
        q = q_ref[0, h]
        c0 = c_ref[0, h, :, pl.ds(t0, LANES)][:, 0:1]

        def body(c, carry):
            m, l, acc = carry
            k0 = pl.multiple_of(c * chunk, chunk)
            k = k_ref[0, h, pl.ds(k0, chunk), :]
            v = v_ref[0, h, pl.ds(k0, chunk), :]
            s = _dot_nt(q, k) + (c0 - c_ref[0, h, :, pl.ds(k0, chunk)])
            kpos = k0 + lax.broadcasted_iota(I32, (tq_n, chunk), 1)
            ok = kpos <= tq
            s = jnp.where(ok, s, NEG)
            m_new = jnp.maximum(m, jnp.max(s, axis=-1, keepdims=True))
            alpha = jnp.exp(m - m_new)
            p = jnp.where(ok, jnp.exp(s - m_new), 0.0)
            l = alpha * l + jnp.sum(p, axis=-1, keepdims=True)
            acc = alpha * acc + _dot(p.astype(BF16), v)
            return m_new, l, acc

        init = (jnp.full((tq_n, 1), NEG, F32), jnp.zeros((tq_n, 1), F32), jnp.zeros((tq_n, HEAD_DIM), F32))
        _, l, acc = lax.fori_loop(0, n_chunk, body, init)
        o_scr[h] = acc / l
        return 0

    lax.fori_loop(0, FOX_HEADS, head, 0)
    for h in range(FOX_HEADS):
        o_ref[0, :, HEAD_DIM * h:HEAD_DIM * (h + 1)] = o_scr[h].astype(BF16)


def _fox(fq, fk, fv, c):
    B, H, S, _ = fq.shape
    tq_n = min(FOX_Q_TILE, S)
    chunk = min(KV_CHUNK, S)
    return pl.pallas_call(
        functools.partial(_fox_kernel, chunk=chunk),
        grid=(B, S // tq_n),
        in_specs=[
            pl.BlockSpec((1, H, tq_n, HEAD_DIM), lambda b, i: (b, 0, i, 0)),
            pl.BlockSpec((1, H, S, HEAD_DIM), lambda b, i: (b, 0, 0, 0)),
            pl.BlockSpec((1, H, S, HEAD_DIM), lambda b, i: (b, 0, 0, 0)),
            pl.BlockSpec((1, H, 1, S), lambda b, i: (b, 0, 0, 0)),
        ],
        out_specs=pl.BlockSpec((1, tq_n, H * HEAD_DIM), lambda b, i: (b, i, 0)),
        out_shape=jax.ShapeDtypeStruct((B, S, H * HEAD_DIM), BF16),
        scratch_shapes=[pltpu.VMEM((H, tq_n, HEAD_DIM), F32)],
        compiler_params=_params("parallel", "arbitrary"),
        name="fox_attention",
    )(fq, fk, fv, c)


def _merge_kernel(x_ref, on_ref, of_ref, sgn_ref, sgf_ref, wn_ref, wf_ref, wo_ref, g2_ref,
                  wrh_ref, wrl_ref, br_ref, x1_ref, h2_ref, ri_ref, rw_ref, cnt_ref, carry_scr):
    i = pl.program_id(0)
    tm = x_ref.shape[0]

    @pl.when(i == 0)
    def _():
        carry_scr[...] = jnp.zeros_like(carry_scr)

    a = _dot(on_ref[...], wn_ref[...])
    b = _dot(of_ref[...], wf_ref[...])
    mixed = (sgn_ref[...].astype(F32) * a + sgf_ref[...].astype(F32) * b).astype(BF16)
    x1 = x_ref[...] + _dot(mixed, wo_ref[...])
    x1_ref[...] = x1
    h2 = _rms(x1, g2_ref[...])
    h2_ref[...] = h2

    hh = h2.astype(BF16)
    hl = (h2 - hh.astype(F32)).astype(BF16)
    logits = _dot(hh, wrh_ref[...]) + _dot(hl, wrh_ref[...]) + _dot(hh, wrl_ref[...]) + br_ref[...]
    lane = lax.broadcasted_iota(I32, (tm, LANES), 1)
    lane_f = lane.astype(F32)
    l = jnp.where(lane < N_EXPERTS, logits, BELOW_NEG)
    idxs, vals, hots = [], [], []
    for _ in range(TOP_K):
        m = jnp.max(l, axis=-1, keepdims=True)
        idx = jnp.min(jnp.where(l == m, lane_f, float(LANES)), axis=-1, keepdims=True)
        hot = lane_f == idx
        idxs.append(idx)
        vals.append(m)
        hots.append(hot)
        l = jnp.where(hot, BELOW_NEG, l)
    exps = [jnp.exp(v - vals[0]) for v in vals]
    den = exps[0]
    for e in exps[1:]:
        den = den + e

    chosen = jnp.zeros((tm, LANES), F32)
    for hot in hots:
        chosen = chosen + jnp.where(hot, 1.0, 0.0)
    r_i = lax.broadcasted_iota(I32, (tm, tm), 0)
    c_i = lax.broadcasted_iota(I32, (tm, tm), 1)
    earlier = jnp.where(c_i < r_i, 1.0, 0.0).astype(BF16)
    before = _dot(earlier, chosen.astype(BF16)) + carry_scr[0:1, :]
    carry_scr[...] = carry_scr[...] + jnp.sum(chosen, axis=0, keepdims=True)
    cnt_ref[...] = carry_scr[...]

    ri = jnp.zeros((tm, LANES), F32)
    rw = jnp.zeros((tm, LANES), F32)
    for k in range(TOP_K):
        rank = jnp.sum(jnp.where(hots[k], before, 0.0), axis=-1, keepdims=True)
        ri = jnp.where(lane == k, idxs[k], ri)
        ri = jnp.where(lane == TOP_K + k, rank, ri)
        rw = jnp.where(lane == k, exps[k] / den, rw)
    ri_ref[...] = ri.astype(I32)
    rw_ref[...] = rw


def _merge(x2d, o_nsa, o_fox, sgn, sgf, wn, wf, wo, g2, wr_hi, wr_lo, br, tm):
    T, D = x2d.shape
    wq = o_nsa.shape[1]
    row = lambda w: pl.BlockSpec((tm, w), lambda i: (i, 0))
    full = lambda a: pl.BlockSpec(a.shape, lambda i: (0,) * a.ndim)
    return pl.pallas_call(
        _merge_kernel,
        grid=(T // tm,),
        in_specs=[row(D), row(wq), row(wq), row(D), row(D), full(wn), full(wf), full(wo), full(g2),
                  full(wr_hi), full(wr_lo), full(br)],
        out_specs=[row(D), row(D), row(LANES), row(LANES), pl.BlockSpec((8, LANES), lambda i: (0, 0))],
        out_shape=[
            jax.ShapeDtypeStruct((T, D), F32), jax.ShapeDtypeStruct((T, D), F32),
            jax.ShapeDtypeStruct((T, LANES), I32), jax.ShapeDtypeStruct((T, LANES), F32),
            jax.ShapeDtypeStruct((8, LANES), F32),
        ],
        scratch_shapes=[pltpu.VMEM((8, LANES), F32)],
        compiler_params=_params("arbitrary"),
        name="merge_router",
    )(x2d, o_nsa, o_fox, sgn, sgf, wn, wf, wo, g2, wr_hi, wr_lo, br)


def _dest_kernel(ri_ref, cnt_ref, dest_ref, tab_ref):
    tm = ri_ref.shape[0]
    shift = MOE_ROWS.bit_length() - 1
    cnt = cnt_ref[...].astype(I32)
    padded = ((cnt + (MOE_ROWS - 1)) >> shift) << shift
    lane8 = lax.broadcasted_iota(I32, (8, LANES), 1)
    pend = padded
    sh = 1
    while sh < LANES:
        pend = pend + jnp.where(lane8 >= sh, pltpu.roll(pend, sh, 1), 0)
        sh *= 2
    pstart = pend - padded
    row8 = lax.broadcasted_iota(I32, (8, LANES), 0)
    tab_ref[...] = jnp.where(row8 == 0, cnt, jnp.where(row8 == 1, padded, jnp.where(row8 == 2, pstart, pend)))

    ri = ri_ref[...]
    lane = lax.broadcasted_iota(I32, (tm, LANES), 1)
    start_row = pstart[0:1, :].astype(F32)
    dest = jnp.zeros((tm, LANES), F32)
    for k in range(TOP_K):
        hot = lane == ri[:, k:k + 1]
        base = jnp.sum(jnp.where(hot, start_row, 0.0), axis=-1, keepdims=True)
        dest = jnp.where(lane == k, base + ri[:, TOP_K + k:TOP_K + k + 1].astype(F32), dest)
    dest_ref[...] = dest.astype(I32)


def _dest(ri, cnt, tm):
    T = ri.shape[0]
    return pl.pallas_call(
        _dest_kernel,
        grid=(T // tm,),
        in_specs=[pl.BlockSpec((tm, LANES), lambda i: (i, 0)), pl.BlockSpec((8, LANES), lambda i: (0, 0))],
        out_specs=[pl.BlockSpec((tm, LANES), lambda i: (i, 0)), pl.BlockSpec((8, LANES), lambda i: (0, 0))],
        out_shape=[jax.ShapeDtypeStruct((T, LANES), I32), jax.ShapeDtypeStruct((8, LANES), I32)],
        compiler_params=_params("arbitrary"),
        name="moe_dest",
    )(ri, cnt)


def _pad_copies(tab_ref, zbuf, xin_hbm, zsem):
    sublanes = 8
    out = []
    for e in range(N_EXPERTS):
        cnt = tab_ref[e]
        pad = tab_ref[LANES + e] - cnt
        base = tab_ref[2 * LANES + e] + cnt
        piece = 1
        while piece < MOE_ROWS:
            cond = (pad & piece) != 0
            if piece < sublanes:
                for r in range(piece):
                    out.append((cond, pltpu.make_async_copy(
                        zbuf.at[pl.ds(0, 1)], xin_hbm.at[pl.ds(base + r, 1)], zsem)))
            else:
                out.append((cond, pltpu.make_async_copy(
                    zbuf.at[pl.ds(0, piece)], xin_hbm.at[pl.ds(pl.multiple_of(base, sublanes), piece)], zsem)))
            base = base + jnp.where(cond, piece, 0)
            piece *= 2
    last = N_EXPERTS - 1
    n_live = (tab_ref[2 * LANES + last] + tab_ref[LANES + last]) // MOE_ROWS
    n_blocks = xin_hbm.shape[0] // MOE_ROWS
    for blk in range(n_blocks - N_EXPERTS, n_blocks):
        out.append((blk >= n_live, pltpu.make_async_copy(
            zbuf, xin_hbm.at[pl.ds(blk * MOE_ROWS, MOE_ROWS)], zsem)))
    return out


def _dispatch_kernel(tab_ref, dest_hbm, h2_ref, xin_hbm, idx_smem, zbuf, sem, zsem, isem):
    i = pl.program_id(0)
    tm = h2_ref.shape[0]
    idx_cp = pltpu.make_async_copy(dest_hbm.at[i], idx_smem, isem)
    idx_cp.start()

    @pl.when(i == 0)
    def _():
        zbuf[...] = jnp.zeros_like(zbuf)
        for cond, cp in _pad_copies(tab_ref, zbuf, xin_hbm, zsem):
            @pl.when(cond)
            def _():
                cp.start()

    idx_cp.wait()

    def issue(t, _):
        for k in range(TOP_K):
            d = idx_smem[TOP_K * t + k]
            pltpu.make_async_copy(h2_ref.at[pl.ds(t, 1)], xin_hbm.at[pl.ds(d, 1)], sem).start()
        return 0

    lax.fori_loop(0, tm, issue, 0)
    for _ in range(TOP_K):
        pltpu.make_async_copy(h2_ref, xin_hbm.at[pl.ds(0, tm)], sem).wait()

    @pl.when(i == 0)
    def _():
        for cond, cp in _pad_copies(tab_ref, zbuf, xin_hbm, zsem):
            @pl.when(cond)
            def _():
                cp.wait()


def _dispatch(tab_flat, dest_tiles, h2, n_rows, tm):
    T, D = h2.shape
    return pl.pallas_call(
        _dispatch_kernel,
        grid_spec=pltpu.PrefetchScalarGridSpec(
            num_scalar_prefetch=1,
            grid=(T // tm,),
            in_specs=[pl.BlockSpec(memory_space=pl.ANY), pl.BlockSpec((tm, D), lambda i, tab: (i, 0))],
            out_specs=pl.BlockSpec(memory_space=pl.ANY),
            scratch_shapes=[
                pltpu.SMEM((TOP_K * tm,), I32),
                pltpu.VMEM((MOE_ROWS, D), F32),
                pltpu.SemaphoreType.DMA, pltpu.SemaphoreType.DMA, pltpu.SemaphoreType.DMA,
            ],
        ),
        out_shape=jax.ShapeDtypeStruct((n_rows, D), F32),
        compiler_params=_params("arbitrary"),
        name="moe_dispatch",
    )(tab_flat, dest_tiles, h2)


def _expert_kernel(be_ref, bs_ref, nv_ref, x_ref, w1_ref, b1_ref, w2_ref, b2_ref, y_ref, w1b, w2b, *, d_ff):
    s = pl.program_id(0)
    live = s < nv_ref[0]
    fresh = (s == 0) | (be_ref[s] != be_ref[jnp.maximum(s - 1, 0)])

    @pl.when(live & fresh)
    def _():
        w1b[...] = w1_ref[0].astype(BF16)
        w2b[...] = w2_ref[0].astype(BF16)

    @pl.when(live)
    def _():
        hc = _dot(x_ref[...].astype(BF16), w1b[...]) + b1_ref[0]
        gt = jnp.minimum(hc[:, :d_ff], SWIGLU_LIMIT)
        up = jnp.clip(hc[:, d_ff:], -SWIGLU_LIMIT, SWIGLU_LIMIT)
        glu = gt * jax.nn.sigmoid(SWIGLU_ALPHA * gt)
        act = ((up + 1.0) * glu).astype(BF16)
        y_ref[...] = _dot(act, w2b[...]) + b2_ref[0]

    @pl.when(jnp.logical_not(live))
    def _():
        y_ref[...] = jnp.zeros_like(y_ref)


def _experts(blk_e, blk_src, n_live, xin, w1, b1, w2, b2):
    n_rows, D = xin.shape
    E, _, two_f = w1.shape
    d_ff = two_f // 2
    return pl.pallas_call(
        functools.partial(_expert_kernel, d_ff=d_ff),
        grid_spec=pltpu.PrefetchScalarGridSpec(
            num_scalar_prefetch=3,
            grid=(n_rows // MOE_ROWS,),
            in_specs=[
                pl.BlockSpec((MOE_ROWS, D), lambda s, be, bs, nv: (bs[s], 0)),
                pl.BlockSpec((1, D, two_f), lambda s, be, bs, nv: (be[s], 0, 0)),
                pl.BlockSpec((1, 1, two_f), lambda s, be, bs, nv: (be[s], 0, 0)),
                pl.BlockSpec((1, d_ff, D), lambda s, be, bs, nv: (be[s], 0, 0)),
                pl.BlockSpec((1, 1, D), lambda s, be, bs, nv: (be[s], 0, 0)),
            ],
            out_specs=pl.BlockSpec((MOE_ROWS, D), lambda s, be, bs, nv: (s, 0)),
            scratch_shapes=[pltpu.VMEM((D, two_f), BF16), pltpu.VMEM((d_ff, D), BF16)],
        ),
        out_shape=jax.ShapeDtypeStruct((n_rows, D), F32),
        compiler_params=_params("arbitrary"),
        name="moe_experts",
    )(blk_e, blk_src, n_live, xin, w1, b1, w2, b2)


def _combine_kernel(dest_hbm, x1_ref, rw_ref, gf_ref, y_hbm, o_ref, idx_smem, ybuf, sem, isem):
    i = pl.program_id(0)
    tm = x1_ref.shape[0]
    idx_cp = pltpu.make_async_copy(dest_hbm.at[i], idx_smem, isem)
    idx_cp.start()
    idx_cp.wait()

    def issue(t, _):
        for k in range(TOP_K):
            d = idx_smem[TOP_K * t + k]
            pltpu.make_async_copy(y_hbm.at[pl.ds(d, 1)], ybuf.at[k, pl.ds(t, 1)], sem).start()
        return 0

    lax.fori_loop(0, tm, issue, 0)
    for k in range(TOP_K):
        pltpu.make_async_copy(y_hbm.at[pl.ds(0, tm)], ybuf.at[k], sem).wait()

    rw = rw_ref[...]
    x2 = x1_ref[...]
    for k in range(TOP_K):
        x2 = x2 + rw[:, k:k + 1] * ybuf[k]
    o_ref[...] = _rms(x2, gf_ref[...])


def _combine(dest_tiles, x1, rw, gf, y, tm):
    T, D = x1.shape
    return pl.pallas_call(
        _combine_kernel,
        grid=(T // tm,),
        in_specs=[
            pl.BlockSpec(memory_space=pl.ANY),
            pl.BlockSpec((tm, D), lambda i: (i, 0)),
            pl.BlockSpec((tm, LANES), lambda i: (i, 0)),
            pl.BlockSpec((1, D), lambda i: (0, 0)),
            pl.BlockSpec(memory_space=pl.ANY),
        ],
        out_specs=pl.BlockSpec((tm, D), lambda i: (i, 0)),
        out_shape=jax.ShapeDtypeStruct((T, D), F32),
        scratch_shapes=[
            pltpu.SMEM((TOP_K * tm,), I32),
            pltpu.VMEM((TOP_K, tm, D), F32),
            pltpu.SemaphoreType.DMA, pltpu.SemaphoreType.DMA,
        ],
        compiler_params=_params("arbitrary"),
        name="moe_combine",
    )(dest_tiles, x1, rw, gf, y)


def _rope_tables(positions):
    half = ROT_DIM // 2
    freqs = ROPE_THETA ** (-jnp.arange(0, ROT_DIM, 2, dtype=F32) / ROT_DIM)
    ang = positions.astype(F32)[..., None] * freqs
    cos, sin = jnp.cos(ang), jnp.sin(ang)
    d = np.arange(LANES) % HEAD_DIM
    f_idx = d % half
    in_lo = jnp.asarray(d < half)
    in_hi = jnp.asarray((d >= half) & (d < ROT_DIM))
    cos_l, sin_l = cos[..., f_idx], sin[..., f_idx]
    c = jnp.where(in_lo | in_hi, cos_l, 1.0)
    s1 = jnp.where(in_lo, -sin_l, 0.0)
    s2 = jnp.where(in_hi, sin_l, 0.0)
    return c, s1, s2


def _arrange_w_in(w_in, d_model):
    nq, nkv, ng = NSA_HEADS * HEAD_DIM, NSA_KV_GROUPS * HEAD_DIM, NSA_HEADS * 3
    fw = FOX_HEADS * HEAD_DIM
    sizes = [nq] + [nkv] * 6 + [ng, fw, fw, fw, FOX_HEADS, d_model, d_model]
    offs = np.concatenate([[0], np.cumsum(sizes)])
    piece = lambda k: w_in[:, offs[k]:offs[k + 1]]
    gates, ff = piece(7), piece(11)
    per_group = HEADS_PER_GROUP * 3
    zeros = lambda n: jnp.zeros((w_in.shape[0], n), w_in.dtype)
    misc0 = jnp.concatenate([gates[:, :per_group], zeros(32 - per_group), ff, zeros(LANES - 32 - FOX_HEADS)], axis=1)
    misc1 = jnp.concatenate([gates[:, per_group:], zeros(LANES - per_group)], axis=1)
    cols = [piece(k) for k in range(7)] + [piece(8), piece(9), piece(10), piece(12), piece(13), misc0, misc1]
    return jnp.concatenate(cols, axis=1).astype(BF16)


def _split_bf16(a):
    hi = a.astype(BF16)
    return hi, (a - hi.astype(F32)).astype(BF16)


def kernel(x, positions, norm1_g, w_in, cmp_pos_emb, cmp_w1, cmp_w2, fox_f_bias, w_proj_nsa, w_proj_fox, w_out,
           norm2_g, router_w, router_b, expert_w1, expert_b1, expert_w2, expert_b2, norm_f_g):
    B, S, D = x.shape
    T = B * S
    depth = norm1_g.shape[0]
    assert depth == 1, "the combine kernel fuses the output norm, so it must follow the only layer"
    assert S % KV_CHUNK == 0 or S < KV_CHUNK
    assert S // SEL_BLOCK <= LANES and S % (CMP_STRIDE * 8) == 0
    tm_proj = min(512, S)
    tm_tok = min(512, T)
    tm_comb = min(256, T)

    rope_c, rope_s1, rope_s2 = _rope_tables(positions)
    n_sb = S // SEL_BLOCK
    n_rows16 = S // CMP_STRIDE
    ci = np.arange(n_rows16)[:, None] * CMP_STRIDE
    sj = np.arange(LANES)[None, :] * SEL_BLOCK
    overlap = jnp.asarray(((ci < sj + SEL_BLOCK) & (ci + CMP_BLOCK > sj) & (np.arange(LANES)[None, :] < n_sb)
                           & (np.arange(n_rows16)[:, None] < n_rows16 - 1)).astype(np.float32), BF16)
    expand = jnp.asarray((np.arange(S)[None, :] // SEL_BLOCK == np.arange(LANES)[:, None]).astype(np.float32), BF16)

    for l in range(depth):
        w_all = _arrange_w_in(w_in[l], D)
        fb_row = jnp.zeros((1, LANES), F32).at[0, 32:32 + FOX_HEADS].set(fox_f_bias[l].astype(F32))
        (qraw, qrot, kvc, ks, vs, kw, vw, fq, fk, fv, sgn, sgf, misc) = _inproj(
            x, norm1_g[l][None, :], w_all, rope_c, rope_s1, rope_s2, fb_row, tm_proj)

        kv_rows = kvc.reshape(B, 2, NSA_KV_GROUPS, n_rows16, CMP_STRIDE * HEAD_DIM)
        pe = cmp_pos_emb[l].reshape(2, 1, CMP_BLOCK * HEAD_DIM).astype(F32)
        cw1 = jnp.pad(cmp_w1[l], ((0, 0), (0, 0), (0, LANES - HEAD_DIM))).astype(BF16)
        cw2 = jnp.pad(cmp_w2[l], ((0, 0), (0, LANES - HEAD_DIM), (0, 0))).astype(BF16)
        cmp_kv = _compress(kv_rows, pe, cw1, cw2)

        o_nsa = _nsa(qraw, qrot, cmp_kv, ks, vs, kw, vw, misc, expand, overlap)

        log_f = jnp.transpose(misc[:, 0, :, 32:32 + FOX_HEADS], (0, 2, 1))
        c = _fox_cumsum(log_f).reshape(B, FOX_HEADS, 1, S)
        o_fox = _fox(fq, fk, fv, c)

        wr = jnp.pad(router_w[l], ((0, 0), (0, LANES - N_EXPERTS)))
        wr_hi, wr_lo = _split_bf16(wr)
        br = jnp.pad(router_b[l], (0, LANES - N_EXPERTS))[None, :].astype(F32)
        x1, h2, ri, rw, cnt = _merge(
            x.reshape(T, D), o_nsa.reshape(T, -1), o_fox.reshape(T, -1), sgn.reshape(T, D), sgf.reshape(T, D),
            w_proj_nsa[l].astype(BF16), w_proj_fox[l].astype(BF16), w_out[l].astype(BF16),
            norm2_g[l][None, :], wr_hi, wr_lo, br, tm_tok)

        dest, tab = _dest(ri, cnt, tm_tok)
        dest4 = dest[:, :TOP_K]
        pend = tab[3, :N_EXPERTS]
        n_blocks = (T * TOP_K) // MOE_ROWS + N_EXPERTS
        starts = jnp.arange(n_blocks, dtype=I32) * MOE_ROWS
        n_live = (pend[N_EXPERTS - 1] // MOE_ROWS).astype(I32)
        blk_src = jnp.minimum(jnp.arange(n_blocks, dtype=I32), n_live - 1)
        blk_e = jnp.minimum(jnp.searchsorted(pend, starts, side='right'), N_EXPERTS - 1).astype(I32)
        blk_e = blk_e[blk_src]

        xin = _dispatch(tab[:3].reshape(-1), dest4.reshape(T // tm_tok, TOP_K * tm_tok), h2,
                        n_blocks * MOE_ROWS, tm_tok)
        y = _experts(blk_e, blk_src, n_live[None], xin, expert_w1[l], expert_b1[l][:, None, :],
                     expert_w2[l], expert_b2[l][:, None, :])
        out = _combine(dest4.reshape(T // tm_comb, TOP_K * tm_comb), x1, rw, norm_f_g[None, :], y, tm_comb)
        x = out.reshape(B, S, D)
    return x
```

```python
import functools

import numpy as np
import jax
import jax.numpy as jnp
from jax import lax
from jax.experimental import pallas as pl
from jax.experimental.pallas import tpu as pltpu

HEAD_DIM = 64
NSA_HEADS = 8
NSA_KV_GROUPS = 2
HEADS_PER_GROUP = NSA_HEADS // NSA_KV_GROUPS
FOX_HEADS = 8
ROT_DIM = HEAD_DIM // 4
ROPE_THETA = 500000.0
CMP_BLOCK = 32
CMP_STRIDE = 16
SEL_BLOCK = 64
N_SEL = 8
WINDOW = 512
N_EXPERTS = 32
TOP_K = 4
SWIGLU_ALPHA = 1.702
SWIGLU_LIMIT = 7.0
NORM_EPS = 1e-6
NEG = -1e30
BELOW_NEG = -3e38
FORCE_BONUS = 1e4

LANES = 128
SUBLANES = 8
NSA_Q_TILE = 128
KV_CHUNK = 512
FOX_TILE = 256
MOE_ROWS = 256
FGATE_LANE = 32
VMEM_LIMIT = 56 * 1024 * 1024

F32 = jnp.float32
BF16 = jnp.bfloat16
I32 = jnp.int32


def _dot(a, b):
    return jnp.dot(a, b, preferred_element_type=F32)


def _dot_nt(a, b):
    return lax.dot_general(a, b, (((1,), (1,)), ((), ())), preferred_element_type=F32)


def _params(*sem):
    return pltpu.CompilerParams(dimension_semantics=sem, vmem_limit_bytes=VMEM_LIMIT)


def _rms(x, g):
    return x * lax.rsqrt(jnp.mean(x * x, axis=-1, keepdims=True) + NORM_EPS) * g


def _split3(a):
    t1 = a.astype(BF16)
    r1 = a - t1.astype(F32)
    t2 = r1.astype(BF16)
    t3 = (r1 - t2.astype(F32)).astype(BF16)
    return t1, t2, t3


def _inproj_kernel(x_ref, g_ref, w_ref, rc_ref, rs1_ref, rs2_ref, fb_ref,
                   qraw_ref, qrot_ref, kvc_ref, ks_ref, vs_ref, kw_ref, vw_ref,
                   fq_ref, fk_ref, fv_ref, sgn_ref, sgf_ref, misc_ref, csum_scr, *, d_model):
    scale = HEAD_DIM ** -0.5
    i = pl.program_id(1)
    tm = x_ref.shape[1]
    xn = _rms(x_ref[0], g_ref[...]).astype(BF16)
    rc, rs1, rs2 = rc_ref[0], rs1_ref[0], rs2_ref[0]
    lane = lax.broadcasted_iota(I32, (tm, LANES), 1)
    low = lane < HEAD_DIM
    one_at_64 = jnp.where(lane == HEAD_DIM, 1.0, 0.0)

    def rope(slab):
        half = ROT_DIM // 2
        return slab * rc + pltpu.roll(slab, LANES - half, 1) * rs1 + pltpu.roll(slab, half, 1) * rs2

    def put_pair(ref, lead, slab, tail):
        ref[lead + (0,)] = jnp.where(low, slab, tail).astype(BF16)
        ref[lead + (1,)] = jnp.where(low, pltpu.roll(slab, HEAD_DIM, 1), tail).astype(BF16)

    misc_off = w_ref.shape[1] - 2 * LANES
    r = _dot(xn, w_ref[:, misc_off:])
    m0 = r[:, :LANES]
    z = m0 + fb_ref[...]
    log_f = jnp.minimum(z, 0.0) - jnp.log(1.0 + jnp.exp(-jnp.abs(z)))
    misc_ref[0, 0] = jax.nn.sigmoid(m0)
    misc_ref[0, 1] = jax.nn.sigmoid(r[:, LANES:])

    @pl.when(i == 0)
    def _():
        csum_scr[...] = jnp.zeros_like(csum_scr)

    in_gate = (lane >= FGATE_LANE) & (lane < FGATE_LANE + FOX_HEADS)
    t1, t2, t3 = _split3(jnp.where(in_gate, log_f, 0.0))
    r_i = lax.broadcasted_iota(I32, (tm, tm), 0)
    c_i = lax.broadcasted_iota(I32, (tm, tm), 1)
    upto = jnp.where(c_i <= r_i, 1.0, 0.0).astype(BF16)
    csum = _dot(upto, t1) + _dot(upto, t2) + _dot(upto, t3) + csum_scr[0:1, :]
    csum_scr[...] = jnp.broadcast_to(csum[tm - 1:tm, :], csum_scr.shape)

    nq = NSA_HEADS * HEAD_DIM
    r = _dot(xn, w_ref[:, 0:nq]) * scale
    for s in range(nq // LANES):
        slab = r[:, LANES * s:LANES * (s + 1)]
        qraw_ref[0, 2 * s] = slab[:, :HEAD_DIM].astype(BF16)
        qraw_ref[0, 2 * s + 1] = slab[:, HEAD_DIM:].astype(BF16)
        put_pair(qrot_ref, (0, s), rope(slab), 0.0)
    off = nq

    r = _dot(xn, w_ref[:, off:off + 6 * LANES])
    for t in range(2):
        slab = r[:, t * LANES:(t + 1) * LANES]
        kvc_ref[0, t, 0] = slab[:, :HEAD_DIM].astype(BF16)
        kvc_ref[0, t, 1] = slab[:, HEAD_DIM:].astype(BF16)
    tok = i * tm + lax.broadcasted_iota(I32, (tm, LANES), 0)
    block_hot = jnp.where(lane - HEAD_DIM == (tok >> (SEL_BLOCK.bit_length() - 1)), 1.0, 0.0)
    put_pair(ks_ref, (0,), rope(r[:, 2 * LANES:3 * LANES]), block_hot)
    put_pair(vs_ref, (0,), r[:, 3 * LANES:4 * LANES], one_at_64)
    put_pair(kw_ref, (0,), rope(r[:, 4 * LANES:5 * LANES]), 0.0)
    put_pair(vw_ref, (0,), r[:, 5 * LANES:6 * LANES], one_at_64)
    off += 6 * LANES

    fw = FOX_HEADS * HEAD_DIM
    ones3 = jnp.where((lane >= HEAD_DIM) & (lane < HEAD_DIM + 3), 1.0, 0.0)
    r = _dot(xn, w_ref[:, off:off + fw]) * scale
    for s in range(fw // LANES):
        put_pair(fq_ref, (0, s), r[:, LANES * s:LANES * (s + 1)], ones3)
    off += fw
    r = _dot(xn, w_ref[:, off:off + fw])
    for s in range(fw // LANES):
        slab = r[:, LANES * s:LANES * (s + 1)]
        for par in range(2):
            h = 2 * s + par
            neg_c = -csum[:, FGATE_LANE + h:FGATE_LANE + h + 1]
            c1, c2, c3 = _split3(neg_c)
            tail = jnp.where(lane == HEAD_DIM, c1.astype(F32), jnp.where(
                lane == HEAD_DIM + 1, c2.astype(F32), jnp.where(lane == HEAD_DIM + 2, c3.astype(F32), 0.0)))
            head = slab if par == 0 else pltpu.roll(slab, HEAD_DIM, 1)
            fk_ref[0, h] = jnp.where(low, head, tail).astype(BF16)
    off += fw
    r = _dot(xn, w_ref[:, off:off + fw])
    for s in range(fw // LANES):
        put_pair(fv_ref, (0, s), r[:, LANES * s:LANES * (s + 1)], one_at_64)
    off += fw

    step = min(512, d_model)
    for ref in (sgn_ref, sgf_ref):
        for c in range(0, d_model, step):
            r = _dot(xn, w_ref[:, off + c:off + c + step])
            ref[0, :, c:c + step] = jax.nn.sigmoid(r).astype(BF16)
        off += d_model


def _inproj(x, norm_g, w_all, rope_c, rope_s1, rope_s2, f_bias_row, tm):
    B, S, D = x.shape
    ncol = w_all.shape[1]
    G = NSA_KV_GROUPS
    wide = lambda heads: jax.ShapeDtypeStruct((B, heads, S, LANES), BF16)
    pair_shape = lambda heads: jax.ShapeDtypeStruct((B, heads // 2, 2, S, LANES), BF16)
    pair_spec = lambda heads: pl.BlockSpec((1, heads // 2, 2, tm, LANES), lambda b, i: (b, 0, 0, i, 0))
    kvspec = pl.BlockSpec((1, G, tm, LANES), lambda b, i: (b, 0, i, 0))
    tabspec = pl.BlockSpec((1, tm, LANES), lambda b, i: (b, i, 0))
    gspec = pl.BlockSpec((1, tm, D), lambda b, i: (b, i, 0))
    return pl.pallas_call(
        functools.partial(_inproj_kernel, d_model=D),
        grid=(B, S // tm),
        in_specs=[
            pl.BlockSpec((1, tm, D), lambda b, i: (b, i, 0)),
            pl.BlockSpec((1, D), lambda b, i: (0, 0)),
            pl.BlockSpec((D, ncol), lambda b, i: (0, 0)),
            tabspec, tabspec, tabspec,
            pl.BlockSpec((1, LANES), lambda b, i: (0, 0)),
        ],
        out_specs=[
            pl.BlockSpec((1, NSA_HEADS, tm, HEAD_DIM), lambda b, i: (b, 0, i, 0)),
            pair_spec(NSA_HEADS),
            pl.BlockSpec((1, 2, G, tm, HEAD_DIM), lambda b, i: (b, 0, 0, i, 0)),
            kvspec, kvspec, kvspec, kvspec,
            pair_spec(FOX_HEADS),
            pl.BlockSpec((1, FOX_HEADS, tm, LANES), lambda b, i: (b, 0, i, 0)),
            pair_spec(FOX_HEADS),
            gspec, gspec,
            pl.BlockSpec((1, 2, tm, LANES), lambda b, i: (b, 0, i, 0)),
        ],
        out_shape=[
            jax.ShapeDtypeStruct((B, NSA_HEADS, S, HEAD_DIM), BF16),
            pair_shape(NSA_HEADS),
            jax.ShapeDtypeStruct((B, 2, G, S, HEAD_DIM), BF16),
            wide(G), wide(G), wide(G), wide(G),
            pair_shape(FOX_HEADS), wide(FOX_HEADS), pair_shape(FOX_HEADS),
            jax.ShapeDtypeStruct((B, S, D), BF16), jax.ShapeDtypeStruct((B, S, D), BF16),
            jax.ShapeDtypeStruct((B, 2, S, LANES), F32),
        ],
        scratch_shapes=[pltpu.VMEM((SUBLANES, LANES), F32)],
        compiler_params=_params("parallel", "arbitrary"),
        name="inproj",
    )(x, norm_g, w_all, rope_c, rope_s1, rope_s2, f_bias_row)


def _compress_kernel(r_ref, pe_ref, w1_ref, w2_ref, o_ref):
    half = CMP_STRIDE * HEAD_DIM
    rows = r_ref[0, 0, 0].astype(F32)
    lo = (rows + pe_ref[0, :, :half]).astype(BF16)
    hi = (rows + pe_ref[0, :, half:]).astype(BF16)
    a = _dot(lo, w1_ref[0, :half])
    b = _dot(hi, w1_ref[0, half:])
    n = a.shape[0]
    pre = a + pltpu.roll(b, n - 1, 0)
    act = jax.nn.gelu(pre, approximate=True).astype(BF16)
    o_ref[0, 0, 0] = _dot(act, w2_ref[0]).astype(BF16)


def _compress(kv_rows, pe, w1, w2):
    B, _, G, n_rows, width = kv_rows.shape
    return pl.pallas_call(
        _compress_kernel,
        grid=(B, 2, G),
        in_specs=[
            pl.BlockSpec((1, 1, 1, n_rows, width), lambda b, t, g: (b, t, g, 0, 0)),
            pl.BlockSpec((1, 1, 2 * width), lambda b, t, g: (t, 0, 0)),
            pl.BlockSpec((1, 2 * width, LANES), lambda b, t, g: (t, 0, 0)),
            pl.BlockSpec((1, LANES, HEAD_DIM), lambda b, t, g: (t, 0, 0)),
        ],
        out_specs=pl.BlockSpec((1, 1, 1, n_rows, HEAD_DIM), lambda b, t, g: (b, t, g, 0, 0)),
        out_shape=jax.ShapeDtypeStruct((B, 2, G, n_rows, HEAD_DIM), BF16),
        compiler_params=_params("parallel", "parallel", "parallel"),
        name="nsa_compress",
    )(kv_rows, pe, w1, w2)


def _softmax_rows(logits, ok):
    l = jnp.where(ok, logits, NEG)
    m = jnp.max(l, axis=-1, keepdims=True)
    e = jnp.where(ok, jnp.exp(l - m), 0.0)
    s = jnp.sum(e, axis=-1, keepdims=True)
    return e / jnp.where(s > 0.0, s, 1.0)


def _nsa_kernel(qraw_ref, qrot_ref, kcmp_ref, vcmp_ref, ks_ref, vs_ref, kw_ref, vw_ref,
                misc_ref, overlap_ref, o_ref, *, seq, chunk):
    tq_n = NSA_Q_TILE
    hpg = HEADS_PER_GROUP
    rows_n = hpg * tq_n
    n_sb = seq // SEL_BLOCK
    n_sel = min(N_SEL, n_sb)
    qi = pl.program_id(2)
    t0 = qi * tq_n
    tq = t0 + lax.broadcasted_iota(I32, (tq_n, 1), 0)

    q = qraw_ref[0].reshape(rows_n, HEAD_DIM)
    n_c = kcmp_ref.shape[3]
    lg = _dot_nt(q, kcmp_ref[0, 0, 0])
    cidx = lax.broadcasted_iota(I32, (tq_n, n_c), 1)
    cmask = cidx * CMP_STRIDE + (CMP_BLOCK - 1) <= tq
    vcmp = vcmp_ref[0, 0, 0]
    o_c = []
    p_sum = jnp.zeros((tq_n, n_c), F32)
    for h in range(hpg):
        p = _softmax_rows(lg[h * tq_n:(h + 1) * tq_n], cmask)
        p_sum = p_sum + p
        o_c.append(_dot(p.astype(BF16), vcmp))

    p_hi = p_sum.astype(BF16)
    p_lo = (p_sum - p_hi.astype(F32)).astype(BF16)
    imp = (_dot_nt(overlap_ref[...], p_hi) + _dot_nt(overlap_ref[...], p_lo))[:n_sb]
    j = lax.broadcasted_iota(I32, (n_sb, tq_n), 0)
    t_row = t0 + lax.broadcasted_iota(I32, (n_sb, tq_n), 1)
    cur = t_row >> (SEL_BLOCK.bit_length() - 1)
    forced = jnp.where(j == 0, 1.0, jnp.where(j == cur, 1.0, jnp.where(j == cur - 1, 1.0, 0.0)))
    score = jnp.where(forced > 0.0, imp + FORCE_BONUS, jnp.where(j * SEL_BLOCK <= t_row, imp, NEG))
    beaten = jnp.zeros((n_sb, tq_n), F32)
    for jp in range(n_sb):
        row = score[jp:jp + 1, :]
        wins_ties = jnp.where(row >= score, 1.0, 0.0)
        wins = jnp.where(row > score, 1.0, 0.0)
        beaten = beaten + jnp.where(j > jp, wins_ties, wins)
    penalty = jnp.where(beaten < n_sel, 0.0, NEG)
    penalty = jnp.concatenate([penalty, jnp.zeros((LANES - n_sb, tq_n), F32)], axis=0).T
    penalty = pltpu.roll(penalty, HEAD_DIM, 1)

    qrot = qrot_ref[0].reshape(rows_n, LANES).astype(F32)
    lane_r = lax.broadcasted_iota(I32, (rows_n, LANES), 1)
    q_sel = jnp.where(lane_r < HEAD_DIM, qrot, jnp.concatenate([penalty] * hpg, axis=0)).astype(BF16)
    q_win = jnp.where(lane_r == HEAD_DIM, 1.0, qrot).astype(BF16)
    row_in_tile = lax.broadcasted_iota(I32, (rows_n, 1), 0) & (tq_n - 1)

    def sel_logits(c):
        return _dot_nt(q_sel, ks_ref[0, 0, pl.ds(pl.multiple_of(c * chunk, chunk), chunk), :])

    def sel_update(c, s, m, acc):
        m_new = jnp.maximum(m, jnp.broadcast_to(jnp.max(s, axis=-1, keepdims=True), m.shape))
        p = jnp.exp(s - jnp.tile(m_new, (1, chunk // LANES)))
        v = vs_ref[0, 0, pl.ds(pl.multiple_of(c * chunk, chunk), chunk), :]
        return m_new, jnp.exp(m - m_new) * acc + _dot(p.astype(BF16), v)

    def sel_body(c, st):
        s, m, acc = st
        s_next = sel_logits(c + 1)
        m, acc = sel_update(c, s, m, acc)
        return s_next, m, acc

    n_full = t0 // chunk
    init = (sel_logits(0), jnp.full((rows_n, LANES), NEG, F32), jnp.zeros((rows_n, LANES), F32))
    s, m, acc = lax.fori_loop(0, n_full, sel_body, init)
    kpos = n_full * chunk + lax.broadcasted_iota(I32, (rows_n, chunk), 1)
    s = jnp.where(kpos <= t0 + row_in_tile, s, NEG)
    _, acc_s = sel_update(n_full, s, m, acc)

    wlen = WINDOW + tq_n
    w0 = pl.multiple_of(t0, tq_n)
    kw = kw_ref[0, 0, pl.ds(w0, wlen), :]
    vw = vw_ref[0, 0, pl.ds(w0, wlen), :]
    sw = _dot_nt(q_win, kw)
    col = lax.broadcasted_iota(I32, (rows_n, tq_n), 1)
    s_old = jnp.where(col > row_in_tile, sw[:, :tq_n], NEG)
    s_new = jnp.where(col <= row_in_tile, sw[:, WINDOW:], NEG)
    sw = jnp.concatenate([s_old, sw[:, tq_n:WINDOW], s_new], axis=1)
    mw = jnp.broadcast_to(jnp.max(sw, axis=-1, keepdims=True), (rows_n, LANES))
    acc_w = _dot(jnp.exp(sw - jnp.tile(mw, (1, wlen // LANES))).astype(BF16), vw)

    g = misc_ref[0, 0]
    for h in range(hpg):
        a_s = acc_s[h * tq_n:(h + 1) * tq_n]
        a_w = acc_w[h * tq_n:(h + 1) * tq_n]
        o_s = a_s[:, :HEAD_DIM] / a_s[:, HEAD_DIM:HEAD_DIM + 1]
        o_w = a_w[:, :HEAD_DIM] / a_w[:, HEAD_DIM:HEAD_DIM + 1]
        o_h = g[:, 3 * h:3 * h + 1] * o_c[h] + g[:, 3 * h + 1:3 * h + 2] * o_s + g[:, 3 * h + 2:3 * h + 3] * o_w
        o_ref[0, :, HEAD_DIM * h:HEAD_DIM * (h + 1)] = o_h.astype(BF16)


def _nsa(qraw, qrot, cmp_kv, ks, vs, kw_pad, vw_pad, misc, overlap):
    B, _, S, _ = qraw.shape
    G = NSA_KV_GROUPS
    n_c = cmp_kv.shape[3]
    chunk = min(KV_CHUNK, S)
    kvspec = pl.BlockSpec((1, 1, S, LANES), lambda b, g, i: (b, g, 0, 0))
    padspec = pl.BlockSpec((1, 1, S + WINDOW, LANES), lambda b, g, i: (b, g, 0, 0))
    return pl.pallas_call(
        functools.partial(_nsa_kernel, seq=S, chunk=chunk),
        grid=(B, G, S // NSA_Q_TILE),
        in_specs=[
            pl.BlockSpec((1, HEADS_PER_GROUP, NSA_Q_TILE, HEAD_DIM), lambda b, g, i: (b, g, i, 0)),
            pl.BlockSpec((1, HEADS_PER_GROUP, NSA_Q_TILE, LANES), lambda b, g, i: (b, g, i, 0)),
            pl.BlockSpec((1, 1, 1, n_c, HEAD_DIM), lambda b, g, i: (b, 0, g, 0, 0)),
            pl.BlockSpec((1, 1, 1, n_c, HEAD_DIM), lambda b, g, i: (b, 1, g, 0, 0)),
            kvspec, kvspec, padspec, padspec,
            pl.BlockSpec((1, 1, NSA_Q_TILE, LANES), lambda b, g, i: (b, g, i, 0)),
            pl.BlockSpec((LANES, n_c), lambda b, g, i: (0, 0)),
        ],
        out_specs=pl.BlockSpec((1, NSA_Q_TILE, HEADS_PER_GROUP * HEAD_DIM), lambda b, g, i: (b, i, g)),
        out_shape=jax.ShapeDtypeStruct((B, S, NSA_HEADS * HEAD_DIM), BF16),
        compiler_params=_params("parallel", "parallel", "arbitrary"),
        name="nsa_attention",
    )(qraw, qrot, cmp_kv, cmp_kv, ks, vs, kw_pad, vw_pad, misc, overlap)


def _fox_kernel(q_ref, k_ref, v_ref, o_ref, m_scr, acc_scr):
    tile = q_ref.shape[2]
    wide = 2 * tile
    qi = pl.program_id(1)
    t0 = pl.multiple_of(qi * tile, tile)
    causal = (lax.broadcasted_iota(I32, (tile, tile), 1) <= lax.broadcasted_iota(I32, (tile, tile), 0))
    m_scr[...] = jnp.full_like(m_scr, NEG)
    acc_scr[...] = jnp.zeros_like(acc_scr)

    def step(k0, width, diagonal):
        for h in range(FOX_HEADS):
            s = _dot_nt(q_ref[0, h], k_ref[0, h, pl.ds(k0, width), :])
            if diagonal:
                s = jnp.where(causal, s, NEG)
            m = m_scr[h]
            m_new = jnp.maximum(m, jnp.broadcast_to(jnp.max(s, axis=-1, keepdims=True), m.shape))
            p = jnp.exp(s - jnp.tile(m_new, (1, width // LANES)))
            acc_scr[h] = jnp.exp(m - m_new) * acc_scr[h] + _dot(p.astype(BF16), v_ref[0, h, pl.ds(k0, width), :])
            m_scr[h] = m_new

    n_wide = t0 // wide

    def body(c, _):
        step(pl.multiple_of(c * wide, wide), wide, False)
        return 0

    lax.fori_loop(0, n_wide, body, 0)

    @pl.when(t0 - n_wide * wide >= tile)
    def _():
        step(pl.multiple_of(n_wide * wide, tile), tile, False)

    step(t0, tile, True)
    for h in range(FOX_HEADS):
        acc = acc_scr[h]
        o_ref[0, :, HEAD_DIM * h:HEAD_DIM * (h + 1)] = (acc[:, :HEAD_DIM] / acc[:, HEAD_DIM:HEAD_DIM + 1]).astype(BF16)


def _fox(fq, fk, fv):
    B, H, S, _ = fq.shape
    tile = min(FOX_TILE, S)
    assert S % (2 * tile) == 0 or S == tile
    return pl.pallas_call(
        _fox_kernel,
        grid=(B, S // tile),
        in_specs=[
            pl.BlockSpec((1, H, tile, LANES), lambda b, i: (b, 0, i, 0)),
            pl.BlockSpec((1, H, S, LANES), lambda b, i: (b, 0, 0, 0)),
            pl.BlockSpec((1, H, S, LANES), lambda b, i: (b, 0, 0, 0)),
        ],
        out_specs=pl.BlockSpec((1, tile, H * HEAD_DIM), lambda b, i: (b, i, 0)),
        out_shape=jax.ShapeDtypeStruct((B, S, H * HEAD_DIM), BF16),
        scratch_shapes=[pltpu.VMEM((H, tile, LANES), F32), pltpu.VMEM((H, tile, LANES), F32)],
        compiler_params=_params("parallel", "arbitrary"),
        name="fox_attention",
    )(fq, fk, fv)


def _merge_kernel(x_ref, on_ref, of_ref, sgn_ref, sgf_ref, wn_ref, wf_ref, wo_ref, g2_ref,
                  wrh_ref, wrl_ref, br_ref, x1_ref, h2_ref, ri_ref, rw_ref, cnt_ref, carry_scr):
    i = pl.program_id(0)
    tm = x_ref.shape[0]

    @pl.when(i == 0)
    def _():
        carry_scr[...] = jnp.zeros_like(carry_scr)

    a = _dot(on_ref[...], wn_ref[...])
    b = _dot(of_ref[...], wf_ref[...])
    mixed = (sgn_ref[...].astype(F32) * a + sgf_ref[...].astype(F32) * b).astype(BF16)
    x1 = x_ref[...] + _dot(mixed, wo_ref[...])
    x1_ref[...] = x1
    h2 = _rms(x1, g2_ref[...])
    h2_ref[...] = h2

    hh = h2.astype(BF16)
    hl = (h2 - hh.astype(F32)).astype(BF16)
    logits = _dot(hh, wrh_ref[...]) + _dot(hl, wrh_ref[...]) + _dot(hh, wrl_ref[...]) + br_ref[...]
    lane = lax.broadcasted_iota(I32, (tm, LANES), 1)
    lane_f = lane.astype(F32)
    l = jnp.where(lane < N_EXPERTS, logits, BELOW_NEG)
    idxs, vals, hots = [], [], []
    for _ in range(TOP_K):
        m = jnp.max(l, axis=-1, keepdims=True)
        idx = jnp.min(jnp.where(l == m, lane_f, float(LANES)), axis=-1, keepdims=True)
        hot = lane_f == idx
        idxs.append(idx)
        vals.append(m)
        hots.append(hot)
        l = jnp.where(hot, BELOW_NEG, l)
    exps = [jnp.exp(v - vals[0]) for v in vals]
    den = exps[0]
    for e in exps[1:]:
        den = den + e

    chosen = jnp.zeros((tm, LANES), F32)
    for hot in hots:
        chosen = chosen + jnp.where(hot, 1.0, 0.0)
    r_i = lax.broadcasted_iota(I32, (tm, tm), 0)
    c_i = lax.broadcasted_iota(I32, (tm, tm), 1)
    earlier = jnp.where(c_i < r_i, 1.0, 0.0).astype(BF16)
    before = _dot(earlier, chosen.astype(BF16)) + carry_scr[0:1, :]
    carry_scr[...] = carry_scr[...] + jnp.sum(chosen, axis=0, keepdims=True)
    cnt_ref[...] = carry_scr[...]

    ri = jnp.zeros((tm, LANES), F32)
    rw = jnp.zeros((tm, LANES), F32)
    for k in range(TOP_K):
        rank = jnp.sum(jnp.where(hots[k], before, 0.0), axis=-1, keepdims=True)
        ri = jnp.where(lane == k, idxs[k], ri)
        ri = jnp.where(lane == TOP_K + k, rank, ri)
        rw = jnp.where(lane == k, exps[k] / den, rw)
    ri_ref[...] = ri.astype(I32)
    rw_ref[...] = rw


def _merge(x2d, o_nsa, o_fox, sgn, sgf, wn, wf, wo, g2, wr_hi, wr_lo, br, tm):
    T, D = x2d.shape
    wq = o_nsa.shape[1]
    row = lambda w: pl.BlockSpec((tm, w), lambda i: (i, 0))
    full = lambda a: pl.BlockSpec(a.shape, lambda i: (0,) * a.ndim)
    return pl.pallas_call(
        _merge_kernel,
        grid=(T // tm,),
        in_specs=[row(D), row(wq), row(wq), row(D), row(D), full(wn), full(wf), full(wo), full(g2),
                  full(wr_hi), full(wr_lo), full(br)],
        out_specs=[row(D), row(D), row(LANES), row(LANES), pl.BlockSpec((SUBLANES, LANES), lambda i: (0, 0))],
        out_shape=[
            jax.ShapeDtypeStruct((T, D), F32), jax.ShapeDtypeStruct((T, D), F32),
            jax.ShapeDtypeStruct((T, LANES), I32), jax.ShapeDtypeStruct((T, LANES), F32),
            jax.ShapeDtypeStruct((SUBLANES, LANES), F32),
        ],
        scratch_shapes=[pltpu.VMEM((SUBLANES, LANES), F32)],
        compiler_params=_params("arbitrary"),
        name="merge_router",
    )(x2d, o_nsa, o_fox, sgn, sgf, wn, wf, wo, g2, wr_hi, wr_lo, br)


def _dest_kernel(ri_ref, cnt_ref, dest_ref, tab_ref):
    tm = ri_ref.shape[0]
    shift = MOE_ROWS.bit_length() - 1
    cnt = cnt_ref[...].astype(I32)
    padded = ((cnt + (MOE_ROWS - 1)) >> shift) << shift
    lane8 = lax.broadcasted_iota(I32, (SUBLANES, LANES), 1)
    pend = padded
    sh = 1
    while sh < LANES:
        pend = pend + jnp.where(lane8 >= sh, pltpu.roll(pend, sh, 1), 0)
        sh *= 2
    pstart = pend - padded
    row8 = lax.broadcasted_iota(I32, (SUBLANES, LANES), 0)
    tab_ref[...] = jnp.where(row8 == 0, cnt, jnp.where(row8 == 1, padded, jnp.where(row8 == 2, pstart, pend)))

    ri = ri_ref[...]
    lane = lax.broadcasted_iota(I32, (tm, LANES), 1)
    start_row = pstart[0:1, :].astype(F32)
    dest = jnp.zeros((tm, LANES), F32)
    for k in range(TOP_K):
        hot = lane == ri[:, k:k + 1]
        base = jnp.sum(jnp.where(hot, start_row, 0.0), axis=-1, keepdims=True)
        dest = jnp.where(lane == k, base + ri[:, TOP_K + k:TOP_K + k + 1].astype(F32), dest)
    dest_ref[...] = dest.astype(I32)


def _dest(ri, cnt, tm):
    T = ri.shape[0]
    return pl.pallas_call(
        _dest_kernel,
        grid=(T // tm,),
        in_specs=[pl.BlockSpec((tm, LANES), lambda i: (i, 0)), pl.BlockSpec((SUBLANES, LANES), lambda i: (0, 0))],
        out_specs=[pl.BlockSpec((tm, LANES), lambda i: (i, 0)), pl.BlockSpec((SUBLANES, LANES), lambda i: (0, 0))],
        out_shape=[jax.ShapeDtypeStruct((T, LANES), I32), jax.ShapeDtypeStruct((SUBLANES, LANES), I32)],
        compiler_params=_params("arbitrary"),
        name="moe_dest",
    )(ri, cnt)


def _pad_copies(tab_ref, zbuf, xin_hbm, zsem):
    out = []
    for e in range(N_EXPERTS):
        cnt = tab_ref[e]
        pad = tab_ref[LANES + e] - cnt
        base = tab_ref[2 * LANES + e] + cnt
        piece = 1
        while piece < MOE_ROWS:
            cond = (pad & piece) != 0
            if piece < SUBLANES:
                for r in range(piece):
                    out.append((cond, pltpu.make_async_copy(
                        zbuf.at[pl.ds(0, 1)], xin_hbm.at[pl.ds(base + r, 1)], zsem)))
            else:
                out.append((cond, pltpu.make_async_copy(
                    zbuf.at[pl.ds(0, piece)], xin_hbm.at[pl.ds(pl.multiple_of(base, SUBLANES), piece)], zsem)))
            base = base + jnp.where(cond, piece, 0)
            piece *= 2
    last = N_EXPERTS - 1
    n_live = (tab_ref[2 * LANES + last] + tab_ref[LANES + last]) // MOE_ROWS
    n_blocks = xin_hbm.shape[0] // MOE_ROWS
    for blk in range(n_blocks - N_EXPERTS, n_blocks):
        out.append((blk >= n_live, pltpu.make_async_copy(
            zbuf, xin_hbm.at[pl.ds(blk * MOE_ROWS, MOE_ROWS)], zsem)))
    return out


def _dispatch_kernel(tab_ref, dest_hbm, h2_ref, xin_hbm, idx_smem, zbuf, sem, zsem, isem):
    i = pl.program_id(0)
    tm = h2_ref.shape[0]
    idx_cp = pltpu.make_async_copy(dest_hbm.at[i], idx_smem, isem)
    idx_cp.start()

    @pl.when(i == 0)
    def _():
        zbuf[...] = jnp.zeros_like(zbuf)
        for cond, cp in _pad_copies(tab_ref, zbuf, xin_hbm, zsem):
            @pl.when(cond)
            def _():
                cp.start()

    idx_cp.wait()

    def issue(t, _):
        for k in range(TOP_K):
            d = idx_smem[TOP_K * t + k]
            pltpu.make_async_copy(h2_ref.at[pl.ds(t, 1)], xin_hbm.at[pl.ds(d, 1)], sem).start(priority=k % 2)
        return 0

    lax.fori_loop(0, tm, issue, 0)
    for _ in range(TOP_K):
        pltpu.make_async_copy(h2_ref, xin_hbm.at[pl.ds(0, tm)], sem).wait()

    @pl.when(i == 0)
    def _():
        for cond, cp in _pad_copies(tab_ref, zbuf, xin_hbm, zsem):
            @pl.when(cond)
            def _():
                cp.wait()


def _dispatch(tab_flat, dest_tiles, h2, n_rows, tm):
    T, D = h2.shape
    return pl.pallas_call(
        _dispatch_kernel,
        grid_spec=pltpu.PrefetchScalarGridSpec(
            num_scalar_prefetch=1,
            grid=(T // tm,),
            in_specs=[pl.BlockSpec(memory_space=pl.ANY), pl.BlockSpec((tm, D), lambda i, tab: (i, 0))],
            out_specs=pl.BlockSpec(memory_space=pl.ANY),
            scratch_shapes=[
                pltpu.SMEM((TOP_K * tm,), I32),
                pltpu.VMEM((MOE_ROWS, D), F32),
                pltpu.SemaphoreType.DMA, pltpu.SemaphoreType.DMA, pltpu.SemaphoreType.DMA,
            ],
        ),
        out_shape=jax.ShapeDtypeStruct((n_rows, D), F32),
        compiler_params=_params("arbitrary"),
        name="moe_dispatch",
    )(tab_flat, dest_tiles, h2)


def _expert_kernel(be_ref, bs_ref, nv_ref, x_ref, w1_ref, b1_ref, w2_ref, b2_ref, y_ref, w1b, w2b, *, d_ff):
    s = pl.program_id(0)
    live = s < nv_ref[0]
    fresh = (s == 0) | (be_ref[s] != be_ref[jnp.maximum(s - 1, 0)])

    @pl.when(live & fresh)
    def _():
        w1b[...] = w1_ref[0].astype(BF16)
        w2b[...] = w2_ref[0].astype(BF16)

    @pl.when(live)
    def _():
        hc = _dot(x_ref[...].astype(BF16), w1b[...]) + b1_ref[0]
        gt = jnp.minimum(hc[:, :d_ff], SWIGLU_LIMIT)
        up = jnp.clip(hc[:, d_ff:], -SWIGLU_LIMIT, SWIGLU_LIMIT)
        glu = gt * jax.nn.sigmoid(SWIGLU_ALPHA * gt)
        act = ((up + 1.0) * glu).astype(BF16)
        y_ref[...] = _dot(act, w2b[...]) + b2_ref[0]

    @pl.when(jnp.logical_not(live))
    def _():
        y_ref[...] = jnp.zeros_like(y_ref)


def _experts(blk_e, blk_src, n_live, xin, w1, b1, w2, b2):
    n_rows, D = xin.shape
    E, _, two_f = w1.shape
    d_ff = two_f // 2
    return pl.pallas_call(
        functools.partial(_expert_kernel, d_ff=d_ff),
        grid_spec=pltpu.PrefetchScalarGridSpec(
            num_scalar_prefetch=3,
            grid=(n_rows // MOE_ROWS,),
            in_specs=[
                pl.BlockSpec((MOE_ROWS, D), lambda s, be, bs, nv: (bs[s], 0)),
                pl.BlockSpec((1, D, two_f), lambda s, be, bs, nv: (be[s], 0, 0)),
                pl.BlockSpec((1, 1, two_f), lambda s, be, bs, nv: (be[s], 0, 0)),
                pl.BlockSpec((1, d_ff, D), lambda s, be, bs, nv: (be[s], 0, 0)),
                pl.BlockSpec((1, 1, D), lambda s, be, bs, nv: (be[s], 0, 0)),
            ],
            out_specs=pl.BlockSpec((MOE_ROWS, D), lambda s, be, bs, nv: (s, 0)),
            scratch_shapes=[pltpu.VMEM((D, two_f), BF16), pltpu.VMEM((d_ff, D), BF16)],
        ),
        out_shape=jax.ShapeDtypeStruct((n_rows, D), F32),
        compiler_params=_params("arbitrary"),
        name="moe_experts",
    )(blk_e, blk_src, n_live, xin, w1, b1, w2, b2)


def _combine_kernel(dest_hbm, x1_ref, rw_ref, gf_ref, y_hbm, o_ref, idx_smem, ybuf, sem, isem):
    i = pl.program_id(0)
    tm = x1_ref.shape[0]
    idx_cp = pltpu.make_async_copy(dest_hbm.at[i], idx_smem, isem)
    idx_cp.start()
    idx_cp.wait()

    def issue(t, _):
        for k in range(TOP_K):
            d = idx_smem[TOP_K * t + k]
            pltpu.make_async_copy(y_hbm.at[pl.ds(d, 1)], ybuf.at[k, pl.ds(t, 1)], sem).start(priority=k % 2)
        return 0

    lax.fori_loop(0, tm, issue, 0)
    for k in range(TOP_K):
        pltpu.make_async_copy(y_hbm.at[pl.ds(0, tm)], ybuf.at[k], sem).wait()

    rw = rw_ref[...]
    x2 = x1_ref[...]
    for k in range(TOP_K):
        x2 = x2 + rw[:, k:k + 1] * ybuf[k]
    o_ref[...] = _rms(x2, gf_ref[...])


def _combine(dest_tiles, x1, rw, gf, y, tm):
    T, D = x1.shape
    return pl.pallas_call(
        _combine_kernel,
        grid=(T // tm,),
        in_specs=[
            pl.BlockSpec(memory_space=pl.ANY),
            pl.BlockSpec((tm, D), lambda i: (i, 0)),
            pl.BlockSpec((tm, LANES), lambda i: (i, 0)),
            pl.BlockSpec((1, D), lambda i: (0, 0)),
            pl.BlockSpec(memory_space=pl.ANY),
        ],
        out_specs=pl.BlockSpec((tm, D), lambda i: (i, 0)),
        out_shape=jax.ShapeDtypeStruct((T, D), F32),
        scratch_shapes=[
            pltpu.SMEM((TOP_K * tm,), I32),
            pltpu.VMEM((TOP_K, tm, D), F32),
            pltpu.SemaphoreType.DMA, pltpu.SemaphoreType.DMA,
        ],
        compiler_params=_params("arbitrary"),
        name="moe_combine",
    )(dest_tiles, x1, rw, gf, y)


def _rope_tables(positions):
    half = ROT_DIM // 2
    freqs = ROPE_THETA ** (-jnp.arange(0, ROT_DIM, 2, dtype=F32) / ROT_DIM)
    ang = positions.astype(F32)[..., None] * freqs
    cos, sin = jnp.cos(ang), jnp.sin(ang)
    d = np.arange(LANES) % HEAD_DIM
    f_idx = d % half
    in_lo = jnp.asarray(d < half)
    in_hi = jnp.asarray((d >= half) & (d < ROT_DIM))
    cos_l, sin_l = cos[..., f_idx], sin[..., f_idx]
    c = jnp.where(in_lo | in_hi, cos_l, 1.0)
    s1 = jnp.where(in_lo, -sin_l, 0.0)
    s2 = jnp.where(in_hi, sin_l, 0.0)
    return c, s1, s2


def _arrange_w_in(w_in, d_model):
    nq, nkv, ng = NSA_HEADS * HEAD_DIM, NSA_KV_GROUPS * HEAD_DIM, NSA_HEADS * 3
    fw = FOX_HEADS * HEAD_DIM
    sizes = [nq] + [nkv] * 6 + [ng, fw, fw, fw, FOX_HEADS, d_model, d_model]
    offs = np.concatenate([[0], np.cumsum(sizes)])
    piece = lambda k: w_in[:, offs[k]:offs[k + 1]]
    gates, ff = piece(7), piece(11)
    per_group = HEADS_PER_GROUP * 3
    zeros = lambda n: jnp.zeros((w_in.shape[0], n), w_in.dtype)
    misc0 = jnp.concatenate([gates[:, :per_group], zeros(FGATE_LANE - per_group), ff,
                             zeros(LANES - FGATE_LANE - FOX_HEADS)], axis=1)
    misc1 = jnp.concatenate([gates[:, per_group:], zeros(LANES - per_group)], axis=1)
    cols = [piece(k) for k in range(7)] + [piece(8), piece(9), piece(10), piece(12), piece(13), misc0, misc1]
    return jnp.concatenate(cols, axis=1).astype(BF16)


def _split_bf16(a):
    hi = a.astype(BF16)
    return hi, (a - hi.astype(F32)).astype(BF16)


def kernel(x, positions, norm1_g, w_in, cmp_pos_emb, cmp_w1, cmp_w2, fox_f_bias, w_proj_nsa, w_proj_fox, w_out,
           norm2_g, router_w, router_b, expert_w1, expert_b1, expert_w2, expert_b2, norm_f_g):
    B, S, D = x.shape
    T = B * S
    assert norm1_g.shape[0] == 1, "the combine kernel fuses the output norm, so it must follow the only layer"
    assert S % KV_CHUNK == 0 or S < KV_CHUNK
    assert S // SEL_BLOCK <= LANES - HEAD_DIM and S % (CMP_STRIDE * SUBLANES) == 0
    tm_proj = min(512, S)
    assert WINDOW % tm_proj == 0 or tm_proj % WINDOW == 0
    tm_tok = min(512, T)
    tm_comb = min(256, T)
    l = 0

    rope_c, rope_s1, rope_s2 = _rope_tables(positions)
    n_sb = S // SEL_BLOCK
    n_rows16 = S // CMP_STRIDE
    ci = np.arange(n_rows16)[None, :] * CMP_STRIDE
    sj = np.arange(LANES)[:, None] * SEL_BLOCK
    overlap = jnp.asarray(((ci < sj + SEL_BLOCK) & (ci + CMP_BLOCK > sj) & (np.arange(LANES)[:, None] < n_sb)
                           & (np.arange(n_rows16)[None, :] < n_rows16 - 1)).astype(np.float32), BF16)

    w_all = _arrange_w_in(w_in[l], D)
    fb_row = jnp.zeros((1, LANES), F32).at[0, FGATE_LANE:FGATE_LANE + FOX_HEADS].set(fox_f_bias[l].astype(F32))
    (qraw, qrot, kvc, ks, vs, kw, vw, fq, fk, fv, sgn, sgf, misc) = _inproj(
        x, norm1_g[l][None, :], w_all, rope_c, rope_s1, rope_s2, fb_row, tm_proj)
    qrot = qrot.reshape(B, NSA_HEADS, S, LANES)
    fq = fq.reshape(B, FOX_HEADS, S, LANES)
    fv = fv.reshape(B, FOX_HEADS, S, LANES)

    kv_rows = kvc.reshape(B, 2, NSA_KV_GROUPS, n_rows16, CMP_STRIDE * HEAD_DIM)
    pe = cmp_pos_emb[l].reshape(2, 1, CMP_BLOCK * HEAD_DIM).astype(F32)
    cw1 = jnp.pad(cmp_w1[l], ((0, 0), (0, 0), (0, LANES - HEAD_DIM))).astype(BF16)
    cw2 = jnp.pad(cmp_w2[l], ((0, 0), (0, LANES - HEAD_DIM), (0, 0))).astype(BF16)
    cmp_kv = _compress(kv_rows, pe, cw1, cw2)

    front_k = jnp.zeros((B, NSA_KV_GROUPS, WINDOW, LANES), BF16).at[..., HEAD_DIM].set(NEG)
    kw_pad = jnp.concatenate([front_k, kw], axis=2)
    vw_pad = jnp.concatenate([jnp.zeros((B, NSA_KV_GROUPS, WINDOW, LANES), BF16), vw], axis=2)
    o_nsa = _nsa(qraw, qrot, cmp_kv, ks, vs, kw_pad, vw_pad, misc, overlap)
    o_fox = _fox(fq, fk, fv)

    wr = jnp.pad(router_w[l], ((0, 0), (0, LANES - N_EXPERTS)))
    wr_hi, wr_lo = _split_bf16(wr)
    br = jnp.pad(router_b[l], (0, LANES - N_EXPERTS))[None, :].astype(F32)
    x1, h2, ri, rw, cnt = _merge(
        x.reshape(T, D), o_nsa.reshape(T, -1), o_fox.reshape(T, -1), sgn.reshape(T, D), sgf.reshape(T, D),
        w_proj_nsa[l].astype(BF16), w_proj_fox[l].astype(BF16), w_out[l].astype(BF16),
        norm2_g[l][None, :], wr_hi, wr_lo, br, tm_tok)

    dest, tab = _dest(ri, cnt, tm_tok)
    dest4 = dest[:, :TOP_K]
    pend = tab[3, :N_EXPERTS]
    n_blocks = (T * TOP_K) // MOE_ROWS + N_EXPERTS
    n_live = (pend[N_EXPERTS - 1] // MOE_ROWS).astype(I32)
    blk_src = jnp.minimum(jnp.arange(n_blocks, dtype=I32), n_live - 1)
    blk_e = jnp.sum((pend[None, :] <= (blk_src * MOE_ROWS)[:, None]).astype(I32), axis=1)
    blk_e = jnp.minimum(blk_e, N_EXPERTS - 1)

    xin = _dispatch(tab[:3].reshape(-1), dest4.reshape(T // tm_tok, TOP_K * tm_tok), h2,
                    n_blocks * MOE_ROWS, tm_tok)
    y = _experts(blk_e, blk_src, n_live[None], xin, expert_w1[l], expert_b1[l][:, None, :],
                 expert_w2[l], expert_b2[l][:, None, :])
    out = _combine(dest4.reshape(T // tm_comb, TOP_K * tm_comb), x1, rw, norm_f_g[None, :], y, tm_comb)
    return out.reshape(B, S, D)
```

```python
import functools

import numpy as np
import jax
import jax.numpy as jnp
from jax import lax
from jax.experimental import pallas as pl
from jax.experimental.pallas import tpu as pltpu

HEAD_DIM = 64
NSA_HEADS = 8
NSA_KV_GROUPS = 2
HEADS_PER_GROUP = NSA_HEADS // NSA_KV_GROUPS
FOX_HEADS = 8
ROT_DIM = HEAD_DIM // 4
ROPE_THETA = 500000.0
CMP_BLOCK = 32
CMP_STRIDE = 16
SEL_BLOCK = 64
N_SEL = 8
WINDOW = 512
N_EXPERTS = 32
TOP_K = 4
SWIGLU_ALPHA = 1.702
SWIGLU_LIMIT = 7.0
NORM_EPS = 1e-6
NEG = -1e30
BELOW_NEG = -3e38
FORCE_BONUS = 1e4

LANES = 128
SUBLANES = 8
NSA_Q_TILE = 128
KV_CHUNK = 512
FOX_TILE = 512
MOE_ROWS = 256
FGATE_LANE = 32
VMEM_LIMIT = 56 * 1024 * 1024

F32 = jnp.float32
BF16 = jnp.bfloat16
I32 = jnp.int32


def _dot(a, b):
    return jnp.dot(a, b, preferred_element_type=F32)


def _dot_nt(a, b):
    return lax.dot_general(a, b, (((1,), (1,)), ((), ())), preferred_element_type=F32)


def _params(*sem):
    return pltpu.CompilerParams(dimension_semantics=sem, vmem_limit_bytes=VMEM_LIMIT)


def _rms(x, g):
    return x * lax.rsqrt(jnp.mean(x * x, axis=-1, keepdims=True) + NORM_EPS) * g


def _split3(a):
    t1 = a.astype(BF16)
    r1 = a - t1.astype(F32)
    t2 = r1.astype(BF16)
    t3 = (r1 - t2.astype(F32)).astype(BF16)
    return t1, t2, t3


def _inproj_kernel(x_ref, g_ref, w_ref, rc_ref, rs1_ref, rs2_ref, fb_ref,
                   qraw_ref, qrot_ref, kvc_ref, ks_ref, vs_ref, kw_ref, vw_ref,
                   fq_ref, fk_ref, fv_ref, sgn_ref, sgf_ref, misc_ref, csum_scr, *, d_model):
    scale = HEAD_DIM ** -0.5
    i = pl.program_id(1)
    tm = x_ref.shape[1]
    xn = _rms(x_ref[0], g_ref[...]).astype(BF16)
    rc, rs1, rs2 = rc_ref[0], rs1_ref[0], rs2_ref[0]
    lane = lax.broadcasted_iota(I32, (tm, LANES), 1)
    low = lane < HEAD_DIM
    one_at_64 = jnp.where(lane == HEAD_DIM, 1.0, 0.0)

    def rope(slab):
        half = ROT_DIM // 2
        return slab * rc + pltpu.roll(slab, LANES - half, 1) * rs1 + pltpu.roll(slab, half, 1) * rs2

    def put_pair(ref, lead, slab, tail):
        ref[lead + (0,)] = jnp.where(low, slab, tail).astype(BF16)
        ref[lead + (1,)] = jnp.where(low, pltpu.roll(slab, HEAD_DIM, 1), tail).astype(BF16)

    misc_off = w_ref.shape[1] - 2 * LANES
    r = _dot(xn, w_ref[:, misc_off:])
    m0 = r[:, :LANES]
    z = m0 + fb_ref[...]
    log_f = jnp.minimum(z, 0.0) - jnp.log(1.0 + jnp.exp(-jnp.abs(z)))
    misc_ref[0, 0] = jax.nn.sigmoid(m0)
    misc_ref[0, 1] = jax.nn.sigmoid(r[:, LANES:])

    @pl.when(i == 0)
    def _():
        csum_scr[...] = jnp.zeros_like(csum_scr)

    in_gate = (lane >= FGATE_LANE) & (lane < FGATE_LANE + FOX_HEADS)
    t1, t2, t3 = _split3(jnp.where(in_gate, log_f, 0.0))
    r_i = lax.broadcasted_iota(I32, (tm, tm), 0)
    c_i = lax.broadcasted_iota(I32, (tm, tm), 1)
    upto = jnp.where(c_i <= r_i, 1.0, 0.0).astype(BF16)
    csum = _dot(upto, t1) + _dot(upto, t2) + _dot(upto, t3) + csum_scr[0:1, :]
    csum_scr[...] = jnp.broadcast_to(csum[tm - 1:tm, :], csum_scr.shape)

    nq = NSA_HEADS * HEAD_DIM
    r = _dot(xn, w_ref[:, 0:nq]) * scale
    for s in range(nq // LANES):
        slab = r[:, LANES * s:LANES * (s + 1)]
        qraw_ref[0, 2 * s] = slab[:, :HEAD_DIM].astype(BF16)
        qraw_ref[0, 2 * s + 1] = slab[:, HEAD_DIM:].astype(BF16)
        put_pair(qrot_ref, (0, s), rope(slab), 0.0)
    off = nq

    r = _dot(xn, w_ref[:, off:off + 6 * LANES])
    for t in range(2):
        slab = r[:, t * LANES:(t + 1) * LANES]
        kvc_ref[0, t, 0] = slab[:, :HEAD_DIM].astype(BF16)
        kvc_ref[0, t, 1] = slab[:, HEAD_DIM:].astype(BF16)
    tok = i * tm + lax.broadcasted_iota(I32, (tm, LANES), 0)
    block_hot = jnp.where(lane - HEAD_DIM == (tok >> (SEL_BLOCK.bit_length() - 1)), 1.0, 0.0)
    put_pair(ks_ref, (0,), rope(r[:, 2 * LANES:3 * LANES]), block_hot)
    put_pair(vs_ref, (0,), r[:, 3 * LANES:4 * LANES], one_at_64)
    put_pair(kw_ref, (0,), rope(r[:, 4 * LANES:5 * LANES]), 0.0)
    put_pair(vw_ref, (0,), r[:, 5 * LANES:6 * LANES], one_at_64)
    off += 6 * LANES

    fw = FOX_HEADS * HEAD_DIM
    ones3 = jnp.where((lane >= HEAD_DIM) & (lane < HEAD_DIM + 3), 1.0, 0.0)
    r = _dot(xn, w_ref[:, off:off + fw]) * scale
    for s in range(fw // LANES):
        put_pair(fq_ref, (0, s), r[:, LANES * s:LANES * (s + 1)], ones3)
    off += fw
    r = _dot(xn, w_ref[:, off:off + fw])
    for s in range(fw // LANES):
        slab = r[:, LANES * s:LANES * (s + 1)]
        for par in range(2):
            h = 2 * s + par
            neg_c = -csum[:, FGATE_LANE + h:FGATE_LANE + h + 1]
            c1, c2, c3 = _split3(neg_c)
            tail = jnp.where(lane == HEAD_DIM, c1.astype(F32), jnp.where(
                lane == HEAD_DIM + 1, c2.astype(F32), jnp.where(lane == HEAD_DIM + 2, c3.astype(F32), 0.0)))
            head = slab if par == 0 else pltpu.roll(slab, HEAD_DIM, 1)
            fk_ref[0, h] = jnp.where(low, head, tail).astype(BF16)
    off += fw
    r = _dot(xn, w_ref[:, off:off + fw])
    for s in range(fw // LANES):
        put_pair(fv_ref, (0, s), r[:, LANES * s:LANES * (s + 1)], one_at_64)
    off += fw

    step = min(512, d_model)
    for ref in (sgn_ref, sgf_ref):
        for c in range(0, d_model, step):
            r = _dot(xn, w_ref[:, off + c:off + c + step])
            ref[0, :, c:c + step] = jax.nn.sigmoid(r).astype(BF16)
        off += d_model


def _inproj(x, norm_g, w_all, rope_c, rope_s1, rope_s2, f_bias_row, tm):
    B, S, D = x.shape
    ncol = w_all.shape[1]
    G = NSA_KV_GROUPS
    wide = lambda heads: jax.ShapeDtypeStruct((B, heads, S, LANES), BF16)
    pair_shape = lambda heads: jax.ShapeDtypeStruct((B, heads // 2, 2, S, LANES), BF16)
    pair_spec = lambda heads: pl.BlockSpec((1, heads // 2, 2, tm, LANES), lambda b, i: (b, 0, 0, i, 0))
    kvspec = pl.BlockSpec((1, G, tm, LANES), lambda b, i: (b, 0, i, 0))
    tabspec = pl.BlockSpec((1, tm, LANES), lambda b, i: (b, i, 0))
    gspec = pl.BlockSpec((1, tm, D), lambda b, i: (b, i, 0))
    return pl.pallas_call(
        functools.partial(_inproj_kernel, d_model=D),
        grid=(B, S // tm),
        in_specs=[
            pl.BlockSpec((1, tm, D), lambda b, i: (b, i, 0)),
            pl.BlockSpec((1, D), lambda b, i: (0, 0)),
            pl.BlockSpec((D, ncol), lambda b, i: (0, 0)),
            tabspec, tabspec, tabspec,
            pl.BlockSpec((1, LANES), lambda b, i: (0, 0)),
        ],
        out_specs=[
            pl.BlockSpec((1, NSA_HEADS, tm, HEAD_DIM), lambda b, i: (b, 0, i, 0)),
            pair_spec(NSA_HEADS),
            pl.BlockSpec((1, 2, G, tm, HEAD_DIM), lambda b, i: (b, 0, 0, i, 0)),
            kvspec, kvspec, kvspec, kvspec,
            pair_spec(FOX_HEADS),
            pl.BlockSpec((1, FOX_HEADS, tm, LANES), lambda b, i: (b, 0, i, 0)),
            pair_spec(FOX_HEADS),
            gspec, gspec,
            pl.BlockSpec((1, 2, tm, LANES), lambda b, i: (b, 0, i, 0)),
        ],
        out_shape=[
            jax.ShapeDtypeStruct((B, NSA_HEADS, S, HEAD_DIM), BF16),
            pair_shape(NSA_HEADS),
            jax.ShapeDtypeStruct((B, 2, G, S, HEAD_DIM), BF16),
            wide(G), wide(G), wide(G), wide(G),
            pair_shape(FOX_HEADS), wide(FOX_HEADS), pair_shape(FOX_HEADS),
            jax.ShapeDtypeStruct((B, S, D), BF16), jax.ShapeDtypeStruct((B, S, D), BF16),
            jax.ShapeDtypeStruct((B, 2, S, LANES), F32),
        ],
        scratch_shapes=[pltpu.VMEM((SUBLANES, LANES), F32)],
        compiler_params=_params("parallel", "arbitrary"),
        name="inproj",
    )(x, norm_g, w_all, rope_c, rope_s1, rope_s2, f_bias_row)


def _compress_kernel(r_ref, pe_ref, w1_ref, w2_ref, o_ref):
    half = CMP_STRIDE * HEAD_DIM
    rows = r_ref[0, 0, 0].astype(F32)
    lo = (rows + pe_ref[0, :, :half]).astype(BF16)
    hi = (rows + pe_ref[0, :, half:]).astype(BF16)
    a = _dot(lo, w1_ref[0, :half])
    b = _dot(hi, w1_ref[0, half:])
    n = a.shape[0]
    pre = a + pltpu.roll(b, n - 1, 0)
    act = jax.nn.gelu(pre, approximate=True).astype(BF16)
    o_ref[0, 0, 0] = _dot(act, w2_ref[0]).astype(BF16)


def _compress(kv_rows, pe, w1, w2):
    B, _, G, n_rows, width = kv_rows.shape
    return pl.pallas_call(
        _compress_kernel,
        grid=(B, 2, G),
        in_specs=[
            pl.BlockSpec((1, 1, 1, n_rows, width), lambda b, t, g: (b, t, g, 0, 0)),
            pl.BlockSpec((1, 1, 2 * width), lambda b, t, g: (t, 0, 0)),
            pl.BlockSpec((1, 2 * width, LANES), lambda b, t, g: (t, 0, 0)),
            pl.BlockSpec((1, LANES, HEAD_DIM), lambda b, t, g: (t, 0, 0)),
        ],
        out_specs=pl.BlockSpec((1, 1, 1, n_rows, HEAD_DIM), lambda b, t, g: (b, t, g, 0, 0)),
        out_shape=jax.ShapeDtypeStruct((B, 2, G, n_rows, HEAD_DIM), BF16),
        compiler_params=_params("parallel", "parallel", "parallel"),
        name="nsa_compress",
    )(kv_rows, pe, w1, w2)


def _softmax_rows(logits, ok):
    l = jnp.where(ok, logits, NEG)
    m = jnp.max(l, axis=-1, keepdims=True)
    e = jnp.where(ok, jnp.exp(l - m), 0.0)
    s = jnp.sum(e, axis=-1, keepdims=True)
    return e / jnp.where(s > 0.0, s, 1.0)


def _nsa_kernel(qraw_ref, qrot_ref, kcmp_ref, vcmp_ref, ks_ref, vs_ref, kw_ref, vw_ref,
                misc_ref, overlap_ref, o_ref, *, seq, chunk):
    tq_n = NSA_Q_TILE
    hpg = HEADS_PER_GROUP
    rows_n = hpg * tq_n
    n_sb = seq // SEL_BLOCK
    n_sel = min(N_SEL, n_sb)
    qi = pl.program_id(2)
    t0 = qi * tq_n
    tq = t0 + lax.broadcasted_iota(I32, (tq_n, 1), 0)

    q = qraw_ref[0].reshape(rows_n, HEAD_DIM)
    n_c = kcmp_ref.shape[3]
    lg = _dot_nt(q, kcmp_ref[0, 0, 0])
    cidx = lax.broadcasted_iota(I32, (tq_n, n_c), 1)
    cmask = cidx * CMP_STRIDE + (CMP_BLOCK - 1) <= tq
    vcmp = vcmp_ref[0, 0, 0]
    o_c = []
    p_sum = jnp.zeros((tq_n, n_c), F32)
    for h in range(hpg):
        p = _softmax_rows(lg[h * tq_n:(h + 1) * tq_n], cmask)
        p_sum = p_sum + p
        o_c.append(_dot(p.astype(BF16), vcmp))

    p_hi = p_sum.astype(BF16)
    p_lo = (p_sum - p_hi.astype(F32)).astype(BF16)
    imp = (_dot_nt(overlap_ref[...], p_hi) + _dot_nt(overlap_ref[...], p_lo))[:n_sb]
    j = lax.broadcasted_iota(I32, (n_sb, tq_n), 0)
    t_row = t0 + lax.broadcasted_iota(I32, (n_sb, tq_n), 1)
    cur = t_row >> (SEL_BLOCK.bit_length() - 1)
    forced = jnp.where(j == 0, 1.0, jnp.where(j == cur, 1.0, jnp.where(j == cur - 1, 1.0, 0.0)))
    score = jnp.where(forced > 0.0, imp + FORCE_BONUS, jnp.where(j * SEL_BLOCK <= t_row, imp, NEG))
    beaten = jnp.zeros((n_sb, tq_n), F32)
    for jp in range(n_sb):
        row = score[jp:jp + 1, :]
        wins_ties = jnp.where(row >= score, 1.0, 0.0)
        wins = jnp.where(row > score, 1.0, 0.0)
        beaten = beaten + jnp.where(j > jp, wins_ties, wins)
    penalty = jnp.where(beaten < n_sel, 0.0, NEG)
    penalty = jnp.concatenate([penalty, jnp.zeros((LANES - n_sb, tq_n), F32)], axis=0).T
    penalty = pltpu.roll(penalty, HEAD_DIM, 1)

    qrot = qrot_ref[0].reshape(rows_n, LANES).astype(F32)
    lane_r = lax.broadcasted_iota(I32, (rows_n, LANES), 1)
    q_sel = jnp.where(lane_r < HEAD_DIM, qrot, jnp.concatenate([penalty] * hpg, axis=0)).astype(BF16)
    q_win = jnp.where(lane_r == HEAD_DIM, 1.0, qrot).astype(BF16)
    row_in_tile = lax.broadcasted_iota(I32, (rows_n, 1), 0) & (tq_n - 1)

    def sel_logits(c):
        return _dot_nt(q_sel, ks_ref[0, 0, pl.ds(pl.multiple_of(c * chunk, chunk), chunk), :])

    def sel_update(c, s, m, acc):
        m_new = jnp.maximum(m, jnp.broadcast_to(jnp.max(s, axis=-1, keepdims=True), m.shape))
        p = jnp.exp(s - jnp.tile(m_new, (1, chunk // LANES)))
        v = vs_ref[0, 0, pl.ds(pl.multiple_of(c * chunk, chunk), chunk), :]
        return m_new, jnp.exp(m - m_new) * acc + _dot(p.astype(BF16), v)

    def sel_body(c, st):
        s, m, acc = st
        s_next = sel_logits(c + 1)
        m, acc = sel_update(c, s, m, acc)
        return s_next, m, acc

    n_full = t0 // chunk
    init = (sel_logits(0), jnp.full((rows_n, LANES), NEG, F32), jnp.zeros((rows_n, LANES), F32))
    s, m, acc = lax.fori_loop(0, n_full, sel_body, init)
    kpos = n_full * chunk + lax.broadcasted_iota(I32, (rows_n, chunk), 1)
    s = jnp.where(kpos <= t0 + row_in_tile, s, NEG)
    _, acc_s = sel_update(n_full, s, m, acc)

    wlen = WINDOW + tq_n
    w0 = pl.multiple_of(t0, tq_n)
    kw = kw_ref[0, 0, pl.ds(w0, wlen), :]
    vw = vw_ref[0, 0, pl.ds(w0, wlen), :]
    sw = _dot_nt(q_win, kw)
    col = lax.broadcasted_iota(I32, (rows_n, tq_n), 1)
    s_old = jnp.where(col > row_in_tile, sw[:, :tq_n], NEG)
    s_new = jnp.where(col <= row_in_tile, sw[:, WINDOW:], NEG)
    sw = jnp.concatenate([s_old, sw[:, tq_n:WINDOW], s_new], axis=1)
    mw = jnp.broadcast_to(jnp.max(sw, axis=-1, keepdims=True), (rows_n, LANES))
    acc_w = _dot(jnp.exp(sw - jnp.tile(mw, (1, wlen // LANES))).astype(BF16), vw)

    g = misc_ref[0, 0]
    for h in range(hpg):
        a_s = acc_s[h * tq_n:(h + 1) * tq_n]
        a_w = acc_w[h * tq_n:(h + 1) * tq_n]
        o_s = a_s[:, :HEAD_DIM] / a_s[:, HEAD_DIM:HEAD_DIM + 1]
        o_w = a_w[:, :HEAD_DIM] / a_w[:, HEAD_DIM:HEAD_DIM + 1]
        o_h = g[:, 3 * h:3 * h + 1] * o_c[h] + g[:, 3 * h + 1:3 * h + 2] * o_s + g[:, 3 * h + 2:3 * h + 3] * o_w
        o_ref[0, :, HEAD_DIM * h:HEAD_DIM * (h + 1)] = o_h.astype(BF16)


def _nsa(qraw, qrot, cmp_kv, ks, vs, kw_pad, vw_pad, misc, overlap):
    B, _, S, _ = qraw.shape
    G = NSA_KV_GROUPS
    n_c = cmp_kv.shape[3]
    chunk = min(KV_CHUNK, S)
    kvspec = pl.BlockSpec((1, 1, S, LANES), lambda b, g, i: (b, g, 0, 0))
    padspec = pl.BlockSpec((1, 1, S + WINDOW, LANES), lambda b, g, i: (b, g, 0, 0))
    return pl.pallas_call(
        functools.partial(_nsa_kernel, seq=S, chunk=chunk),
        grid=(B, G, S // NSA_Q_TILE),
        in_specs=[
            pl.BlockSpec((1, HEADS_PER_GROUP, NSA_Q_TILE, HEAD_DIM), lambda b, g, i: (b, g, i, 0)),
            pl.BlockSpec((1, HEADS_PER_GROUP, NSA_Q_TILE, LANES), lambda b, g, i: (b, g, i, 0)),
            pl.BlockSpec((1, 1, 1, n_c, HEAD_DIM), lambda b, g, i: (b, 0, g, 0, 0)),
            pl.BlockSpec((1, 1, 1, n_c, HEAD_DIM), lambda b, g, i: (b, 1, g, 0, 0)),
            kvspec, kvspec, padspec, padspec,
            pl.BlockSpec((1, 1, NSA_Q_TILE, LANES), lambda b, g, i: (b, g, i, 0)),
            pl.BlockSpec((LANES, n_c), lambda b, g, i: (0, 0)),
        ],
        out_specs=pl.BlockSpec((1, NSA_Q_TILE, HEADS_PER_GROUP * HEAD_DIM), lambda b, g, i: (b, i, g)),
        out_shape=jax.ShapeDtypeStruct((B, S, NSA_HEADS * HEAD_DIM), BF16),
        compiler_params=_params("parallel", "parallel", "arbitrary"),
        name="nsa_attention",
    )(qraw, qrot, cmp_kv, cmp_kv, ks, vs, kw_pad, vw_pad, misc, overlap)


def _fox_kernel(q_ref, k_ref, v_ref, o_ref, m_scr, acc_scr):
    tile = q_ref.shape[2]
    qi = pl.program_id(1)
    t0 = pl.multiple_of(qi * tile, tile)
    causal = (lax.broadcasted_iota(I32, (tile, tile), 1) <= lax.broadcasted_iota(I32, (tile, tile), 0))
    m_scr[...] = jnp.full_like(m_scr, NEG)
    acc_scr[...] = jnp.zeros_like(acc_scr)

    def step(k0, width, diagonal):
        for h in range(FOX_HEADS):
            s = _dot_nt(q_ref[0, h], k_ref[0, h, pl.ds(k0, width), :])
            if diagonal:
                s = jnp.where(causal, s, NEG)
            m = m_scr[h]
            m_new = jnp.maximum(m, jnp.broadcast_to(jnp.max(s, axis=-1, keepdims=True), m.shape))
            p = jnp.exp(s - jnp.tile(m_new, (1, width // LANES)))
            acc_scr[h] = jnp.exp(m - m_new) * acc_scr[h] + _dot(p.astype(BF16), v_ref[0, h, pl.ds(k0, width), :])
            m_scr[h] = m_new

    def body(c, _):
        step(pl.multiple_of(c * tile, tile), tile, False)
        return 0

    lax.fori_loop(0, qi, body, 0)
    step(t0, tile, True)
    for h in range(FOX_HEADS):
        acc = acc_scr[h]
        o_ref[0, :, HEAD_DIM * h:HEAD_DIM * (h + 1)] = (acc[:, :HEAD_DIM] / acc[:, HEAD_DIM:HEAD_DIM + 1]).astype(BF16)


def _fox(fq, fk, fv):
    B, H, S, _ = fq.shape
    tile = min(FOX_TILE, S)
    return pl.pallas_call(
        _fox_kernel,
        grid=(B, S // tile),
        in_specs=[
            pl.BlockSpec((1, H, tile, LANES), lambda b, i: (b, 0, i, 0)),
            pl.BlockSpec((1, H, S, LANES), lambda b, i: (b, 0, 0, 0)),
            pl.BlockSpec((1, H, S, LANES), lambda b, i: (b, 0, 0, 0)),
        ],
        out_specs=pl.BlockSpec((1, tile, H * HEAD_DIM), lambda b, i: (b, i, 0)),
        out_shape=jax.ShapeDtypeStruct((B, S, H * HEAD_DIM), BF16),
        scratch_shapes=[pltpu.VMEM((H, tile, LANES), F32), pltpu.VMEM((H, tile, LANES), F32)],
        compiler_params=_params("parallel", "arbitrary"),
        name="fox_attention",
    )(fq, fk, fv)


def _merge_kernel(x_ref, on_ref, of_ref, sgn_ref, sgf_ref, wn_ref, wf_ref, wo_ref, g2_ref,
                  wrh_ref, wrl_ref, br_ref, x1_ref, h2_ref, ri_ref, rw_ref, cnt_ref, carry_scr):
    i = pl.program_id(0)
    tm = x_ref.shape[0]

    @pl.when(i == 0)
    def _():
        carry_scr[...] = jnp.zeros_like(carry_scr)

    a = _dot(on_ref[...], wn_ref[...])
    b = _dot(of_ref[...], wf_ref[...])
    mixed = (sgn_ref[...].astype(F32) * a + sgf_ref[...].astype(F32) * b).astype(BF16)
    x1 = x_ref[...] + _dot(mixed, wo_ref[...])
    x1_ref[...] = x1
    h2 = _rms(x1, g2_ref[...])
    h2_ref[...] = h2

    hh = h2.astype(BF16)
    hl = (h2 - hh.astype(F32)).astype(BF16)
    logits = _dot(hh, wrh_ref[...]) + _dot(hl, wrh_ref[...]) + _dot(hh, wrl_ref[...]) + br_ref[...]
    lane = lax.broadcasted_iota(I32, (tm, LANES), 1)
    lane_f = lane.astype(F32)
    l = jnp.where(lane < N_EXPERTS, logits, BELOW_NEG)
    idxs, vals, hots = [], [], []
    for _ in range(TOP_K):
        m = jnp.max(l, axis=-1, keepdims=True)
        idx = jnp.min(jnp.where(l == m, lane_f, float(LANES)), axis=-1, keepdims=True)
        hot = lane_f == idx
        idxs.append(idx)
        vals.append(m)
        hots.append(hot)
        l = jnp.where(hot, BELOW_NEG, l)
    exps = [jnp.exp(v - vals[0]) for v in vals]
    den = exps[0]
    for e in exps[1:]:
        den = den + e

    chosen = jnp.zeros((tm, LANES), F32)
    for hot in hots:
        chosen = chosen + jnp.where(hot, 1.0, 0.0)
    r_i = lax.broadcasted_iota(I32, (tm, tm), 0)
    c_i = lax.broadcasted_iota(I32, (tm, tm), 1)
    earlier = jnp.where(c_i < r_i, 1.0, 0.0).astype(BF16)
    before = _dot(earlier, chosen.astype(BF16)) + carry_scr[0:1, :]
    carry_scr[...] = carry_scr[...] + jnp.sum(chosen, axis=0, keepdims=True)
    cnt_ref[...] = carry_scr[...]

    ri = jnp.zeros((tm, LANES), F32)
    rw = jnp.zeros((tm, LANES), F32)
    for k in range(TOP_K):
        rank = jnp.sum(jnp.where(hots[k], before, 0.0), axis=-1, keepdims=True)
        ri = jnp.where(lane == k, idxs[k], ri)
        ri = jnp.where(lane == TOP_K + k, rank, ri)
        rw = jnp.where(lane == k, exps[k] / den, rw)
    ri_ref[...] = ri.astype(I32)
    rw_ref[...] = rw


def _merge(x2d, o_nsa, o_fox, sgn, sgf, wn, wf, wo, g2, wr_hi, wr_lo, br, tm):
    T, D = x2d.shape
    wq = o_nsa.shape[1]
    row = lambda w: pl.BlockSpec((tm, w), lambda i: (i, 0))
    full = lambda a: pl.BlockSpec(a.shape, lambda i: (0,) * a.ndim)
    return pl.pallas_call(
        _merge_kernel,
        grid=(T // tm,),
        in_specs=[row(D), row(wq), row(wq), row(D), row(D), full(wn), full(wf), full(wo), full(g2),
                  full(wr_hi), full(wr_lo), full(br)],
        out_specs=[row(D), row(D), row(LANES), row(LANES), pl.BlockSpec((SUBLANES, LANES), lambda i: (0, 0))],
        out_shape=[
            jax.ShapeDtypeStruct((T, D), F32), jax.ShapeDtypeStruct((T, D), F32),
            jax.ShapeDtypeStruct((T, LANES), I32), jax.ShapeDtypeStruct((T, LANES), F32),
            jax.ShapeDtypeStruct((SUBLANES, LANES), F32),
        ],
        scratch_shapes=[pltpu.VMEM((SUBLANES, LANES), F32)],
        compiler_params=_params("arbitrary"),
        name="merge_router",
    )(x2d, o_nsa, o_fox, sgn, sgf, wn, wf, wo, g2, wr_hi, wr_lo, br)


def _dest_kernel(ri_ref, cnt_ref, dest_ref, tab_ref):
    tm = ri_ref.shape[0]
    shift = MOE_ROWS.bit_length() - 1
    cnt = cnt_ref[...].astype(I32)
    padded = ((cnt + (MOE_ROWS - 1)) >> shift) << shift
    lane8 = lax.broadcasted_iota(I32, (SUBLANES, LANES), 1)
    pend = padded
    sh = 1
    while sh < LANES:
        pend = pend + jnp.where(lane8 >= sh, pltpu.roll(pend, sh, 1), 0)
        sh *= 2
    pstart = pend - padded
    row8 = lax.broadcasted_iota(I32, (SUBLANES, LANES), 0)
    tab_ref[...] = jnp.where(row8 == 0, cnt, jnp.where(row8 == 1, padded, jnp.where(row8 == 2, pstart, pend)))

    ri = ri_ref[...]
    lane = lax.broadcasted_iota(I32, (tm, LANES), 1)
    start_row = pstart[0:1, :].astype(F32)
    dest = jnp.zeros((tm, LANES), F32)
    for k in range(TOP_K):
        hot = lane == ri[:, k:k + 1]
        base = jnp.sum(jnp.where(hot, start_row, 0.0), axis=-1, keepdims=True)
        dest = jnp.where(lane == k, base + ri[:, TOP_K + k:TOP_K + k + 1].astype(F32), dest)
    dest_ref[...] = dest.astype(I32)


def _dest(ri, cnt, tm):
    T = ri.shape[0]
    return pl.pallas_call(
        _dest_kernel,
        grid=(T // tm,),
        in_specs=[pl.BlockSpec((tm, LANES), lambda i: (i, 0)), pl.BlockSpec((SUBLANES, LANES), lambda i: (0, 0))],
        out_specs=[pl.BlockSpec((tm, LANES), lambda i: (i, 0)), pl.BlockSpec((SUBLANES, LANES), lambda i: (0, 0))],
        out_shape=[jax.ShapeDtypeStruct((T, LANES), I32), jax.ShapeDtypeStruct((SUBLANES, LANES), I32)],
        compiler_params=_params("arbitrary"),
        name="moe_dest",
    )(ri, cnt)


def _pad_copies(tab_ref, zbuf, xin_hbm, zsem):
    out = []
    for e in range(N_EXPERTS):
        cnt = tab_ref[e]
        pad = tab_ref[LANES + e] - cnt
        base = tab_ref[2 * LANES + e] + cnt
        piece = 1
        while piece < MOE_ROWS:
            cond = (pad & piece) != 0
            if piece < SUBLANES:
                for r in range(piece):
                    out.append((cond, pltpu.make_async_copy(
                        zbuf.at[pl.ds(0, 1)], xin_hbm.at[pl.ds(base + r, 1)], zsem)))
            else:
                out.append((cond, pltpu.make_async_copy(
                    zbuf.at[pl.ds(0, piece)], xin_hbm.at[pl.ds(pl.multiple_of(base, SUBLANES), piece)], zsem)))
            base = base + jnp.where(cond, piece, 0)
            piece *= 2
    last = N_EXPERTS - 1
    n_live = (tab_ref[2 * LANES + last] + tab_ref[LANES + last]) // MOE_ROWS
    n_blocks = xin_hbm.shape[0] // MOE_ROWS
    for blk in range(n_blocks - N_EXPERTS, n_blocks):
        out.append((blk >= n_live, pltpu.make_async_copy(
            zbuf, xin_hbm.at[pl.ds(blk * MOE_ROWS, MOE_ROWS)], zsem)))
    return out


def _dispatch_kernel(tab_ref, dest_hbm, h2_hbm, xin_hbm, idx_smem, hbuf, zbuf, lsem, ssem, isem, zsem):
    i = pl.program_id(0)
    n = pl.num_programs(0)
    tm = hbuf.shape[1]

    def load(j):
        return (pltpu.make_async_copy(h2_hbm.at[pl.ds(j * tm, tm)], hbuf.at[j % 3], lsem.at[j % 3]),
                pltpu.make_async_copy(dest_hbm.at[j], idx_smem.at[pl.ds((j % 3) * (TOP_K * tm), TOP_K * tm)],
                                      isem.at[j % 3]))

    def wait_scatters(j):
        for _ in range(TOP_K):
            pltpu.make_async_copy(hbuf.at[j % 3], xin_hbm.at[pl.ds(0, tm)], ssem.at[j % 3]).wait()

    @pl.when(i == 0)
    def _():
        for cp in load(0):
            cp.start()
        zbuf[...] = jnp.zeros_like(zbuf)
        for cond, cp in _pad_copies(tab_ref, zbuf, xin_hbm, zsem):
            @pl.when(cond)
            def _():
                cp.start()

    @pl.when(i + 1 < n)
    def _():
        for cp in load(i + 1):
            cp.start()

    for cp in load(i):
        cp.wait()
    for slot in range(3):
        @pl.when(i % 3 == slot)
        def _():
            def issue(t, _):
                for k in range(TOP_K):
                    d = idx_smem[slot * (TOP_K * tm) + TOP_K * t + k]
                    pltpu.make_async_copy(hbuf.at[slot, pl.ds(t, 1)], xin_hbm.at[pl.ds(d, 1)],
                                          ssem.at[slot]).start(priority=k % 2)
                return 0

            lax.fori_loop(0, tm, issue, 0)

    @pl.when(i > 0)
    def _():
        wait_scatters(i - 1)

    @pl.when(i == n - 1)
    def _():
        wait_scatters(i)
        for cond, cp in _pad_copies(tab_ref, zbuf, xin_hbm, zsem):
            @pl.when(cond)
            def _():
                cp.wait()


def _dispatch(tab_flat, dest_tiles, h2p, n_rows, tm):
    T, W = h2p.shape
    return pl.pallas_call(
        _dispatch_kernel,
        grid_spec=pltpu.PrefetchScalarGridSpec(
            num_scalar_prefetch=1,
            grid=(T // tm,),
            in_specs=[pl.BlockSpec(memory_space=pl.ANY), pl.BlockSpec(memory_space=pl.ANY)],
            out_specs=pl.BlockSpec(memory_space=pl.ANY),
            scratch_shapes=[
                pltpu.SMEM((3 * TOP_K * tm,), I32),
                pltpu.VMEM((3, tm, W), F32),
                pltpu.VMEM((MOE_ROWS, W), F32),
                pltpu.SemaphoreType.DMA((3,)), pltpu.SemaphoreType.DMA((3,)), pltpu.SemaphoreType.DMA((3,)),
                pltpu.SemaphoreType.DMA,
            ],
        ),
        out_shape=jax.ShapeDtypeStruct((n_rows, W), F32),
        compiler_params=_params("arbitrary"),
        name="moe_dispatch",
    )(tab_flat, dest_tiles, h2p)


def _expert_kernel(be_ref, bs_ref, nv_ref, x_ref, w1_ref, b1_ref, w2_ref, b2_ref, y_ref, w1b, w2b, *, d_ff):
    s = pl.program_id(0)
    live = s < nv_ref[0]
    fresh = (s == 0) | (be_ref[s] != be_ref[jnp.maximum(s - 1, 0)])

    @pl.when(live & fresh)
    def _():
        w1b[...] = w1_ref[0].astype(BF16)
        w2b[...] = w2_ref[0].astype(BF16)

    @pl.when(live)
    def _():
        hc = _dot(x_ref[...].astype(BF16), w1b[...]) + b1_ref[0]
        gt = jnp.minimum(hc[:, :d_ff], SWIGLU_LIMIT)
        up = jnp.clip(hc[:, d_ff:], -SWIGLU_LIMIT, SWIGLU_LIMIT)
        glu = gt * jax.nn.sigmoid(SWIGLU_ALPHA * gt)
        act = ((up + 1.0) * glu).astype(BF16)
        y_ref[...] = _dot(act, w2b[...]) + b2_ref[0]

    @pl.when(jnp.logical_not(live))
    def _():
        y_ref[...] = jnp.zeros_like(y_ref)


def _experts(blk_e, blk_src, n_live, xin, w1, b1, w2, b2):
    n_rows, W = xin.shape
    E, D, two_f = w1.shape
    d_ff = two_f // 2
    return pl.pallas_call(
        functools.partial(_expert_kernel, d_ff=d_ff),
        grid_spec=pltpu.PrefetchScalarGridSpec(
            num_scalar_prefetch=3,
            grid=(n_rows // MOE_ROWS,),
            in_specs=[
                pl.BlockSpec((MOE_ROWS, W), lambda s, be, bs, nv: (bs[s], 0)),
                pl.BlockSpec((1, D, two_f), lambda s, be, bs, nv: (be[s], 0, 0)),
                pl.BlockSpec((1, 1, two_f), lambda s, be, bs, nv: (be[s], 0, 0)),
                pl.BlockSpec((1, d_ff, D), lambda s, be, bs, nv: (be[s], 0, 0)),
                pl.BlockSpec((1, 1, D), lambda s, be, bs, nv: (be[s], 0, 0)),
            ],
            out_specs=pl.BlockSpec((MOE_ROWS, W), lambda s, be, bs, nv: (s, 0)),
            scratch_shapes=[pltpu.VMEM((D, two_f), BF16), pltpu.VMEM((d_ff, D), BF16)],
        ),
        out_shape=jax.ShapeDtypeStruct((n_rows, W), F32),
        compiler_params=_params("arbitrary"),
        name="moe_experts",
    )(blk_e, blk_src, n_live, xin, w1, b1, w2, b2)


def _combine_kernel(dest_hbm, x1_ref, rw_ref, gf_ref, y_hbm, o_ref, idx_smem, ybuf, gsem, isem):
    i = pl.program_id(0)
    n = pl.num_programs(0)
    tm = x1_ref.shape[0]

    def idx_copy(j):
        return pltpu.make_async_copy(dest_hbm.at[j], idx_smem.at[pl.ds((j % 2) * (TOP_K * tm), TOP_K * tm)],
                                     isem.at[j % 2])

    def issue_into(slot):
        def issue(t, _):
            for k in range(TOP_K):
                d = idx_smem[slot * (TOP_K * tm) + TOP_K * t + k]
                pltpu.make_async_copy(y_hbm.at[pl.ds(d, 1)], ybuf.at[slot, k, pl.ds(t, 1)],
                                      gsem.at[slot]).start(priority=k % 2)
            return 0

        lax.fori_loop(0, tm, issue, 0)

    def issue_gathers(j):
        for slot in range(2):
            @pl.when(j % 2 == slot)
            def _():
                issue_into(slot)

    @pl.when(i == 0)
    def _():
        idx_copy(0).start()
        idx_copy(0).wait()
        issue_into(0)

        @pl.when(n > 1)
        def _():
            idx_copy(1).start()

    @pl.when(i + 1 < n)
    def _():
        idx_copy(i + 1).wait()
        issue_gathers(i + 1)

        @pl.when(i + 2 < n)
        def _():
            idx_copy(i + 2).start()

    slot = i % 2
    for k in range(TOP_K):
        pltpu.make_async_copy(y_hbm.at[pl.ds(0, tm)], ybuf.at[slot, k], gsem.at[slot]).wait()

    rw = rw_ref[...]
    x2 = x1_ref[...]
    for k in range(TOP_K):
        x2 = x2 + rw[:, k:k + 1] * ybuf[slot, k]
    o_ref[...] = _rms(x2, gf_ref[...])


def _combine(dest_tiles, x1, rw, gf, y, tm):
    T, D = x1.shape
    W = y.shape[1]
    return pl.pallas_call(
        _combine_kernel,
        grid=(T // tm,),
        in_specs=[
            pl.BlockSpec(memory_space=pl.ANY),
            pl.BlockSpec((tm, D), lambda i: (i, 0)),
            pl.BlockSpec((tm, LANES), lambda i: (i, 0)),
            pl.BlockSpec((1, D), lambda i: (0, 0)),
            pl.BlockSpec(memory_space=pl.ANY),
        ],
        out_specs=pl.BlockSpec((tm, D), lambda i: (i, 0)),
        out_shape=jax.ShapeDtypeStruct((T, D), F32),
        scratch_shapes=[
            pltpu.SMEM((2 * TOP_K * tm,), I32),
            pltpu.VMEM((2, TOP_K, tm, W), F32),
            pltpu.SemaphoreType.DMA((2,)), pltpu.SemaphoreType.DMA((2,)),
        ],
        compiler_params=_params("arbitrary"),
        name="moe_combine",
    )(dest_tiles, x1, rw, gf, y)


def _rope_tables(positions):
    half = ROT_DIM // 2
    freqs = ROPE_THETA ** (-jnp.arange(0, ROT_DIM, 2, dtype=F32) / ROT_DIM)
    ang = positions.astype(F32)[..., None] * freqs
    cos, sin = jnp.cos(ang), jnp.sin(ang)
    d = np.arange(LANES) % HEAD_DIM
    f_idx = d % half
    in_lo = jnp.asarray(d < half)
    in_hi = jnp.asarray((d >= half) & (d < ROT_DIM))
    cos_l, sin_l = cos[..., f_idx], sin[..., f_idx]
    c = jnp.where(in_lo | in_hi, cos_l, 1.0)
    s1 = jnp.where(in_lo, -sin_l, 0.0)
    s2 = jnp.where(in_hi, sin_l, 0.0)
    return c, s1, s2


def _arrange_w_in(w_in, d_model):
    nq, nkv, ng = NSA_HEADS * HEAD_DIM, NSA_KV_GROUPS * HEAD_DIM, NSA_HEADS * 3
    fw = FOX_HEADS * HEAD_DIM
    sizes = [nq] + [nkv] * 6 + [ng, fw, fw, fw, FOX_HEADS, d_model, d_model]
    offs = np.concatenate([[0], np.cumsum(sizes)])
    piece = lambda k: w_in[:, offs[k]:offs[k + 1]]
    gates, ff = piece(7), piece(11)
    per_group = HEADS_PER_GROUP * 3
    zeros = lambda n: jnp.zeros((w_in.shape[0], n), w_in.dtype)
    misc0 = jnp.concatenate([gates[:, :per_group], zeros(FGATE_LANE - per_group), ff,
                             zeros(LANES - FGATE_LANE - FOX_HEADS)], axis=1)
    misc1 = jnp.concatenate([gates[:, per_group:], zeros(LANES - per_group)], axis=1)
    cols = [piece(k) for k in range(7)] + [piece(8), piece(9), piece(10), piece(12), piece(13), misc0, misc1]
    return jnp.concatenate(cols, axis=1).astype(BF16)


def _split_bf16(a):
    hi = a.astype(BF16)
    return hi, (a - hi.astype(F32)).astype(BF16)


def kernel(x, positions, norm1_g, w_in, cmp_pos_emb, cmp_w1, cmp_w2, fox_f_bias, w_proj_nsa, w_proj_fox, w_out,
           norm2_g, router_w, router_b, expert_w1, expert_b1, expert_w2, expert_b2, norm_f_g):
    B, S, D = x.shape
    T = B * S
    assert norm1_g.shape[0] == 1, "the combine kernel fuses the output norm, so it must follow the only layer"
    assert S % KV_CHUNK == 0 or S < KV_CHUNK
    assert S // SEL_BLOCK <= LANES - HEAD_DIM and S % (CMP_STRIDE * SUBLANES) == 0
    tm_proj = min(512, S)
    assert WINDOW % tm_proj == 0 or tm_proj % WINDOW == 0
    tm_tok = min(512, T)
    tm_comb = min(256, T)
    l = 0

    rope_c, rope_s1, rope_s2 = _rope_tables(positions)
    n_sb = S // SEL_BLOCK
    n_rows16 = S // CMP_STRIDE
    ci = np.arange(n_rows16)[None, :] * CMP_STRIDE
    sj = np.arange(LANES)[:, None] * SEL_BLOCK
    overlap = jnp.asarray(((ci < sj + SEL_BLOCK) & (ci + CMP_BLOCK > sj) & (np.arange(LANES)[:, None] < n_sb)
                           & (np.arange(n_rows16)[None, :] < n_rows16 - 1)).astype(np.float32), BF16)

    w_all = _arrange_w_in(w_in[l], D)
    fb_row = jnp.zeros((1, LANES), F32).at[0, FGATE_LANE:FGATE_LANE + FOX_HEADS].set(fox_f_bias[l].astype(F32))
    (qraw, qrot, kvc, ks, vs, kw, vw, fq, fk, fv, sgn, sgf, misc) = _inproj(
        x, norm1_g[l][None, :], w_all, rope_c, rope_s1, rope_s2, fb_row, tm_proj)
    qrot = qrot.reshape(B, NSA_HEADS, S, LANES)
    fq = fq.reshape(B, FOX_HEADS, S, LANES)
    fv = fv.reshape(B, FOX_HEADS, S, LANES)

    kv_rows = kvc.reshape(B, 2, NSA_KV_GROUPS, n_rows16, CMP_STRIDE * HEAD_DIM)
    pe = cmp_pos_emb[l].reshape(2, 1, CMP_BLOCK * HEAD_DIM).astype(F32)
    cw1 = jnp.pad(cmp_w1[l], ((0, 0), (0, 0), (0, LANES - HEAD_DIM))).astype(BF16)
    cw2 = jnp.pad(cmp_w2[l], ((0, 0), (0, LANES - HEAD_DIM), (0, 0))).astype(BF16)
    cmp_kv = _compress(kv_rows, pe, cw1, cw2)

    front_k = jnp.zeros((B, NSA_KV_GROUPS, WINDOW, LANES), BF16).at[..., HEAD_DIM].set(NEG)
    kw_pad = jnp.concatenate([front_k, kw], axis=2)
    vw_pad = jnp.concatenate([jnp.zeros((B, NSA_KV_GROUPS, WINDOW, LANES), BF16), vw], axis=2)
    o_nsa = _nsa(qraw, qrot, cmp_kv, ks, vs, kw_pad, vw_pad, misc, overlap)
    o_fox = _fox(fq, fk, fv)

    wr = jnp.pad(router_w[l], ((0, 0), (0, LANES - N_EXPERTS)))
    wr_hi, wr_lo = _split_bf16(wr)
    br = jnp.pad(router_b[l], (0, LANES - N_EXPERTS))[None, :].astype(F32)
    x1, h2, ri, rw, cnt = _merge(
        x.reshape(T, D), o_nsa.reshape(T, -1), o_fox.reshape(T, -1), sgn.reshape(T, D), sgf.reshape(T, D),
        w_proj_nsa[l].astype(BF16), w_proj_fox[l].astype(BF16), w_out[l].astype(BF16),
        norm2_g[l][None, :], wr_hi, wr_lo, br, tm_tok)

    dest, tab = _dest(ri, cnt, tm_tok)
    dest4 = dest[:, :TOP_K]
    pend = tab[3, :N_EXPERTS]
    n_blocks = (T * TOP_K) // MOE_ROWS + N_EXPERTS
    n_live = (pend[N_EXPERTS - 1] // MOE_ROWS).astype(I32)
    blk_src = jnp.minimum(jnp.arange(n_blocks, dtype=I32), n_live - 1)
    blk_e = jnp.sum((pend[None, :] <= (blk_src * MOE_ROWS)[:, None]).astype(I32), axis=1)
    blk_e = jnp.minimum(blk_e, N_EXPERTS - 1)

    xin = _dispatch(tab[:3].reshape(-1), dest4.reshape(T // tm_tok, TOP_K * tm_tok), h2,
                    n_blocks * MOE_ROWS, tm_tok)
    y = _experts(blk_e, blk_src, n_live[None], xin, expert_w1[l], expert_b1[l][:, None, :],
                 expert_w2[l], expert_b2[l][:, None, :])
    out = _combine(dest4.reshape(T // tm_comb, TOP_K * tm_comb), x1, rw, norm_f_g[None, :], y, tm_comb)
    return out.reshape(B, S, D)
```

```python
import functools

import numpy as np
import jax
import jax.numpy as jnp
from jax import lax
from jax.experimental import pallas as pl
from jax.experimental.pallas import tpu as pltpu

HEAD_DIM = 64
NSA_HEADS = 8
NSA_KV_GROUPS = 2
HEADS_PER_GROUP = NSA_HEADS // NSA_KV_GROUPS
FOX_HEADS = 8
ROT_DIM = HEAD_DIM // 4
ROPE_THETA = 500000.0
CMP_BLOCK = 32
CMP_STRIDE = 16
SEL_BLOCK = 64
N_SEL = 8
WINDOW = 512
N_EXPERTS = 32
TOP_K = 4
SWIGLU_ALPHA = 1.702
SWIGLU_LIMIT = 7.0
NORM_EPS = 1e-6
NEG = -1e30
BELOW_NEG = -3e38
FORCE_BONUS = 1e4

LANES = 128
SUBLANES = 8
NSA_Q_TILE = 128
KV_CHUNK = 512
FOX_TILE = 512
MOE_ROWS = 512
FGATE_LANE = 32
VMEM_LIMIT = 56 * 1024 * 1024

F32 = jnp.float32
BF16 = jnp.bfloat16
I32 = jnp.int32


def _dot(a, b):
    return jnp.dot(a, b, preferred_element_type=F32)


def _dot_nt(a, b):
    return lax.dot_general(a, b, (((1,), (1,)), ((), ())), preferred_element_type=F32)


def _params(*sem):
    return pltpu.CompilerParams(dimension_semantics=sem, vmem_limit_bytes=VMEM_LIMIT)


def _rms(x, g):
    return x * lax.rsqrt(jnp.mean(x * x, axis=-1, keepdims=True) + NORM_EPS) * g


def _split3(a):
    t1 = a.astype(BF16)
    r1 = a - t1.astype(F32)
    t2 = r1.astype(BF16)
    t3 = (r1 - t2.astype(F32)).astype(BF16)
    return t1, t2, t3


def _inproj_kernel(x_ref, g_ref, w_ref, rc_ref, rs1_ref, rs2_ref, fb_ref,
                   qraw_ref, qrot_ref, kvc_ref, ks_ref, vs_ref, kw_ref, vw_ref,
                   fq_ref, fk_ref, fv_ref, sgn_ref, sgf_ref, misc_ref, csum_scr, *, d_model):
    scale = HEAD_DIM ** -0.5
    i = pl.program_id(1)
    tm = x_ref.shape[1]
    xn = _rms(x_ref[0], g_ref[...]).astype(BF16)
    rc, rs1, rs2 = rc_ref[0], rs1_ref[0], rs2_ref[0]
    lane = lax.broadcasted_iota(I32, (tm, LANES), 1)
    low = lane < HEAD_DIM
    one_at_64 = jnp.where(lane == HEAD_DIM, 1.0, 0.0)

    def rope(slab):
        half = ROT_DIM // 2
        return slab * rc + pltpu.roll(slab, LANES - half, 1) * rs1 + pltpu.roll(slab, half, 1) * rs2

    def put_pair(ref, lead, slab, tail):
        ref[lead + (0,)] = jnp.where(low, slab, tail).astype(BF16)
        ref[lead + (1,)] = jnp.where(low, pltpu.roll(slab, HEAD_DIM, 1), tail).astype(BF16)

    misc_off = w_ref.shape[1] - 2 * LANES
    r = _dot(xn, w_ref[:, misc_off:])
    m0 = r[:, :LANES]
    z = m0 + fb_ref[...]
    log_f = jnp.minimum(z, 0.0) - jnp.log(1.0 + jnp.exp(-jnp.abs(z)))
    misc_ref[0, 0] = jax.nn.sigmoid(m0)
    misc_ref[0, 1] = jax.nn.sigmoid(r[:, LANES:])

    @pl.when(i == 0)
    def _():
        csum_scr[...] = jnp.zeros_like(csum_scr)

    in_gate = (lane >= FGATE_LANE) & (lane < FGATE_LANE + FOX_HEADS)
    t1, t2, t3 = _split3(jnp.where(in_gate, log_f, 0.0))
    r_i = lax.broadcasted_iota(I32, (tm, tm), 0)
    c_i = lax.broadcasted_iota(I32, (tm, tm), 1)
    upto = jnp.where(c_i <= r_i, 1.0, 0.0).astype(BF16)
    csum = _dot(upto, t1) + _dot(upto, t2) + _dot(upto, t3) + csum_scr[0:1, :]
    csum_scr[...] = jnp.broadcast_to(csum[tm - 1:tm, :], csum_scr.shape)

    nq = NSA_HEADS * HEAD_DIM
    r = _dot(xn, w_ref[:, 0:nq]) * scale
    for s in range(nq // LANES):
        slab = r[:, LANES * s:LANES * (s + 1)]
        qraw_ref[0, 2 * s] = slab[:, :HEAD_DIM].astype(BF16)
        qraw_ref[0, 2 * s + 1] = slab[:, HEAD_DIM:].astype(BF16)
        put_pair(qrot_ref, (0, s), rope(slab), 0.0)
    off = nq

    r = _dot(xn, w_ref[:, off:off + 6 * LANES])
    for t in range(2):
        slab = r[:, t * LANES:(t + 1) * LANES]
        kvc_ref[0, t, 0] = slab[:, :HEAD_DIM].astype(BF16)
        kvc_ref[0, t, 1] = slab[:, HEAD_DIM:].astype(BF16)
    tok = i * tm + lax.broadcasted_iota(I32, (tm, LANES), 0)
    block_hot = jnp.where(lane - HEAD_DIM == (tok >> (SEL_BLOCK.bit_length() - 1)), 1.0, 0.0)
    put_pair(ks_ref, (0,), rope(r[:, 2 * LANES:3 * LANES]), block_hot)
    put_pair(vs_ref, (0,), r[:, 3 * LANES:4 * LANES], one_at_64)
    put_pair(kw_ref, (0,), rope(r[:, 4 * LANES:5 * LANES]), 0.0)
    put_pair(vw_ref, (0,), r[:, 5 * LANES:6 * LANES], one_at_64)
    off += 6 * LANES

    fw = FOX_HEADS * HEAD_DIM
    ones3 = jnp.where((lane >= HEAD_DIM) & (lane < HEAD_DIM + 3), 1.0, 0.0)
    r = _dot(xn, w_ref[:, off:off + fw]) * scale
    for s in range(fw // LANES):
        put_pair(fq_ref, (0, s), r[:, LANES * s:LANES * (s + 1)], ones3)
    off += fw
    r = _dot(xn, w_ref[:, off:off + fw])
    for s in range(fw // LANES):
        slab = r[:, LANES * s:LANES * (s + 1)]
        for par in range(2):
            h = 2 * s + par
            neg_c = -csum[:, FGATE_LANE + h:FGATE_LANE + h + 1]
            c1, c2, c3 = _split3(neg_c)
            tail = jnp.where(lane == HEAD_DIM, c1.astype(F32), jnp.where(
                lane == HEAD_DIM + 1, c2.astype(F32), jnp.where(lane == HEAD_DIM + 2, c3.astype(F32), 0.0)))
            head = slab if par == 0 else pltpu.roll(slab, HEAD_DIM, 1)
            fk_ref[0, h] = jnp.where(low, head, tail).astype(BF16)
    off += fw
    r = _dot(xn, w_ref[:, off:off + fw])
    for s in range(fw // LANES):
        put_pair(fv_ref, (0, s), r[:, LANES * s:LANES * (s + 1)], one_at_64)
    off += fw

    step = min(512, d_model)
    for ref in (sgn_ref, sgf_ref):
        for c in range(0, d_model, step):
            r = _dot(xn, w_ref[:, off + c:off + c + step])
            ref[0, :, c:c + step] = jax.nn.sigmoid(r).astype(BF16)
        off += d_model


def _inproj(x, norm_g, w_all, rope_c, rope_s1, rope_s2, f_bias_row, tm):
    B, S, D = x.shape
    ncol = w_all.shape[1]
    G = NSA_KV_GROUPS
    wide = lambda heads: jax.ShapeDtypeStruct((B, heads, S, LANES), BF16)
    pair_shape = lambda heads: jax.ShapeDtypeStruct((B, heads // 2, 2, S, LANES), BF16)
    pair_spec = lambda heads: pl.BlockSpec((1, heads // 2, 2, tm, LANES), lambda b, i: (b, 0, 0, i, 0))
    kvspec = pl.BlockSpec((1, G, tm, LANES), lambda b, i: (b, 0, i, 0))
    tabspec = pl.BlockSpec((1, tm, LANES), lambda b, i: (b, i, 0))
    gspec = pl.BlockSpec((1, tm, D), lambda b, i: (b, i, 0))
    return pl.pallas_call(
        functools.partial(_inproj_kernel, d_model=D),
        grid=(B, S // tm),
        in_specs=[
            pl.BlockSpec((1, tm, D), lambda b, i: (b, i, 0)),
            pl.BlockSpec((1, D), lambda b, i: (0, 0)),
            pl.BlockSpec((D, ncol), lambda b, i: (0, 0)),
            tabspec, tabspec, tabspec,
            pl.BlockSpec((1, LANES), lambda b, i: (0, 0)),
        ],
        out_specs=[
            pl.BlockSpec((1, NSA_HEADS, tm, HEAD_DIM), lambda b, i: (b, 0, i, 0)),
            pair_spec(NSA_HEADS),
            pl.BlockSpec((1, 2, G, tm, HEAD_DIM), lambda b, i: (b, 0, 0, i, 0)),
            kvspec, kvspec, kvspec, kvspec,
            pair_spec(FOX_HEADS),
            pl.BlockSpec((1, FOX_HEADS, tm, LANES), lambda b, i: (b, 0, i, 0)),
            pair_spec(FOX_HEADS),
            gspec, gspec,
            pl.BlockSpec((1, 2, tm, LANES), lambda b, i: (b, 0, i, 0)),
        ],
        out_shape=[
            jax.ShapeDtypeStruct((B, NSA_HEADS, S, HEAD_DIM), BF16),
            pair_shape(NSA_HEADS),
            jax.ShapeDtypeStruct((B, 2, G, S, HEAD_DIM), BF16),
            wide(G), wide(G), wide(G), wide(G),
            pair_shape(FOX_HEADS), wide(FOX_HEADS), pair_shape(FOX_HEADS),
            jax.ShapeDtypeStruct((B, S, D), BF16), jax.ShapeDtypeStruct((B, S, D), BF16),
            jax.ShapeDtypeStruct((B, 2, S, LANES), F32),
        ],
        scratch_shapes=[pltpu.VMEM((SUBLANES, LANES), F32)],
        compiler_params=_params("parallel", "arbitrary"),
        name="inproj",
    )(x, norm_g, w_all, rope_c, rope_s1, rope_s2, f_bias_row)


def _compress_kernel(r_ref, pe_ref, w1_ref, w2_ref, o_ref):
    half = CMP_STRIDE * HEAD_DIM
    rows = r_ref[0, 0, 0].astype(F32)
    lo = (rows + pe_ref[0, :, :half]).astype(BF16)
    hi = (rows + pe_ref[0, :, half:]).astype(BF16)
    a = _dot(lo, w1_ref[0, :half])
    b = _dot(hi, w1_ref[0, half:])
    n = a.shape[0]
    pre = a + pltpu.roll(b, n - 1, 0)
    act = jax.nn.gelu(pre, approximate=True).astype(BF16)
    o_ref[0, 0, 0] = _dot(act, w2_ref[0]).astype(BF16)


def _compress(kv_rows, pe, w1, w2):
    B, _, G, n_rows, width = kv_rows.shape
    return pl.pallas_call(
        _compress_kernel,
        grid=(B, 2, G),
        in_specs=[
            pl.BlockSpec((1, 1, 1, n_rows, width), lambda b, t, g: (b, t, g, 0, 0)),
            pl.BlockSpec((1, 1, 2 * width), lambda b, t, g: (t, 0, 0)),
            pl.BlockSpec((1, 2 * width, LANES), lambda b, t, g: (t, 0, 0)),
            pl.BlockSpec((1, LANES, HEAD_DIM), lambda b, t, g: (t, 0, 0)),
        ],
        out_specs=pl.BlockSpec((1, 1, 1, n_rows, HEAD_DIM), lambda b, t, g: (b, t, g, 0, 0)),
        out_shape=jax.ShapeDtypeStruct((B, 2, G, n_rows, HEAD_DIM), BF16),
        compiler_params=_params("parallel", "parallel", "parallel"),
        name="nsa_compress",
    )(kv_rows, pe, w1, w2)


def _softmax_rows(logits, ok):
    l = jnp.where(ok, logits, NEG)
    m = jnp.max(l, axis=-1, keepdims=True)
    e = jnp.where(ok, jnp.exp(l - m), 0.0)
    s = jnp.sum(e, axis=-1, keepdims=True)
    return e / jnp.where(s > 0.0, s, 1.0)


def _nsa_kernel(qraw_ref, qrot_ref, kcmp_ref, vcmp_ref, ks_ref, vs_ref, kw_ref, vw_ref,
                misc_ref, overlap_ref, o_ref, *, seq, chunk):
    qi = pl.program_id(2)
    n_full = qi * NSA_Q_TILE // chunk
    for last in range(seq // chunk):
        @pl.when(n_full == last)
        def _():
            _nsa_tile(qraw_ref, qrot_ref, kcmp_ref, vcmp_ref, ks_ref, vs_ref, kw_ref, vw_ref,
                      misc_ref, overlap_ref, o_ref, qi, last, seq=seq, chunk=chunk)


def _nsa_tile(qraw_ref, qrot_ref, kcmp_ref, vcmp_ref, ks_ref, vs_ref, kw_ref, vw_ref,
              misc_ref, overlap_ref, o_ref, qi, last, *, seq, chunk):
    tq_n = NSA_Q_TILE
    hpg = HEADS_PER_GROUP
    rows_n = hpg * tq_n
    n_sb = seq // SEL_BLOCK
    n_sel = min(N_SEL, n_sb)
    t0 = qi * tq_n
    tq = t0 + lax.broadcasted_iota(I32, (tq_n, 1), 0)
    qrot = qrot_ref[0].reshape(rows_n, LANES).astype(F32)
    lane_r = lax.broadcasted_iota(I32, (rows_n, LANES), 1)
    row_in_tile = lax.broadcasted_iota(I32, (rows_n, 1), 0) & (tq_n - 1)

    q_win = jnp.where(lane_r == HEAD_DIM, 1.0, qrot).astype(BF16)
    wlen = WINDOW + tq_n
    w0 = pl.multiple_of(t0, tq_n)
    kw = kw_ref[0, 0, pl.ds(w0, wlen), :]
    vw = vw_ref[0, 0, pl.ds(w0, wlen), :]
    sw = _dot_nt(q_win, kw)
    col = lax.broadcasted_iota(I32, (rows_n, tq_n), 1)
    s_old = jnp.where(col > row_in_tile, sw[:, :tq_n], NEG)
    s_new = jnp.where(col <= row_in_tile, sw[:, WINDOW:], NEG)
    sw = jnp.concatenate([s_old, sw[:, tq_n:WINDOW], s_new], axis=1)
    mw = jnp.broadcast_to(jnp.max(sw, axis=-1, keepdims=True), (rows_n, LANES))
    acc_w = _dot(jnp.exp(sw - jnp.tile(mw, (1, wlen // LANES))).astype(BF16), vw)

    q = qraw_ref[0].reshape(rows_n, HEAD_DIM)
    n_c = kcmp_ref.shape[3]
    lg = _dot_nt(q, kcmp_ref[0, 0, 0])
    cidx = lax.broadcasted_iota(I32, (tq_n, n_c), 1)
    cmask = cidx * CMP_STRIDE + (CMP_BLOCK - 1) <= tq
    vcmp = vcmp_ref[0, 0, 0]
    o_c = []
    p_sum = jnp.zeros((tq_n, n_c), F32)
    for h in range(hpg):
        p = _softmax_rows(lg[h * tq_n:(h + 1) * tq_n], cmask)
        p_sum = p_sum + p
        o_c.append(_dot(p.astype(BF16), vcmp))

    p_hi = p_sum.astype(BF16)
    p_lo = (p_sum - p_hi.astype(F32)).astype(BF16)
    imp = (_dot_nt(overlap_ref[...], p_hi) + _dot_nt(overlap_ref[...], p_lo))[:n_sb]
    j = lax.broadcasted_iota(I32, (n_sb, tq_n), 0)
    t_row = t0 + lax.broadcasted_iota(I32, (n_sb, tq_n), 1)
    cur = t_row >> (SEL_BLOCK.bit_length() - 1)
    forced = jnp.where(j == 0, 1.0, jnp.where(j == cur, 1.0, jnp.where(j == cur - 1, 1.0, 0.0)))
    score = jnp.where(forced > 0.0, imp + FORCE_BONUS, jnp.where(j * SEL_BLOCK <= t_row, imp, NEG))
    beaten = jnp.zeros((n_sb, tq_n), F32)
    for jp in range(n_sb):
        row = score[jp:jp + 1, :]
        wins_ties = jnp.where(row >= score, 1.0, 0.0)
        wins = jnp.where(row > score, 1.0, 0.0)
        beaten = beaten + jnp.where(j > jp, wins_ties, wins)
    penalty = jnp.where(beaten < n_sel, 0.0, NEG)
    penalty = jnp.concatenate([penalty, jnp.zeros((LANES - n_sb, tq_n), F32)], axis=0).T
    penalty = pltpu.roll(penalty, HEAD_DIM, 1)

    q_sel = jnp.where(lane_r < HEAD_DIM, qrot, jnp.concatenate([penalty] * hpg, axis=0)).astype(BF16)

    logits = [_dot_nt(q_sel, ks_ref[0, 0, c * chunk:(c + 1) * chunk, :]) for c in range(last + 1)]
    kpos = last * chunk + lax.broadcasted_iota(I32, (rows_n, chunk), 1)
    logits[last] = jnp.where(kpos <= t0 + row_in_tile, logits[last], NEG)
    m = jnp.max(logits[0], axis=-1, keepdims=True)
    for s in logits[1:]:
        m = jnp.maximum(m, jnp.max(s, axis=-1, keepdims=True))
    m = jnp.tile(jnp.broadcast_to(m, (rows_n, LANES)), (1, chunk // LANES))
    acc_s = _dot(jnp.exp(logits[0] - m).astype(BF16), vs_ref[0, 0, 0:chunk, :])
    for c in range(1, last + 1):
        acc_s = acc_s + _dot(jnp.exp(logits[c] - m).astype(BF16), vs_ref[0, 0, c * chunk:(c + 1) * chunk, :])

    g = misc_ref[0, 0]
    for h in range(hpg):
        a_s = acc_s[h * tq_n:(h + 1) * tq_n]
        a_w = acc_w[h * tq_n:(h + 1) * tq_n]
        o_s = a_s[:, :HEAD_DIM] / a_s[:, HEAD_DIM:HEAD_DIM + 1]
        o_w = a_w[:, :HEAD_DIM] / a_w[:, HEAD_DIM:HEAD_DIM + 1]
        o_h = g[:, 3 * h:3 * h + 1] * o_c[h] + g[:, 3 * h + 1:3 * h + 2] * o_s + g[:, 3 * h + 2:3 * h + 3] * o_w
        o_ref[0, :, HEAD_DIM * h:HEAD_DIM * (h + 1)] = o_h.astype(BF16)


def _nsa(qraw, qrot, cmp_kv, ks, vs, kw_pad, vw_pad, misc, overlap):
    B, _, S, _ = qraw.shape
    G = NSA_KV_GROUPS
    n_c = cmp_kv.shape[3]
    chunk = min(KV_CHUNK, S)
    kvspec = pl.BlockSpec((1, 1, S, LANES), lambda b, g, i: (b, g, 0, 0))
    padspec = pl.BlockSpec((1, 1, S + WINDOW, LANES), lambda b, g, i: (b, g, 0, 0))
    return pl.pallas_call(
        functools.partial(_nsa_kernel, seq=S, chunk=chunk),
        grid=(B, G, S // NSA_Q_TILE),
        in_specs=[
            pl.BlockSpec((1, HEADS_PER_GROUP, NSA_Q_TILE, HEAD_DIM), lambda b, g, i: (b, g, i, 0)),
            pl.BlockSpec((1, HEADS_PER_GROUP, NSA_Q_TILE, LANES), lambda b, g, i: (b, g, i, 0)),
            pl.BlockSpec((1, 1, 1, n_c, HEAD_DIM), lambda b, g, i: (b, 0, g, 0, 0)),
            pl.BlockSpec((1, 1, 1, n_c, HEAD_DIM), lambda b, g, i: (b, 1, g, 0, 0)),
            kvspec, kvspec, padspec, padspec,
            pl.BlockSpec((1, 1, NSA_Q_TILE, LANES), lambda b, g, i: (b, g, i, 0)),
            pl.BlockSpec((LANES, n_c), lambda b, g, i: (0, 0)),
        ],
        out_specs=pl.BlockSpec((1, NSA_Q_TILE, HEADS_PER_GROUP * HEAD_DIM), lambda b, g, i: (b, i, g)),
        out_shape=jax.ShapeDtypeStruct((B, S, NSA_HEADS * HEAD_DIM), BF16),
        compiler_params=_params("parallel", "parallel", "arbitrary"),
        name="nsa_attention",
    )(qraw, qrot, cmp_kv, cmp_kv, ks, vs, kw_pad, vw_pad, misc, overlap)


def _fox_kernel(q_ref, k_ref, v_ref, o_ref, m_scr, acc_scr):
    tile = q_ref.shape[2]
    qi = pl.program_id(1)
    t0 = pl.multiple_of(qi * tile, tile)
    causal = (lax.broadcasted_iota(I32, (tile, tile), 1) <= lax.broadcasted_iota(I32, (tile, tile), 0))
    m_scr[...] = jnp.full_like(m_scr, NEG)
    acc_scr[...] = jnp.zeros_like(acc_scr)

    def step(k0, width, diagonal):
        for h in range(FOX_HEADS):
            s = _dot_nt(q_ref[0, h], k_ref[0, h, pl.ds(k0, width), :])
            if diagonal:
                s = jnp.where(causal, s, NEG)
            m = m_scr[h]
            m_new = jnp.maximum(m, jnp.broadcast_to(jnp.max(s, axis=-1, keepdims=True), m.shape))
            p = jnp.exp(s - jnp.tile(m_new, (1, width // LANES)))
            acc_scr[h] = jnp.exp(m - m_new) * acc_scr[h] + _dot(p.astype(BF16), v_ref[0, h, pl.ds(k0, width), :])
            m_scr[h] = m_new

    def body(c, _):
        step(pl.multiple_of(c * tile, tile), tile, False)
        return 0

    lax.fori_loop(0, qi, body, 0)
    step(t0, tile, True)
    for h in range(FOX_HEADS):
        acc = acc_scr[h]
        o_ref[0, :, HEAD_DIM * h:HEAD_DIM * (h + 1)] = (acc[:, :HEAD_DIM] / acc[:, HEAD_DIM:HEAD_DIM + 1]).astype(BF16)


def _fox(fq, fk, fv):
    B, H, S, _ = fq.shape
    tile = min(FOX_TILE, S)
    return pl.pallas_call(
        _fox_kernel,
        grid=(B, S // tile),
        in_specs=[
            pl.BlockSpec((1, H, tile, LANES), lambda b, i: (b, 0, i, 0)),
            pl.BlockSpec((1, H, S, LANES), lambda b, i: (b, 0, 0, 0)),
            pl.BlockSpec((1, H, S, LANES), lambda b, i: (b, 0, 0, 0)),
        ],
        out_specs=pl.BlockSpec((1, tile, H * HEAD_DIM), lambda b, i: (b, i, 0)),
        out_shape=jax.ShapeDtypeStruct((B, S, H * HEAD_DIM), BF16),
        scratch_shapes=[pltpu.VMEM((H, tile, LANES), F32), pltpu.VMEM((H, tile, LANES), F32)],
        compiler_params=_params("parallel", "arbitrary"),
        name="fox_attention",
    )(fq, fk, fv)


def _merge_kernel(x_ref, on_ref, of_ref, sgn_ref, sgf_ref, wn_ref, wf_ref, wo_ref, g2_ref,
                  wrh_ref, wrl_ref, br_ref, x1_ref, h2_ref, ri_ref, rw_ref, cnt_ref, carry_scr):
    i = pl.program_id(0)
    tm = x_ref.shape[0]

    @pl.when(i == 0)
    def _():
        carry_scr[...] = jnp.zeros_like(carry_scr)

    a = _dot(on_ref[...], wn_ref[...])
    b = _dot(of_ref[...], wf_ref[...])
    mixed = (sgn_ref[...].astype(F32) * a + sgf_ref[...].astype(F32) * b).astype(BF16)
    x1 = x_ref[...] + _dot(mixed, wo_ref[...])
    x1_ref[...] = x1
    h2 = _rms(x1, g2_ref[...])
    h2_ref[...] = h2

    hh = h2.astype(BF16)
    hl = (h2 - hh.astype(F32)).astype(BF16)
    logits = _dot(hh, wrh_ref[...]) + _dot(hl, wrh_ref[...]) + _dot(hh, wrl_ref[...]) + br_ref[...]
    lane = lax.broadcasted_iota(I32, (tm, LANES), 1)
    lane_f = lane.astype(F32)
    l = jnp.where(lane < N_EXPERTS, logits, BELOW_NEG)
    idxs, vals, hots = [], [], []
    for _ in range(TOP_K):
        m = jnp.max(l, axis=-1, keepdims=True)
        idx = jnp.min(jnp.where(l == m, lane_f, float(LANES)), axis=-1, keepdims=True)
        hot = lane_f == idx
        idxs.append(idx)
        vals.append(m)
        hots.append(hot)
        l = jnp.where(hot, BELOW_NEG, l)
    exps = [jnp.exp(v - vals[0]) for v in vals]
    den = exps[0]
    for e in exps[1:]:
        den = den + e

    chosen = jnp.zeros((tm, LANES), F32)
    for hot in hots:
        chosen = chosen + jnp.where(hot, 1.0, 0.0)
    r_i = lax.broadcasted_iota(I32, (tm, tm), 0)
    c_i = lax.broadcasted_iota(I32, (tm, tm), 1)
    earlier = jnp.where(c_i < r_i, 1.0, 0.0).astype(BF16)
    before = _dot(earlier, chosen.astype(BF16)) + carry_scr[0:1, :]
    carry_scr[...] = carry_scr[...] + jnp.sum(chosen, axis=0, keepdims=True)
    cnt_ref[...] = carry_scr[...]

    ri = jnp.zeros((tm, LANES), F32)
    rw = jnp.zeros((tm, LANES), F32)
    for k in range(TOP_K):
        rank = jnp.sum(jnp.where(hots[k], before, 0.0), axis=-1, keepdims=True)
        ri = jnp.where(lane == k, idxs[k], ri)
        ri = jnp.where(lane == TOP_K + k, rank, ri)
        rw = jnp.where(lane == k, exps[k] / den, rw)
    ri_ref[...] = ri.astype(I32)
    rw_ref[...] = rw


def _merge(x2d, o_nsa, o_fox, sgn, sgf, wn, wf, wo, g2, wr_hi, wr_lo, br, tm):
    T, D = x2d.shape
    wq = o_nsa.shape[1]
    row = lambda w: pl.BlockSpec((tm, w), lambda i: (i, 0))
    full = lambda a: pl.BlockSpec(a.shape, lambda i: (0,) * a.ndim)
    return pl.pallas_call(
        _merge_kernel,
        grid=(T // tm,),
        in_specs=[row(D), row(wq), row(wq), row(D), row(D), full(wn), full(wf), full(wo), full(g2),
                  full(wr_hi), full(wr_lo), full(br)],
        out_specs=[row(D), row(D), row(LANES), row(LANES), pl.BlockSpec((SUBLANES, LANES), lambda i: (0, 0))],
        out_shape=[
            jax.ShapeDtypeStruct((T, D), F32), jax.ShapeDtypeStruct((T, D), F32),
            jax.ShapeDtypeStruct((T, LANES), I32), jax.ShapeDtypeStruct((T, LANES), F32),
            jax.ShapeDtypeStruct((SUBLANES, LANES), F32),
        ],
        scratch_shapes=[pltpu.VMEM((SUBLANES, LANES), F32)],
        compiler_params=_params("arbitrary"),
        name="merge_router",
    )(x2d, o_nsa, o_fox, sgn, sgf, wn, wf, wo, g2, wr_hi, wr_lo, br)


def _dest_kernel(ri_ref, cnt_ref, dest_ref, tab_ref):
    tm = ri_ref.shape[0]
    shift = MOE_ROWS.bit_length() - 1
    cnt = cnt_ref[...].astype(I32)
    padded = ((cnt + (MOE_ROWS - 1)) >> shift) << shift
    lane8 = lax.broadcasted_iota(I32, (SUBLANES, LANES), 1)
    pend = padded
    sh = 1
    while sh < LANES:
        pend = pend + jnp.where(lane8 >= sh, pltpu.roll(pend, sh, 1), 0)
        sh *= 2
    pstart = pend - padded
    row8 = lax.broadcasted_iota(I32, (SUBLANES, LANES), 0)
    tab_ref[...] = jnp.where(row8 == 0, cnt, jnp.where(row8 == 1, padded, jnp.where(row8 == 2, pstart, pend)))

    ri = ri_ref[...]
    lane = lax.broadcasted_iota(I32, (tm, LANES), 1)
    start_row = pstart[0:1, :].astype(F32)
    dest = jnp.zeros((tm, LANES), F32)
    for k in range(TOP_K):
        hot = lane == ri[:, k:k + 1]
        base = jnp.sum(jnp.where(hot, start_row, 0.0), axis=-1, keepdims=True)
        dest = jnp.where(lane == k, base + ri[:, TOP_K + k:TOP_K + k + 1].astype(F32), dest)
    dest_ref[...] = dest.astype(I32)


def _dest(ri, cnt, tm):
    T = ri.shape[0]
    return pl.pallas_call(
        _dest_kernel,
        grid=(T // tm,),
        in_specs=[pl.BlockSpec((tm, LANES), lambda i: (i, 0)), pl.BlockSpec((SUBLANES, LANES), lambda i: (0, 0))],
        out_specs=[pl.BlockSpec((tm, LANES), lambda i: (i, 0)), pl.BlockSpec((SUBLANES, LANES), lambda i: (0, 0))],
        out_shape=[jax.ShapeDtypeStruct((T, LANES), I32), jax.ShapeDtypeStruct((SUBLANES, LANES), I32)],
        compiler_params=_params("arbitrary"),
        name="moe_dest",
    )(ri, cnt)


def _pad_copies(tab_ref, zbuf, xin_hbm, zsem):
    out = []
    for e in range(N_EXPERTS):
        cnt = tab_ref[e]
        pad = tab_ref[LANES + e] - cnt
        base = tab_ref[2 * LANES + e] + cnt
        piece = 1
        while piece < MOE_ROWS:
            cond = (pad & piece) != 0
            if piece < SUBLANES:
                for r in range(piece):
                    out.append((cond, pltpu.make_async_copy(
                        zbuf.at[pl.ds(0, 1)], xin_hbm.at[pl.ds(base + r, 1)], zsem)))
            else:
                out.append((cond, pltpu.make_async_copy(
                    zbuf.at[pl.ds(0, piece)], xin_hbm.at[pl.ds(pl.multiple_of(base, SUBLANES), piece)], zsem)))
            base = base + jnp.where(cond, piece, 0)
            piece *= 2
    last = N_EXPERTS - 1
    n_live = (tab_ref[2 * LANES + last] + tab_ref[LANES + last]) // MOE_ROWS
    n_blocks = xin_hbm.shape[0] // MOE_ROWS
    for blk in range(n_blocks - N_EXPERTS, n_blocks):
        out.append((blk >= n_live, pltpu.make_async_copy(
            zbuf, xin_hbm.at[pl.ds(blk * MOE_ROWS, MOE_ROWS)], zsem)))
    return out


def _dispatch_kernel(tab_ref, dest_hbm, h2_hbm, xin_hbm, idx_smem, hbuf, zbuf, lsem, ssem, isem, zsem):
    i = pl.program_id(0)
    n = pl.num_programs(0)
    tm = hbuf.shape[1]

    def load(j):
        return (pltpu.make_async_copy(h2_hbm.at[pl.ds(j * tm, tm)], hbuf.at[j % 3], lsem.at[j % 3]),
                pltpu.make_async_copy(dest_hbm.at[j], idx_smem.at[pl.ds((j % 3) * (TOP_K * tm), TOP_K * tm)],
                                      isem.at[j % 3]))

    def wait_scatters(j):
        for _ in range(TOP_K):
            pltpu.make_async_copy(hbuf.at[j % 3], xin_hbm.at[pl.ds(0, tm)], ssem.at[j % 3]).wait()

    @pl.when(i == 0)
    def _():
        for cp in load(0):
            cp.start()
        zbuf[...] = jnp.zeros_like(zbuf)
        for cond, cp in _pad_copies(tab_ref, zbuf, xin_hbm, zsem):
            @pl.when(cond)
            def _():
                cp.start()

    @pl.when(i + 1 < n)
    def _():
        for cp in load(i + 1):
            cp.start()

    for cp in load(i):
        cp.wait()
    for slot in range(3):
        @pl.when(i % 3 == slot)
        def _():
            def issue(t, _):
                for k in range(TOP_K):
                    d = idx_smem[slot * (TOP_K * tm) + TOP_K * t + k]
                    pltpu.make_async_copy(hbuf.at[slot, pl.ds(t, 1)], xin_hbm.at[pl.ds(d, 1)],
                                          ssem.at[slot]).start(priority=k % 2)
                return 0

            lax.fori_loop(0, tm, issue, 0)

    @pl.when(i > 0)
    def _():
        wait_scatters(i - 1)

    @pl.when(i == n - 1)
    def _():
        wait_scatters(i)
        for cond, cp in _pad_copies(tab_ref, zbuf, xin_hbm, zsem):
            @pl.when(cond)
            def _():
                cp.wait()


def _dispatch(tab_flat, dest_tiles, h2p, n_rows, tm):
    T, W = h2p.shape
    return pl.pallas_call(
        _dispatch_kernel,
        grid_spec=pltpu.PrefetchScalarGridSpec(
            num_scalar_prefetch=1,
            grid=(T // tm,),
            in_specs=[pl.BlockSpec(memory_space=pl.ANY), pl.BlockSpec(memory_space=pl.ANY)],
            out_specs=pl.BlockSpec(memory_space=pl.ANY),
            scratch_shapes=[
                pltpu.SMEM((3 * TOP_K * tm,), I32),
                pltpu.VMEM((3, tm, W), F32),
                pltpu.VMEM((MOE_ROWS, W), F32),
                pltpu.SemaphoreType.DMA((3,)), pltpu.SemaphoreType.DMA((3,)), pltpu.SemaphoreType.DMA((3,)),
                pltpu.SemaphoreType.DMA,
            ],
        ),
        out_shape=jax.ShapeDtypeStruct((n_rows, W), F32),
        compiler_params=_params("arbitrary"),
        name="moe_dispatch",
    )(tab_flat, dest_tiles, h2p)


def _expert_kernel(be_ref, bs_ref, nv_ref, x_ref, w1_ref, b1_ref, w2_ref, b2_ref, y_ref, w1b, w2b, *, d_ff):
    s = pl.program_id(0)
    live = s < nv_ref[0]
    fresh = (s == 0) | (be_ref[s] != be_ref[jnp.maximum(s - 1, 0)])

    @pl.when(live & fresh)
    def _():
        w1b[...] = w1_ref[0].astype(BF16)
        w2b[...] = w2_ref[0].astype(BF16)

    @pl.when(live)
    def _():
        hc = _dot(x_ref[...].astype(BF16), w1b[...]) + b1_ref[0]
        gt = jnp.minimum(hc[:, :d_ff], SWIGLU_LIMIT)
        up = jnp.clip(hc[:, d_ff:], -SWIGLU_LIMIT, SWIGLU_LIMIT)
        glu = gt * jax.nn.sigmoid(SWIGLU_ALPHA * gt)
        act = ((up + 1.0) * glu).astype(BF16)
        y_ref[...] = _dot(act, w2b[...]) + b2_ref[0]

    @pl.when(jnp.logical_not(live))
    def _():
        y_ref[...] = jnp.zeros_like(y_ref)


def _experts(blk_e, blk_src, n_live, xin, w1, b1, w2, b2):
    n_rows, W = xin.shape
    E, D, two_f = w1.shape
    d_ff = two_f // 2
    return pl.pallas_call(
        functools.partial(_expert_kernel, d_ff=d_ff),
        grid_spec=pltpu.PrefetchScalarGridSpec(
            num_scalar_prefetch=3,
            grid=(n_rows // MOE_ROWS,),
            in_specs=[
                pl.BlockSpec((MOE_ROWS, W), lambda s, be, bs, nv: (bs[s], 0)),
                pl.BlockSpec((1, D, two_f), lambda s, be, bs, nv: (be[s], 0, 0)),
                pl.BlockSpec((1, 1, two_f), lambda s, be, bs, nv: (be[s], 0, 0)),
                pl.BlockSpec((1, d_ff, D), lambda s, be, bs, nv: (be[s], 0, 0)),
                pl.BlockSpec((1, 1, D), lambda s, be, bs, nv: (be[s], 0, 0)),
            ],
            out_specs=pl.BlockSpec((MOE_ROWS, W), lambda s, be, bs, nv: (s, 0)),
            scratch_shapes=[pltpu.VMEM((D, two_f), BF16), pltpu.VMEM((d_ff, D), BF16)],
        ),
        out_shape=jax.ShapeDtypeStruct((n_rows, W), F32),
        compiler_params=_params("arbitrary"),
        name="moe_experts",
    )(blk_e, blk_src, n_live, xin, w1, b1, w2, b2)


def _combine_kernel(dest_hbm, x1_ref, rw_ref, gf_ref, y_hbm, o_ref, idx_smem, ybuf, gsem, isem):
    i = pl.program_id(0)
    n = pl.num_programs(0)
    tm = x1_ref.shape[0]

    def idx_copy(j):
        return pltpu.make_async_copy(dest_hbm.at[j], idx_smem.at[pl.ds((j % 2) * (TOP_K * tm), TOP_K * tm)],
                                     isem.at[j % 2])

    def issue_into(slot):
        def issue(t, _):
            for k in range(TOP_K):
                d = idx_smem[slot * (TOP_K * tm) + TOP_K * t + k]
                pltpu.make_async_copy(y_hbm.at[pl.ds(d, 1)], ybuf.at[slot, k, pl.ds(t, 1)],
                                      gsem.at[slot]).start(priority=k % 2)
            return 0

        lax.fori_loop(0, tm, issue, 0)

    def issue_gathers(j):
        for slot in range(2):
            @pl.when(j % 2 == slot)
            def _():
                issue_into(slot)

    @pl.when(i == 0)
    def _():
        idx_copy(0).start()
        idx_copy(0).wait()
        issue_into(0)

        @pl.when(n > 1)
        def _():
            idx_copy(1).start()

    @pl.when(i + 1 < n)
    def _():
        idx_copy(i + 1).wait()
        issue_gathers(i + 1)

        @pl.when(i + 2 < n)
        def _():
            idx_copy(i + 2).start()

    slot = i % 2
    for k in range(TOP_K):
        pltpu.make_async_copy(y_hbm.at[pl.ds(0, tm)], ybuf.at[slot, k], gsem.at[slot]).wait()

    rw = rw_ref[...]
    x2 = x1_ref[...]
    for k in range(TOP_K):
        x2 = x2 + rw[:, k:k + 1] * ybuf[slot, k]
    o_ref[...] = _rms(x2, gf_ref[...])


def _combine(dest_tiles, x1, rw, gf, y, tm):
    T, D = x1.shape
    W = y.shape[1]
    return pl.pallas_call(
        _combine_kernel,
        grid=(T // tm,),
        in_specs=[
            pl.BlockSpec(memory_space=pl.ANY),
            pl.BlockSpec((tm, D), lambda i: (i, 0)),
            pl.BlockSpec((tm, LANES), lambda i: (i, 0)),
            pl.BlockSpec((1, D), lambda i: (0, 0)),
            pl.BlockSpec(memory_space=pl.ANY),
        ],
        out_specs=pl.BlockSpec((tm, D), lambda i: (i, 0)),
        out_shape=jax.ShapeDtypeStruct((T, D), F32),
        scratch_shapes=[
            pltpu.SMEM((2 * TOP_K * tm,), I32),
            pltpu.VMEM((2, TOP_K, tm, W), F32),
            pltpu.SemaphoreType.DMA((2,)), pltpu.SemaphoreType.DMA((2,)),
        ],
        compiler_params=_params("arbitrary"),
        name="moe_combine",
    )(dest_tiles, x1, rw, gf, y)


def _rope_tables(positions):
    half = ROT_DIM // 2
    freqs = ROPE_THETA ** (-jnp.arange(0, ROT_DIM, 2, dtype=F32) / ROT_DIM)
    ang = positions.astype(F32)[..., None] * freqs
    cos, sin = jnp.cos(ang), jnp.sin(ang)
    d = np.arange(LANES) % HEAD_DIM
    f_idx = d % half
    in_lo = jnp.asarray(d < half)
    in_hi = jnp.asarray((d >= half) & (d < ROT_DIM))
    cos_l, sin_l = cos[..., f_idx], sin[..., f_idx]
    c = jnp.where(in_lo | in_hi, cos_l, 1.0)
    s1 = jnp.where(in_lo, -sin_l, 0.0)
    s2 = jnp.where(in_hi, sin_l, 0.0)
    return c, s1, s2


def _arrange_w_in(w_in, d_model):
    nq, nkv, ng = NSA_HEADS * HEAD_DIM, NSA_KV_GROUPS * HEAD_DIM, NSA_HEADS * 3
    fw = FOX_HEADS * HEAD_DIM
    sizes = [nq] + [nkv] * 6 + [ng, fw, fw, fw, FOX_HEADS, d_model, d_model]
    offs = np.concatenate([[0], np.cumsum(sizes)])
    piece = lambda k: w_in[:, offs[k]:offs[k + 1]]
    gates, ff = piece(7), piece(11)
    per_group = HEADS_PER_GROUP * 3
    zeros = lambda n: jnp.zeros((w_in.shape[0], n), w_in.dtype)
    misc0 = jnp.concatenate([gates[:, :per_group], zeros(FGATE_LANE - per_group), ff,
                             zeros(LANES - FGATE_LANE - FOX_HEADS)], axis=1)
    misc1 = jnp.concatenate([gates[:, per_group:], zeros(LANES - per_group)], axis=1)
    cols = [piece(k) for k in range(7)] + [piece(8), piece(9), piece(10), piece(12), piece(13), misc0, misc1]
    return jnp.concatenate(cols, axis=1).astype(BF16)


def _split_bf16(a):
    hi = a.astype(BF16)
    return hi, (a - hi.astype(F32)).astype(BF16)


def kernel(x, positions, norm1_g, w_in, cmp_pos_emb, cmp_w1, cmp_w2, fox_f_bias, w_proj_nsa, w_proj_fox, w_out,
           norm2_g, router_w, router_b, expert_w1, expert_b1, expert_w2, expert_b2, norm_f_g):
    B, S, D = x.shape
    T = B * S
    assert norm1_g.shape[0] == 1, "the combine kernel fuses the output norm, so it must follow the only layer"
    assert S % KV_CHUNK == 0 or S < KV_CHUNK
    assert S // SEL_BLOCK <= LANES - HEAD_DIM and S % (CMP_STRIDE * SUBLANES) == 0
    tm_proj = min(512, S)
    assert WINDOW % tm_proj == 0 or tm_proj % WINDOW == 0
    tm_tok = min(512, T)
    tm_comb = min(256, T)
    l = 0

    rope_c, rope_s1, rope_s2 = _rope_tables(positions)
    n_sb = S // SEL_BLOCK
    n_rows16 = S // CMP_STRIDE
    ci = np.arange(n_rows16)[None, :] * CMP_STRIDE
    sj = np.arange(LANES)[:, None] * SEL_BLOCK
    overlap = jnp.asarray(((ci < sj + SEL_BLOCK) & (ci + CMP_BLOCK > sj) & (np.arange(LANES)[:, None] < n_sb)
                           & (np.arange(n_rows16)[None, :] < n_rows16 - 1)).astype(np.float32), BF16)

    w_all = _arrange_w_in(w_in[l], D)
    fb_row = jnp.zeros((1, LANES), F32).at[0, FGATE_LANE:FGATE_LANE + FOX_HEADS].set(fox_f_bias[l].astype(F32))
    (qraw, qrot, kvc, ks, vs, kw, vw, fq, fk, fv, sgn, sgf, misc) = _inproj(
        x, norm1_g[l][None, :], w_all, rope_c, rope_s1, rope_s2, fb_row, tm_proj)
    qrot = qrot.reshape(B, NSA_HEADS, S, LANES)
    fq = fq.reshape(B, FOX_HEADS, S, LANES)
    fv = fv.reshape(B, FOX_HEADS, S, LANES)

    kv_rows = kvc.reshape(B, 2, NSA_KV_GROUPS, n_rows16, CMP_STRIDE * HEAD_DIM)
    pe = cmp_pos_emb[l].reshape(2, 1, CMP_BLOCK * HEAD_DIM).astype(F32)
    cw1 = jnp.pad(cmp_w1[l], ((0, 0), (0, 0), (0, LANES - HEAD_DIM))).astype(BF16)
    cw2 = jnp.pad(cmp_w2[l], ((0, 0), (0, LANES - HEAD_DIM), (0, 0))).astype(BF16)
    cmp_kv = _compress(kv_rows, pe, cw1, cw2)

    front_k = jnp.zeros((B, NSA_KV_GROUPS, WINDOW, LANES), BF16).at[..., HEAD_DIM].set(NEG)
    kw_pad = jnp.concatenate([front_k, kw], axis=2)
    vw_pad = jnp.concatenate([jnp.zeros((B, NSA_KV_GROUPS, WINDOW, LANES), BF16), vw], axis=2)
    o_nsa = _nsa(qraw, qrot, cmp_kv, ks, vs, kw_pad, vw_pad, misc, overlap)
    o_fox = _fox(fq, fk, fv)

    wr = jnp.pad(router_w[l], ((0, 0), (0, LANES - N_EXPERTS)))
    wr_hi, wr_lo = _split_bf16(wr)
    br = jnp.pad(router_b[l], (0, LANES - N_EXPERTS))[None, :].astype(F32)
    x1, h2, ri, rw, cnt = _merge(
        x.reshape(T, D), o_nsa.reshape(T, -1), o_fox.reshape(T, -1), sgn.reshape(T, D), sgf.reshape(T, D),
        w_proj_nsa[l].astype(BF16), w_proj_fox[l].astype(BF16), w_out[l].astype(BF16),
        norm2_g[l][None, :], wr_hi, wr_lo, br, tm_tok)

    dest, tab = _dest(ri, cnt, tm_tok)
    dest4 = dest[:, :TOP_K]
    pend = tab[3, :N_EXPERTS]
    n_blocks = (T * TOP_K) // MOE_ROWS + N_EXPERTS
    n_live = (pend[N_EXPERTS - 1] // MOE_ROWS).astype(I32)
    blk_src = jnp.minimum(jnp.arange(n_blocks, dtype=I32), n_live - 1)
    blk_e = jnp.sum((pend[None, :] <= (blk_src * MOE_ROWS)[:, None]).astype(I32), axis=1)
    blk_e = jnp.minimum(blk_e, N_EXPERTS - 1)

    xin = _dispatch(tab[:3].reshape(-1), dest4.reshape(T // tm_tok, TOP_K * tm_tok), h2,
                    n_blocks * MOE_ROWS, tm_tok)
    y = _experts(blk_e, blk_src, n_live[None], xin, expert_w1[l], expert_b1[l][:, None, :],
                 expert_w2[l], expert_b2[l][:, None, :])
    out = _combine(dest4.reshape(T // tm_comb, TOP_K * tm_comb), x1, rw, norm_f_g[None, :], y, tm_comb)
    return out.reshape(B, S, D)
```

```python
import functools

import numpy as np
import jax
import jax.numpy as jnp
from jax import lax
from jax.experimental import pallas as pl
from jax.experimental.pallas import tpu as pltpu

HEAD_DIM = 64
NSA_HEADS = 8
NSA_KV_GROUPS = 2
HEADS_PER_GROUP = NSA_HEADS // NSA_KV_GROUPS
FOX_HEADS = 8
ROT_DIM = HEAD_DIM // 4
ROPE_THETA = 500000.0
CMP_BLOCK = 32
CMP_STRIDE = 16
SEL_BLOCK = 64
N_SEL = 8
WINDOW = 512
N_EXPERTS = 32
TOP_K = 4
SWIGLU_ALPHA = 1.702
SWIGLU_LIMIT = 7.0
NORM_EPS = 1e-6
NEG = -1e30
BELOW_NEG = -3e38
FORCE_BONUS = 1e4

LANES = 128
SUBLANES = 8
NSA_Q_TILE = 256
KV_CHUNK = 512
FOX_TILE = 512
MOE_ROWS = 512
FGATE_LANE = 32
VMEM_LIMIT = 56 * 1024 * 1024

F32 = jnp.float32
BF16 = jnp.bfloat16
I32 = jnp.int32


def _dot(a, b):
    return jnp.dot(a, b, preferred_element_type=F32)


def _dot_nt(a, b):
    return lax.dot_general(a, b, (((1,), (1,)), ((), ())), preferred_element_type=F32)


def _params(*sem):
    return pltpu.CompilerParams(dimension_semantics=sem, vmem_limit_bytes=VMEM_LIMIT)


def _rms(x, g):
    return x * lax.rsqrt(jnp.mean(x * x, axis=-1, keepdims=True) + NORM_EPS) * g


def _split3(a):
    t1 = a.astype(BF16)
    r1 = a - t1.astype(F32)
    t2 = r1.astype(BF16)
    t3 = (r1 - t2.astype(F32)).astype(BF16)
    return t1, t2, t3


def _inproj_kernel(x_ref, g_ref, w_ref, rc_ref, rs1_ref, rs2_ref, fb_ref,
                   qraw_ref, qrot_ref, kvc_ref, ks_ref, vs_ref, kw_ref, vw_ref,
                   fq_ref, fk_ref, fv_ref, sgn_ref, sgf_ref, misc_ref, csum_scr, *, d_model):
    scale = HEAD_DIM ** -0.5
    i = pl.program_id(1)
    tm = x_ref.shape[1]
    xn = _rms(x_ref[0], g_ref[...]).astype(BF16)
    rc, rs1, rs2 = rc_ref[0], rs1_ref[0], rs2_ref[0]
    lane = lax.broadcasted_iota(I32, (tm, LANES), 1)
    low = lane < HEAD_DIM
    one_at_64 = jnp.where(lane == HEAD_DIM, 1.0, 0.0)

    def rope(slab):
        half = ROT_DIM // 2
        return slab * rc + pltpu.roll(slab, LANES - half, 1) * rs1 + pltpu.roll(slab, half, 1) * rs2

    def put_pair(ref, lead, slab, tail):
        ref[lead + (0,)] = jnp.where(low, slab, tail).astype(BF16)
        ref[lead + (1,)] = jnp.where(low, pltpu.roll(slab, HEAD_DIM, 1), tail).astype(BF16)

    misc_off = w_ref.shape[1] - 2 * LANES
    r = _dot(xn, w_ref[:, misc_off:])
    m0 = r[:, :LANES]
    z = m0 + fb_ref[...]
    log_f = jnp.minimum(z, 0.0) - jnp.log(1.0 + jnp.exp(-jnp.abs(z)))
    misc_ref[0, 0] = jax.nn.sigmoid(m0)
    misc_ref[0, 1] = jax.nn.sigmoid(r[:, LANES:])

    @pl.when(i == 0)
    def _():
        csum_scr[...] = jnp.zeros_like(csum_scr)

    in_gate = (lane >= FGATE_LANE) & (lane < FGATE_LANE + FOX_HEADS)
    t1, t2, t3 = _split3(jnp.where(in_gate, log_f, 0.0))
    r_i = lax.broadcasted_iota(I32, (tm, tm), 0)
    c_i = lax.broadcasted_iota(I32, (tm, tm), 1)
    upto = jnp.where(c_i <= r_i, 1.0, 0.0).astype(BF16)
    csum = _dot(upto, t1) + _dot(upto, t2) + _dot(upto, t3) + csum_scr[0:1, :]
    csum_scr[...] = jnp.broadcast_to(csum[tm - 1:tm, :], csum_scr.shape)

    nq = NSA_HEADS * HEAD_DIM
    r = _dot(xn, w_ref[:, 0:nq]) * scale
    for s in range(nq // LANES):
        slab = r[:, LANES * s:LANES * (s + 1)]
        qraw_ref[0, 2 * s] = slab[:, :HEAD_DIM].astype(BF16)
        qraw_ref[0, 2 * s + 1] = slab[:, HEAD_DIM:].astype(BF16)
        put_pair(qrot_ref, (0, s), rope(slab), 0.0)
    off = nq

    r = _dot(xn, w_ref[:, off:off + 6 * LANES])
    for t in range(2):
        slab = r[:, t * LANES:(t + 1) * LANES]
        kvc_ref[0, t, 0] = slab[:, :HEAD_DIM].astype(BF16)
        kvc_ref[0, t, 1] = slab[:, HEAD_DIM:].astype(BF16)
    tok = i * tm + lax.broadcasted_iota(I32, (tm, LANES), 0)
    block_hot = jnp.where(lane - HEAD_DIM == (tok >> (SEL_BLOCK.bit_length() - 1)), 1.0, 0.0)
    put_pair(ks_ref, (0,), rope(r[:, 2 * LANES:3 * LANES]), block_hot)
    put_pair(vs_ref, (0,), r[:, 3 * LANES:4 * LANES], one_at_64)
    put_pair(kw_ref, (0,), rope(r[:, 4 * LANES:5 * LANES]), 0.0)
    put_pair(vw_ref, (0,), r[:, 5 * LANES:6 * LANES], one_at_64)
    off += 6 * LANES

    fw = FOX_HEADS * HEAD_DIM
    ones3 = jnp.where((lane >= HEAD_DIM) & (lane < HEAD_DIM + 3), 1.0, 0.0)
    r = _dot(xn, w_ref[:, off:off + fw]) * scale
    for s in range(fw // LANES):
        put_pair(fq_ref, (0, s), r[:, LANES * s:LANES * (s + 1)], ones3)
    off += fw
    r = _dot(xn, w_ref[:, off:off + fw])
    for s in range(fw // LANES):
        slab = r[:, LANES * s:LANES * (s + 1)]
        for par in range(2):
            h = 2 * s + par
            neg_c = -csum[:, FGATE_LANE + h:FGATE_LANE + h + 1]
            c1, c2, c3 = _split3(neg_c)
            tail = jnp.where(lane == HEAD_DIM, c1.astype(F32), jnp.where(
                lane == HEAD_DIM + 1, c2.astype(F32), jnp.where(lane == HEAD_DIM + 2, c3.astype(F32), 0.0)))
            head = slab if par == 0 else pltpu.roll(slab, HEAD_DIM, 1)
            fk_ref[0, h] = jnp.where(low, head, tail).astype(BF16)
    off += fw
    r = _dot(xn, w_ref[:, off:off + fw])
    for s in range(fw // LANES):
        put_pair(fv_ref, (0, s), r[:, LANES * s:LANES * (s + 1)], one_at_64)
    off += fw

    step = min(512, d_model)
    for ref in (sgn_ref, sgf_ref):
        for c in range(0, d_model, step):
            r = _dot(xn, w_ref[:, off + c:off + c + step])
            ref[0, :, c:c + step] = jax.nn.sigmoid(r).astype(BF16)
        off += d_model


def _inproj(x, norm_g, w_all, rope_c, rope_s1, rope_s2, f_bias_row, tm):
    B, S, D = x.shape
    ncol = w_all.shape[1]
    G = NSA_KV_GROUPS
    wide = lambda heads: jax.ShapeDtypeStruct((B, heads, S, LANES), BF16)
    pair_shape = lambda heads: jax.ShapeDtypeStruct((B, heads // 2, 2, S, LANES), BF16)
    pair_spec = lambda heads: pl.BlockSpec((1, heads // 2, 2, tm, LANES), lambda b, i: (b, 0, 0, i, 0))
    kvspec = pl.BlockSpec((1, G, tm, LANES), lambda b, i: (b, 0, i, 0))
    tabspec = pl.BlockSpec((1, tm, LANES), lambda b, i: (b, i, 0))
    gspec = pl.BlockSpec((1, tm, D), lambda b, i: (b, i, 0))
    return pl.pallas_call(
        functools.partial(_inproj_kernel, d_model=D),
        grid=(B, S // tm),
        in_specs=[
            pl.BlockSpec((1, tm, D), lambda b, i: (b, i, 0)),
            pl.BlockSpec((1, D), lambda b, i: (0, 0)),
            pl.BlockSpec((D, ncol), lambda b, i: (0, 0)),
            tabspec, tabspec, tabspec,
            pl.BlockSpec((1, LANES), lambda b, i: (0, 0)),
        ],
        out_specs=[
            pl.BlockSpec((1, NSA_HEADS, tm, HEAD_DIM), lambda b, i: (b, 0, i, 0)),
            pair_spec(NSA_HEADS),
            pl.BlockSpec((1, 2, G, tm, HEAD_DIM), lambda b, i: (b, 0, 0, i, 0)),
            kvspec, kvspec, kvspec, kvspec,
            pair_spec(FOX_HEADS),
            pl.BlockSpec((1, FOX_HEADS, tm, LANES), lambda b, i: (b, 0, i, 0)),
            pair_spec(FOX_HEADS),
            gspec, gspec,
            pl.BlockSpec((1, 2, tm, LANES), lambda b, i: (b, 0, i, 0)),
        ],
        out_shape=[
            jax.ShapeDtypeStruct((B, NSA_HEADS, S, HEAD_DIM), BF16),
            pair_shape(NSA_HEADS),
            jax.ShapeDtypeStruct((B, 2, G, S, HEAD_DIM), BF16),
            wide(G), wide(G), wide(G), wide(G),
            pair_shape(FOX_HEADS), wide(FOX_HEADS), pair_shape(FOX_HEADS),
            jax.ShapeDtypeStruct((B, S, D), BF16), jax.ShapeDtypeStruct((B, S, D), BF16),
            jax.ShapeDtypeStruct((B, 2, S, LANES), F32),
        ],
        scratch_shapes=[pltpu.VMEM((SUBLANES, LANES), F32)],
        compiler_params=_params("parallel", "arbitrary"),
        name="inproj",
    )(x, norm_g, w_all, rope_c, rope_s1, rope_s2, f_bias_row)


def _compress_kernel(r_ref, pe_ref, w1_ref, w2_ref, o_ref):
    half = CMP_STRIDE * HEAD_DIM
    rows = r_ref[0, 0, 0].astype(F32)
    lo = (rows + pe_ref[0, :, :half]).astype(BF16)
    hi = (rows + pe_ref[0, :, half:]).astype(BF16)
    a = _dot(lo, w1_ref[0, :half])
    b = _dot(hi, w1_ref[0, half:])
    n = a.shape[0]
    pre = a + pltpu.roll(b, n - 1, 0)
    act = jax.nn.gelu(pre, approximate=True).astype(BF16)
    o_ref[0, 0, 0] = _dot(act, w2_ref[0]).astype(BF16)


def _compress(kv_rows, pe, w1, w2):
    B, _, G, n_rows, width = kv_rows.shape
    return pl.pallas_call(
        _compress_kernel,
        grid=(B, 2, G),
        in_specs=[
            pl.BlockSpec((1, 1, 1, n_rows, width), lambda b, t, g: (b, t, g, 0, 0)),
            pl.BlockSpec((1, 1, 2 * width), lambda b, t, g: (t, 0, 0)),
            pl.BlockSpec((1, 2 * width, LANES), lambda b, t, g: (t, 0, 0)),
            pl.BlockSpec((1, LANES, HEAD_DIM), lambda b, t, g: (t, 0, 0)),
        ],
        out_specs=pl.BlockSpec((1, 1, 1, n_rows, HEAD_DIM), lambda b, t, g: (b, t, g, 0, 0)),
        out_shape=jax.ShapeDtypeStruct((B, 2, G, n_rows, HEAD_DIM), BF16),
        compiler_params=_params("parallel", "parallel", "parallel"),
        name="nsa_compress",
    )(kv_rows, pe, w1, w2)


def _softmax_rows(logits, ok):
    l = jnp.where(ok, logits, NEG)
    m = jnp.max(l, axis=-1, keepdims=True)
    e = jnp.where(ok, jnp.exp(l - m), 0.0)
    s = jnp.sum(e, axis=-1, keepdims=True)
    return e / jnp.where(s > 0.0, s, 1.0)


def _nsa_kernel(qraw_ref, qrot_ref, kcmp_ref, vcmp_ref, ks_ref, vs_ref, kw_ref, vw_ref,
                misc_ref, overlap_ref, o_ref, *, seq, chunk):
    qi = pl.program_id(2)
    n_full = qi * NSA_Q_TILE // chunk
    for last in range(seq // chunk):
        @pl.when(n_full == last)
        def _():
            _nsa_tile(qraw_ref, qrot_ref, kcmp_ref, vcmp_ref, ks_ref, vs_ref, kw_ref, vw_ref,
                      misc_ref, overlap_ref, o_ref, qi, last, seq=seq, chunk=chunk)


def _nsa_tile(qraw_ref, qrot_ref, kcmp_ref, vcmp_ref, ks_ref, vs_ref, kw_ref, vw_ref,
              misc_ref, overlap_ref, o_ref, qi, last, *, seq, chunk):
    tq_n = NSA_Q_TILE
    hpg = HEADS_PER_GROUP
    rows_n = hpg * tq_n
    n_sb = seq // SEL_BLOCK
    n_sel = min(N_SEL, n_sb)
    t0 = qi * tq_n
    tq = t0 + lax.broadcasted_iota(I32, (tq_n, 1), 0)
    qrot = qrot_ref[0].reshape(rows_n, LANES).astype(F32)
    lane_r = lax.broadcasted_iota(I32, (rows_n, LANES), 1)
    row_in_tile = lax.broadcasted_iota(I32, (rows_n, 1), 0) & (tq_n - 1)

    q_win = jnp.where(lane_r == HEAD_DIM, 1.0, qrot).astype(BF16)
    wlen = WINDOW + tq_n
    w0 = pl.multiple_of(t0, tq_n)
    kw = kw_ref[0, 0, pl.ds(w0, wlen), :]
    vw = vw_ref[0, 0, pl.ds(w0, wlen), :]
    sw = _dot_nt(q_win, kw)
    col = lax.broadcasted_iota(I32, (rows_n, tq_n), 1)
    s_old = jnp.where(col > row_in_tile, sw[:, :tq_n], NEG)
    s_new = jnp.where(col <= row_in_tile, sw[:, WINDOW:], NEG)
    sw = jnp.concatenate([s_old, sw[:, tq_n:WINDOW], s_new], axis=1)
    mw = jnp.broadcast_to(jnp.max(sw, axis=-1, keepdims=True), (rows_n, LANES))
    acc_w = _dot(jnp.exp(sw - jnp.tile(mw, (1, wlen // LANES))).astype(BF16), vw)

    q = qraw_ref[0].reshape(rows_n, HEAD_DIM)
    n_c = kcmp_ref.shape[3]
    lg = _dot_nt(q, kcmp_ref[0, 0, 0])
    cidx = lax.broadcasted_iota(I32, (tq_n, n_c), 1)
    cmask = cidx * CMP_STRIDE + (CMP_BLOCK - 1) <= tq
    vcmp = vcmp_ref[0, 0, 0]
    o_c = []
    p_sum = jnp.zeros((tq_n, n_c), F32)
    for h in range(hpg):
        p = _softmax_rows(lg[h * tq_n:(h + 1) * tq_n], cmask)
        p_sum = p_sum + p
        o_c.append(_dot(p.astype(BF16), vcmp))

    p_hi = p_sum.astype(BF16)
    p_lo = (p_sum - p_hi.astype(F32)).astype(BF16)
    imp = (_dot_nt(overlap_ref[...], p_hi) + _dot_nt(overlap_ref[...], p_lo))[:n_sb]
    j = lax.broadcasted_iota(I32, (n_sb, tq_n), 0)
    t_row = t0 + lax.broadcasted_iota(I32, (n_sb, tq_n), 1)
    cur = t_row >> (SEL_BLOCK.bit_length() - 1)
    forced = jnp.where(j == 0, 1.0, jnp.where(j == cur, 1.0, jnp.where(j == cur - 1, 1.0, 0.0)))
    score = jnp.where(forced > 0.0, imp + FORCE_BONUS, jnp.where(j * SEL_BLOCK <= t_row, imp, NEG))
    beaten = jnp.zeros((n_sb, tq_n), F32)
    for jp in range(n_sb):
        row = score[jp:jp + 1, :]
        wins_ties = jnp.where(row >= score, 1.0, 0.0)
        wins = jnp.where(row > score, 1.0, 0.0)
        beaten = beaten + jnp.where(j > jp, wins_ties, wins)
    penalty = jnp.where(beaten < n_sel, 0.0, NEG)
    penalty = jnp.concatenate([penalty, jnp.zeros((LANES - n_sb, tq_n), F32)], axis=0).T
    penalty = pltpu.roll(penalty, HEAD_DIM, 1)

    q_sel = jnp.where(lane_r < HEAD_DIM, qrot, jnp.concatenate([penalty] * hpg, axis=0)).astype(BF16)

    logits = [_dot_nt(q_sel, ks_ref[0, 0, c * chunk:(c + 1) * chunk, :]) for c in range(last + 1)]
    kpos = last * chunk + lax.broadcasted_iota(I32, (rows_n, chunk), 1)
    logits[last] = jnp.where(kpos <= t0 + row_in_tile, logits[last], NEG)
    m = jnp.max(logits[0], axis=-1, keepdims=True)
    for s in logits[1:]:
        m = jnp.maximum(m, jnp.max(s, axis=-1, keepdims=True))
    m = jnp.tile(jnp.broadcast_to(m, (rows_n, LANES)), (1, chunk // LANES))
    acc_s = _dot(jnp.exp(logits[0] - m).astype(BF16), vs_ref[0, 0, 0:chunk, :])
    for c in range(1, last + 1):
        acc_s = acc_s + _dot(jnp.exp(logits[c] - m).astype(BF16), vs_ref[0, 0, c * chunk:(c + 1) * chunk, :])

    g = misc_ref[0, 0]
    for h in range(hpg):
        a_s = acc_s[h * tq_n:(h + 1) * tq_n]
        a_w = acc_w[h * tq_n:(h + 1) * tq_n]
        o_s = a_s[:, :HEAD_DIM] / a_s[:, HEAD_DIM:HEAD_DIM + 1]
        o_w = a_w[:, :HEAD_DIM] / a_w[:, HEAD_DIM:HEAD_DIM + 1]
        o_h = g[:, 3 * h:3 * h + 1] * o_c[h] + g[:, 3 * h + 1:3 * h + 2] * o_s + g[:, 3 * h + 2:3 * h + 3] * o_w
        o_ref[0, :, HEAD_DIM * h:HEAD_DIM * (h + 1)] = o_h.astype(BF16)


def _nsa(qraw, qrot, cmp_kv, ks, vs, kw_pad, vw_pad, misc, overlap):
    B, _, S, _ = qraw.shape
    G = NSA_KV_GROUPS
    n_c = cmp_kv.shape[3]
    chunk = min(KV_CHUNK, S)
    kvspec = pl.BlockSpec((1, 1, S, LANES), lambda b, g, i: (b, g, 0, 0))
    padspec = pl.BlockSpec((1, 1, S + WINDOW, LANES), lambda b, g, i: (b, g, 0, 0))
    return pl.pallas_call(
        functools.partial(_nsa_kernel, seq=S, chunk=chunk),
        grid=(B, G, S // NSA_Q_TILE),
        in_specs=[
            pl.BlockSpec((1, HEADS_PER_GROUP, NSA_Q_TILE, HEAD_DIM), lambda b, g, i: (b, g, i, 0)),
            pl.BlockSpec((1, HEADS_PER_GROUP, NSA_Q_TILE, LANES), lambda b, g, i: (b, g, i, 0)),
            pl.BlockSpec((1, 1, 1, n_c, HEAD_DIM), lambda b, g, i: (b, 0, g, 0, 0)),
            pl.BlockSpec((1, 1, 1, n_c, HEAD_DIM), lambda b, g, i: (b, 1, g, 0, 0)),
            kvspec, kvspec, padspec, padspec,
            pl.BlockSpec((1, 1, NSA_Q_TILE, LANES), lambda b, g, i: (b, g, i, 0)),
            pl.BlockSpec((LANES, n_c), lambda b, g, i: (0, 0)),
        ],
        out_specs=pl.BlockSpec((1, NSA_Q_TILE, HEADS_PER_GROUP * HEAD_DIM), lambda b, g, i: (b, i, g)),
        out_shape=jax.ShapeDtypeStruct((B, S, NSA_HEADS * HEAD_DIM), BF16),
        compiler_params=_params("parallel", "parallel", "arbitrary"),
        name="nsa_attention",
    )(qraw, qrot, cmp_kv, cmp_kv, ks, vs, kw_pad, vw_pad, misc, overlap)


def _fox_kernel(q_ref, k_ref, v_ref, o_ref):
    tile = q_ref.shape[2]
    qi = pl.program_id(1)
    for last in range(k_ref.shape[2] // tile):
        @pl.when(qi == last)
        def _():
            _fox_tile(q_ref, k_ref, v_ref, o_ref, last, tile)


def _fox_tile(q_ref, k_ref, v_ref, o_ref, last, tile):
    causal = (lax.broadcasted_iota(I32, (tile, tile), 1) <= lax.broadcasted_iota(I32, (tile, tile), 0))
    for h in range(FOX_HEADS):
        q = q_ref[0, h]
        logits = [_dot_nt(q, k_ref[0, h, c * tile:(c + 1) * tile, :]) for c in range(last + 1)]
        logits[last] = jnp.where(causal, logits[last], NEG)
        m = jnp.max(logits[0], axis=-1, keepdims=True)
        for s in logits[1:]:
            m = jnp.maximum(m, jnp.max(s, axis=-1, keepdims=True))
        m = jnp.tile(jnp.broadcast_to(m, (tile, LANES)), (1, tile // LANES))
        acc = _dot(jnp.exp(logits[0] - m).astype(BF16), v_ref[0, h, 0:tile, :])
        for c in range(1, last + 1):
            acc = acc + _dot(jnp.exp(logits[c] - m).astype(BF16), v_ref[0, h, c * tile:(c + 1) * tile, :])
        o_ref[0, :, HEAD_DIM * h:HEAD_DIM * (h + 1)] = (acc[:, :HEAD_DIM] / acc[:, HEAD_DIM:HEAD_DIM + 1]).astype(BF16)


def _fox(fq, fk, fv):
    B, H, S, _ = fq.shape
    tile = min(FOX_TILE, S)
    return pl.pallas_call(
        _fox_kernel,
        grid=(B, S // tile),
        in_specs=[
            pl.BlockSpec((1, H, tile, LANES), lambda b, i: (b, 0, i, 0)),
            pl.BlockSpec((1, H, S, LANES), lambda b, i: (b, 0, 0, 0)),
            pl.BlockSpec((1, H, S, LANES), lambda b, i: (b, 0, 0, 0)),
        ],
        out_specs=pl.BlockSpec((1, tile, H * HEAD_DIM), lambda b, i: (b, i, 0)),
        out_shape=jax.ShapeDtypeStruct((B, S, H * HEAD_DIM), BF16),
        compiler_params=_params("parallel", "arbitrary"),
        name="fox_attention",
    )(fq, fk, fv)


def _merge_kernel(x_ref, on_ref, of_ref, sgn_ref, sgf_ref, wn_ref, wf_ref, wo_ref, g2_ref,
                  wrh_ref, wrl_ref, br_ref, x1_ref, h2_ref, ri_ref, rw_ref, cnt_ref, carry_scr):
    i = pl.program_id(0)
    tm = x_ref.shape[0]

    @pl.when(i == 0)
    def _():
        carry_scr[...] = jnp.zeros_like(carry_scr)

    a = _dot(on_ref[...], wn_ref[...])
    b = _dot(of_ref[...], wf_ref[...])
    mixed = (sgn_ref[...].astype(F32) * a + sgf_ref[...].astype(F32) * b).astype(BF16)
    x1 = x_ref[...] + _dot(mixed, wo_ref[...])
    x1_ref[...] = x1
    h2 = _rms(x1, g2_ref[...])
    h2_ref[...] = h2

    hh = h2.astype(BF16)
    hl = (h2 - hh.astype(F32)).astype(BF16)
    logits = _dot(hh, wrh_ref[...]) + _dot(hl, wrh_ref[...]) + _dot(hh, wrl_ref[...]) + br_ref[...]
    lane = lax.broadcasted_iota(I32, (tm, LANES), 1)
    lane_f = lane.astype(F32)
    l = jnp.where(lane < N_EXPERTS, logits, BELOW_NEG)
    idxs, vals, hots = [], [], []
    for _ in range(TOP_K):
        m = jnp.max(l, axis=-1, keepdims=True)
        idx = jnp.min(jnp.where(l == m, lane_f, float(LANES)), axis=-1, keepdims=True)
        hot = lane_f == idx
        idxs.append(idx)
        vals.append(m)
        hots.append(hot)
        l = jnp.where(hot, BELOW_NEG, l)
    exps = [jnp.exp(v - vals[0]) for v in vals]
    den = exps[0]
    for e in exps[1:]:
        den = den + e

    chosen = jnp.zeros((tm, LANES), F32)
    for hot in hots:
        chosen = chosen + jnp.where(hot, 1.0, 0.0)
    r_i = lax.broadcasted_iota(I32, (tm, tm), 0)
    c_i = lax.broadcasted_iota(I32, (tm, tm), 1)
    earlier = jnp.where(c_i < r_i, 1.0, 0.0).astype(BF16)
    before = _dot(earlier, chosen.astype(BF16)) + carry_scr[0:1, :]
    carry_scr[...] = carry_scr[...] + jnp.sum(chosen, axis=0, keepdims=True)
    cnt_ref[...] = carry_scr[...]

    ri = jnp.zeros((tm, LANES), F32)
    rw = jnp.zeros((tm, LANES), F32)
    for k in range(TOP_K):
        rank = jnp.sum(jnp.where(hots[k], before, 0.0), axis=-1, keepdims=True)
        ri = jnp.where(lane == k, idxs[k], ri)
        ri = jnp.where(lane == TOP_K + k, rank, ri)
        rw = jnp.where(lane == k, exps[k] / den, rw)
    ri_ref[...] = ri.astype(I32)
    rw_ref[...] = rw


def _merge(x2d, o_nsa, o_fox, sgn, sgf, wn, wf, wo, g2, wr_hi, wr_lo, br, tm):
    T, D = x2d.shape
    wq = o_nsa.shape[1]
    row = lambda w: pl.BlockSpec((tm, w), lambda i: (i, 0))
    full = lambda a: pl.BlockSpec(a.shape, lambda i: (0,) * a.ndim)
    return pl.pallas_call(
        _merge_kernel,
        grid=(T // tm,),
        in_specs=[row(D), row(wq), row(wq), row(D), row(D), full(wn), full(wf), full(wo), full(g2),
                  full(wr_hi), full(wr_lo), full(br)],
        out_specs=[row(D), row(D), row(LANES), row(LANES), pl.BlockSpec((SUBLANES, LANES), lambda i: (0, 0))],
        out_shape=[
            jax.ShapeDtypeStruct((T, D), F32), jax.ShapeDtypeStruct((T, D), F32),
            jax.ShapeDtypeStruct((T, LANES), I32), jax.ShapeDtypeStruct((T, LANES), F32),
            jax.ShapeDtypeStruct((SUBLANES, LANES), F32),
        ],
        scratch_shapes=[pltpu.VMEM((SUBLANES, LANES), F32)],
        compiler_params=_params("arbitrary"),
        name="merge_router",
    )(x2d, o_nsa, o_fox, sgn, sgf, wn, wf, wo, g2, wr_hi, wr_lo, br)


def _dest_kernel(ri_ref, cnt_ref, dest_ref, tab_ref):
    tm = ri_ref.shape[0]
    shift = MOE_ROWS.bit_length() - 1
    cnt = cnt_ref[...].astype(I32)
    padded = ((cnt + (MOE_ROWS - 1)) >> shift) << shift
    lane8 = lax.broadcasted_iota(I32, (SUBLANES, LANES), 1)
    pend = padded
    sh = 1
    while sh < LANES:
        pend = pend + jnp.where(lane8 >= sh, pltpu.roll(pend, sh, 1), 0)
        sh *= 2
    pstart = pend - padded
    row8 = lax.broadcasted_iota(I32, (SUBLANES, LANES), 0)
    tab_ref[...] = jnp.where(row8 == 0, cnt, jnp.where(row8 == 1, padded, jnp.where(row8 == 2, pstart, pend)))

    ri = ri_ref[...]
    lane = lax.broadcasted_iota(I32, (tm, LANES), 1)
    start_row = pstart[0:1, :].astype(F32)
    dest = jnp.zeros((tm, LANES), F32)
    for k in range(TOP_K):
        hot = lane == ri[:, k:k + 1]
        base = jnp.sum(jnp.where(hot, start_row, 0.0), axis=-1, keepdims=True)
        dest = jnp.where(lane == k, base + ri[:, TOP_K + k:TOP_K + k + 1].astype(F32), dest)
    dest_ref[...] = dest.astype(I32)


def _dest(ri, cnt, tm):
    T = ri.shape[0]
    return pl.pallas_call(
        _dest_kernel,
        grid=(T // tm,),
        in_specs=[pl.BlockSpec((tm, LANES), lambda i: (i, 0)), pl.BlockSpec((SUBLANES, LANES), lambda i: (0, 0))],
        out_specs=[pl.BlockSpec((tm, LANES), lambda i: (i, 0)), pl.BlockSpec((SUBLANES, LANES), lambda i: (0, 0))],
        out_shape=[jax.ShapeDtypeStruct((T, LANES), I32), jax.ShapeDtypeStruct((SUBLANES, LANES), I32)],
        compiler_params=_params("arbitrary"),
        name="moe_dest",
    )(ri, cnt)


def _pad_copies(tab_ref, zbuf, xin_hbm, zsem):
    out = []
    for e in range(N_EXPERTS):
        cnt = tab_ref[e]
        pad = tab_ref[LANES + e] - cnt
        base = tab_ref[2 * LANES + e] + cnt
        piece = 1
        while piece < MOE_ROWS:
            cond = (pad & piece) != 0
            if piece < SUBLANES:
                for r in range(piece):
                    out.append((cond, pltpu.make_async_copy(
                        zbuf.at[pl.ds(0, 1)], xin_hbm.at[pl.ds(base + r, 1)], zsem)))
            else:
                out.append((cond, pltpu.make_async_copy(
                    zbuf.at[pl.ds(0, piece)], xin_hbm.at[pl.ds(pl.multiple_of(base, SUBLANES), piece)], zsem)))
            base = base + jnp.where(cond, piece, 0)
            piece *= 2
    last = N_EXPERTS - 1
    n_live = (tab_ref[2 * LANES + last] + tab_ref[LANES + last]) // MOE_ROWS
    n_blocks = xin_hbm.shape[0] // MOE_ROWS
    for blk in range(n_blocks - N_EXPERTS, n_blocks):
        out.append((blk >= n_live, pltpu.make_async_copy(
            zbuf, xin_hbm.at[pl.ds(blk * MOE_ROWS, MOE_ROWS)], zsem)))
    return out


def _dispatch_kernel(tab_ref, dest_hbm, h2_hbm, xin_hbm, idx_smem, hbuf, zbuf, lsem, ssem, isem, zsem):
    i = pl.program_id(0)
    n = pl.num_programs(0)
    tm = hbuf.shape[1]

    def load(j):
        return (pltpu.make_async_copy(h2_hbm.at[pl.ds(j * tm, tm)], hbuf.at[j % 3], lsem.at[j % 3]),
                pltpu.make_async_copy(dest_hbm.at[j], idx_smem.at[pl.ds((j % 3) * (TOP_K * tm), TOP_K * tm)],
                                      isem.at[j % 3]))

    def wait_scatters(j):
        for _ in range(TOP_K):
            pltpu.make_async_copy(hbuf.at[j % 3], xin_hbm.at[pl.ds(0, tm)], ssem.at[j % 3]).wait()

    @pl.when(i == 0)
    def _():
        for cp in load(0):
            cp.start()
        zbuf[...] = jnp.zeros_like(zbuf)
        for cond, cp in _pad_copies(tab_ref, zbuf, xin_hbm, zsem):
            @pl.when(cond)
            def _():
                cp.start()

    @pl.when(i + 1 < n)
    def _():
        for cp in load(i + 1):
            cp.start()

    for cp in load(i):
        cp.wait()
    for slot in range(3):
        @pl.when(i % 3 == slot)
        def _():
            def issue(t, _):
                for k in range(TOP_K):
                    d = idx_smem[slot * (TOP_K * tm) + TOP_K * t + k]
                    pltpu.make_async_copy(hbuf.at[slot, pl.ds(t, 1)], xin_hbm.at[pl.ds(d, 1)],
                                          ssem.at[slot]).start(priority=k % 2)
                return 0

            lax.fori_loop(0, tm, issue, 0)

    @pl.when(i > 0)
    def _():
        wait_scatters(i - 1)

    @pl.when(i == n - 1)
    def _():
        wait_scatters(i)
        for cond, cp in _pad_copies(tab_ref, zbuf, xin_hbm, zsem):
            @pl.when(cond)
            def _():
                cp.wait()


def _dispatch(tab_flat, dest_tiles, h2p, n_rows, tm):
    T, W = h2p.shape
    return pl.pallas_call(
        _dispatch_kernel,
        grid_spec=pltpu.PrefetchScalarGridSpec(
            num_scalar_prefetch=1,
            grid=(T // tm,),
            in_specs=[pl.BlockSpec(memory_space=pl.ANY), pl.BlockSpec(memory_space=pl.ANY)],
            out_specs=pl.BlockSpec(memory_space=pl.ANY),
            scratch_shapes=[
                pltpu.SMEM((3 * TOP_K * tm,), I32),
                pltpu.VMEM((3, tm, W), F32),
                pltpu.VMEM((MOE_ROWS, W), F32),
                pltpu.SemaphoreType.DMA((3,)), pltpu.SemaphoreType.DMA((3,)), pltpu.SemaphoreType.DMA((3,)),
                pltpu.SemaphoreType.DMA,
            ],
        ),
        out_shape=jax.ShapeDtypeStruct((n_rows, W), F32),
        compiler_params=_params("arbitrary"),
        name="moe_dispatch",
    )(tab_flat, dest_tiles, h2p)


def _expert_kernel(be_ref, bs_ref, nv_ref, x_ref, w1_ref, b1_ref, w2_ref, b2_ref, y_ref, w1b, w2b, *, d_ff):
    s = pl.program_id(0)
    live = s < nv_ref[0]
    fresh = (s == 0) | (be_ref[s] != be_ref[jnp.maximum(s - 1, 0)])

    @pl.when(live & fresh)
    def _():
        w1b[...] = w1_ref[0].astype(BF16)
        w2b[...] = w2_ref[0].astype(BF16)

    @pl.when(live)
    def _():
        hc = _dot(x_ref[...].astype(BF16), w1b[...]) + b1_ref[0]
        gt = jnp.minimum(hc[:, :d_ff], SWIGLU_LIMIT)
        up = jnp.clip(hc[:, d_ff:], -SWIGLU_LIMIT, SWIGLU_LIMIT)
        glu = gt * jax.nn.sigmoid(SWIGLU_ALPHA * gt)
        act = ((up + 1.0) * glu).astype(BF16)
        y_ref[...] = _dot(act, w2b[...]) + b2_ref[0]

    @pl.when(jnp.logical_not(live))
    def _():
        y_ref[...] = jnp.zeros_like(y_ref)


def _experts(blk_e, blk_src, n_live, xin, w1, b1, w2, b2):
    n_rows, W = xin.shape
    E, D, two_f = w1.shape
    d_ff = two_f // 2
    return pl.pallas_call(
        functools.partial(_expert_kernel, d_ff=d_ff),
        grid_spec=pltpu.PrefetchScalarGridSpec(
            num_scalar_prefetch=3,
            grid=(n_rows // MOE_ROWS,),
            in_specs=[
                pl.BlockSpec((MOE_ROWS, W), lambda s, be, bs, nv: (bs[s], 0)),
                pl.BlockSpec((1, D, two_f), lambda s, be, bs, nv: (be[s], 0, 0)),
                pl.BlockSpec((1, 1, two_f), lambda s, be, bs, nv: (be[s], 0, 0)),
                pl.BlockSpec((1, d_ff, D), lambda s, be, bs, nv: (be[s], 0, 0)),
                pl.BlockSpec((1, 1, D), lambda s, be, bs, nv: (be[s], 0, 0)),
            ],
            out_specs=pl.BlockSpec((MOE_ROWS, W), lambda s, be, bs, nv: (s, 0)),
            scratch_shapes=[pltpu.VMEM((D, two_f), BF16), pltpu.VMEM((d_ff, D), BF16)],
        ),
        out_shape=jax.ShapeDtypeStruct((n_rows, W), F32),
        compiler_params=_params("arbitrary"),
        name="moe_experts",
    )(blk_e, blk_src, n_live, xin, w1, b1, w2, b2)


def _combine_kernel(dest_hbm, x1_ref, rw_ref, gf_ref, y_hbm, o_ref, idx_smem, ybuf, gsem, isem):
    i = pl.program_id(0)
    n = pl.num_programs(0)
    tm = x1_ref.shape[0]

    def idx_copy(j):
        return pltpu.make_async_copy(dest_hbm.at[j], idx_smem.at[pl.ds((j % 2) * (TOP_K * tm), TOP_K * tm)],
                                     isem.at[j % 2])

    def issue_into(slot):
        def issue(t, _):
            for k in range(TOP_K):
                d = idx_smem[slot * (TOP_K * tm) + TOP_K * t + k]
                pltpu.make_async_copy(y_hbm.at[pl.ds(d, 1)], ybuf.at[slot, k, pl.ds(t, 1)],
                                      gsem.at[slot]).start(priority=k % 2)
            return 0

        lax.fori_loop(0, tm, issue, 0)

    def issue_gathers(j):
        for slot in range(2):
            @pl.when(j % 2 == slot)
            def _():
                issue_into(slot)

    @pl.when(i == 0)
    def _():
        idx_copy(0).start()
        idx_copy(0).wait()
        issue_into(0)

        @pl.when(n > 1)
        def _():
            idx_copy(1).start()

    @pl.when(i + 1 < n)
    def _():
        idx_copy(i + 1).wait()
        issue_gathers(i + 1)

        @pl.when(i + 2 < n)
        def _():
            idx_copy(i + 2).start()

    slot = i % 2
    for k in range(TOP_K):
        pltpu.make_async_copy(y_hbm.at[pl.ds(0, tm)], ybuf.at[slot, k], gsem.at[slot]).wait()

    rw = rw_ref[...]
    x2 = x1_ref[...]
    for k in range(TOP_K):
        x2 = x2 + rw[:, k:k + 1] * ybuf[slot, k]
    o_ref[...] = _rms(x2, gf_ref[...])


def _combine(dest_tiles, x1, rw, gf, y, tm):
    T, D = x1.shape
    W = y.shape[1]
    return pl.pallas_call(
        _combine_kernel,
        grid=(T // tm,),
        in_specs=[
            pl.BlockSpec(memory_space=pl.ANY),
            pl.BlockSpec((tm, D), lambda i: (i, 0)),
            pl.BlockSpec((tm, LANES), lambda i: (i, 0)),
            pl.BlockSpec((1, D), lambda i: (0, 0)),
            pl.BlockSpec(memory_space=pl.ANY),
        ],
        out_specs=pl.BlockSpec((tm, D), lambda i: (i, 0)),
        out_shape=jax.ShapeDtypeStruct((T, D), F32),
        scratch_shapes=[
            pltpu.SMEM((2 * TOP_K * tm,), I32),
            pltpu.VMEM((2, TOP_K, tm, W), F32),
            pltpu.SemaphoreType.DMA((2,)), pltpu.SemaphoreType.DMA((2,)),
        ],
        compiler_params=_params("arbitrary"),
        name="moe_combine",
    )(dest_tiles, x1, rw, gf, y)


def _rope_tables(positions):
    half = ROT_DIM // 2
    freqs = ROPE_THETA ** (-jnp.arange(0, ROT_DIM, 2, dtype=F32) / ROT_DIM)
    ang = positions.astype(F32)[..., None] * freqs
    cos, sin = jnp.cos(ang), jnp.sin(ang)
    d = np.arange(LANES) % HEAD_DIM
    f_idx = d % half
    in_lo = jnp.asarray(d < half)
    in_hi = jnp.asarray((d >= half) & (d < ROT_DIM))
    cos_l, sin_l = cos[..., f_idx], sin[..., f_idx]
    c = jnp.where(in_lo | in_hi, cos_l, 1.0)
    s1 = jnp.where(in_lo, -sin_l, 0.0)
    s2 = jnp.where(in_hi, sin_l, 0.0)
    return c, s1, s2


def _arrange_w_in(w_in, d_model):
    nq, nkv, ng = NSA_HEADS * HEAD_DIM, NSA_KV_GROUPS * HEAD_DIM, NSA_HEADS * 3
    fw = FOX_HEADS * HEAD_DIM
    sizes = [nq] + [nkv] * 6 + [ng, fw, fw, fw, FOX_HEADS, d_model, d_model]
    offs = np.concatenate([[0], np.cumsum(sizes)])
    piece = lambda k: w_in[:, offs[k]:offs[k + 1]]
    gates, ff = piece(7), piece(11)
    per_group = HEADS_PER_GROUP * 3
    zeros = lambda n: jnp.zeros((w_in.shape[0], n), w_in.dtype)
    misc0 = jnp.concatenate([gates[:, :per_group], zeros(FGATE_LANE - per_group), ff,
                             zeros(LANES - FGATE_LANE - FOX_HEADS)], axis=1)
    misc1 = jnp.concatenate([gates[:, per_group:], zeros(LANES - per_group)], axis=1)
    cols = [piece(k) for k in range(7)] + [piece(8), piece(9), piece(10), piece(12), piece(13), misc0, misc1]
    return jnp.concatenate(cols, axis=1).astype(BF16)


def _split_bf16(a):
    hi = a.astype(BF16)
    return hi, (a - hi.astype(F32)).astype(BF16)


def kernel(x, positions, norm1_g, w_in, cmp_pos_emb, cmp_w1, cmp_w2, fox_f_bias, w_proj_nsa, w_proj_fox, w_out,
           norm2_g, router_w, router_b, expert_w1, expert_b1, expert_w2, expert_b2, norm_f_g):
    B, S, D = x.shape
    T = B * S
    assert norm1_g.shape[0] == 1, "the combine kernel fuses the output norm, so it must follow the only layer"
    assert S % KV_CHUNK == 0 or S < KV_CHUNK
    assert S // SEL_BLOCK <= LANES - HEAD_DIM and S % (CMP_STRIDE * SUBLANES) == 0
    tm_proj = min(512, S)
    assert WINDOW % tm_proj == 0 or tm_proj % WINDOW == 0
    tm_tok = min(512, T)
    tm_comb = min(256, T)
    l = 0

    rope_c, rope_s1, rope_s2 = _rope_tables(positions)
    n_sb = S // SEL_BLOCK
    n_rows16 = S // CMP_STRIDE
    ci = np.arange(n_rows16)[None, :] * CMP_STRIDE
    sj = np.arange(LANES)[:, None] * SEL_BLOCK
    overlap = jnp.asarray(((ci < sj + SEL_BLOCK) & (ci + CMP_BLOCK > sj) & (np.arange(LANES)[:, None] < n_sb)
                           & (np.arange(n_rows16)[None, :] < n_rows16 - 1)).astype(np.float32), BF16)

    w_all = _arrange_w_in(w_in[l], D)
    fb_row = jnp.zeros((1, LANES), F32).at[0, FGATE_LANE:FGATE_LANE + FOX_HEADS].set(fox_f_bias[l].astype(F32))
    (qraw, qrot, kvc, ks, vs, kw, vw, fq, fk, fv, sgn, sgf, misc) = _inproj(
        x, norm1_g[l][None, :], w_all, rope_c, rope_s1, rope_s2, fb_row, tm_proj)
    qrot = qrot.reshape(B, NSA_HEADS, S, LANES)
    fq = fq.reshape(B, FOX_HEADS, S, LANES)
    fv = fv.reshape(B, FOX_HEADS, S, LANES)

    kv_rows = kvc.reshape(B, 2, NSA_KV_GROUPS, n_rows16, CMP_STRIDE * HEAD_DIM)
    pe = cmp_pos_emb[l].reshape(2, 1, CMP_BLOCK * HEAD_DIM).astype(F32)
    cw1 = jnp.pad(cmp_w1[l], ((0, 0), (0, 0), (0, LANES - HEAD_DIM))).astype(BF16)
    cw2 = jnp.pad(cmp_w2[l], ((0, 0), (0, LANES - HEAD_DIM), (0, 0))).astype(BF16)
    cmp_kv = _compress(kv_rows, pe, cw1, cw2)

    front_k = jnp.zeros((B, NSA_KV_GROUPS, WINDOW, LANES), BF16).at[..., HEAD_DIM].set(NEG)
    kw_pad = jnp.concatenate([front_k, kw], axis=2)
    vw_pad = jnp.concatenate([jnp.zeros((B, NSA_KV_GROUPS, WINDOW, LANES), BF16), vw], axis=2)
    o_nsa = _nsa(qraw, qrot, cmp_kv, ks, vs, kw_pad, vw_pad, misc, overlap)
    o_fox = _fox(fq, fk, fv)

    wr = jnp.pad(router_w[l], ((0, 0), (0, LANES - N_EXPERTS)))
    wr_hi, wr_lo = _split_bf16(wr)
    br = jnp.pad(router_b[l], (0, LANES - N_EXPERTS))[None, :].astype(F32)
    x1, h2, ri, rw, cnt = _merge(
        x.reshape(T, D), o_nsa.reshape(T, -1), o_fox.reshape(T, -1), sgn.reshape(T, D), sgf.reshape(T, D),
        w_proj_nsa[l].astype(BF16), w_proj_fox[l].astype(BF16), w_out[l].astype(BF16),
        norm2_g[l][None, :], wr_hi, wr_lo, br, tm_tok)

    dest, tab = _dest(ri, cnt, tm_tok)
    dest4 = dest[:, :TOP_K]
    pend = tab[3, :N_EXPERTS]
    n_blocks = (T * TOP_K) // MOE_ROWS + N_EXPERTS
    n_live = (pend[N_EXPERTS - 1] // MOE_ROWS).astype(I32)
    blk_src = jnp.minimum(jnp.arange(n_blocks, dtype=I32), n_live - 1)
    blk_e = jnp.sum((pend[None, :] <= (blk_src * MOE_ROWS)[:, None]).astype(I32), axis=1)
    blk_e = jnp.minimum(blk_e, N_EXPERTS - 1)

    xin = _dispatch(tab[:3].reshape(-1), dest4.reshape(T // tm_tok, TOP_K * tm_tok), h2,
                    n_blocks * MOE_ROWS, tm_tok)
    y = _experts(blk_e, blk_src, n_live[None], xin, expert_w1[l], expert_b1[l][:, None, :],
                 expert_w2[l], expert_b2[l][:, None, :])
    out = _combine(dest4.reshape(T // tm_comb, TOP_K * tm_comb), x1, rw, norm_f_g[None, :], y, tm_comb)
    return out.reshape(B, S, D)
```

```python
import functools

import numpy as np
import jax
import jax.numpy as jnp
from jax import lax
from jax.experimental import pallas as pl
from jax.experimental.pallas import tpu as pltpu

HEAD_DIM = 64
NSA_HEADS = 8
NSA_KV_GROUPS = 2
HEADS_PER_GROUP = NSA_HEADS // NSA_KV_GROUPS
FOX_HEADS = 8
ROT_DIM = HEAD_DIM // 4
ROPE_THETA = 500000.0
CMP_BLOCK = 32
CMP_STRIDE = 16
SEL_BLOCK = 64
N_SEL = 8
WINDOW = 512
N_EXPERTS = 32
TOP_K = 4
SWIGLU_ALPHA = 1.702
SWIGLU_LIMIT = 7.0
NORM_EPS = 1e-6
NEG = -1e30
BELOW_NEG = -3e38
FORCE_BONUS = 1e4

LANES = 128
SUBLANES = 8
NSA_Q_TILE = 256
KV_CHUNK = 512
FOX_TILE = 512
MOE_ROWS = 512
FGATE_LANE = 32
VMEM_LIMIT = 56 * 1024 * 1024

F32 = jnp.float32
BF16 = jnp.bfloat16
I32 = jnp.int32


def _dot(a, b):
    return jnp.dot(a, b, preferred_element_type=F32)


def _dot_nt(a, b):
    return lax.dot_general(a, b, (((1,), (1,)), ((), ())), preferred_element_type=F32)


def _params(*sem):
    return pltpu.CompilerParams(dimension_semantics=sem, vmem_limit_bytes=VMEM_LIMIT)


def _rms(x, g):
    return x * lax.rsqrt(jnp.mean(x * x, axis=-1, keepdims=True) + NORM_EPS) * g


def _split3(a):
    t1 = a.astype(BF16)
    r1 = a - t1.astype(F32)
    t2 = r1.astype(BF16)
    t3 = (r1 - t2.astype(F32)).astype(BF16)
    return t1, t2, t3


def _inproj_kernel(x_ref, g_ref, w_ref, rc_ref, rs1_ref, rs2_ref, fb_ref,
                   qraw_ref, qrot_ref, kvc_ref, ks_ref, vs_ref, kw_ref, vw_ref,
                   fq_ref, fk_ref, fv_ref, sgn_ref, sgf_ref, misc_ref, csum_scr, *, d_model):
    scale = HEAD_DIM ** -0.5
    i = pl.program_id(1)
    tm = x_ref.shape[1]
    xn = _rms(x_ref[0], g_ref[...]).astype(BF16)
    rc, rs1, rs2 = rc_ref[0], rs1_ref[0], rs2_ref[0]
    lane = lax.broadcasted_iota(I32, (tm, LANES), 1)
    low = lane < HEAD_DIM
    one_at_64 = jnp.where(lane == HEAD_DIM, 1.0, 0.0)

    def rope(slab):
        half = ROT_DIM // 2
        return slab * rc + pltpu.roll(slab, LANES - half, 1) * rs1 + pltpu.roll(slab, half, 1) * rs2

    def put_pair(ref, lead, slab, tail):
        ref[lead + (0,)] = jnp.where(low, slab, tail).astype(BF16)
        ref[lead + (1,)] = jnp.where(low, pltpu.roll(slab, HEAD_DIM, 1), tail).astype(BF16)

    misc_off = w_ref.shape[1] - 2 * LANES
    r = _dot(xn, w_ref[:, misc_off:])
    m0 = r[:, :LANES]
    z = m0 + fb_ref[...]
    log_f = jnp.minimum(z, 0.0) - jnp.log(1.0 + jnp.exp(-jnp.abs(z)))
    misc_ref[0, 0] = jax.nn.sigmoid(m0)
    misc_ref[0, 1] = jax.nn.sigmoid(r[:, LANES:])

    @pl.when(i == 0)
    def _():
        csum_scr[...] = jnp.zeros_like(csum_scr)

    in_gate = (lane >= FGATE_LANE) & (lane < FGATE_LANE + FOX_HEADS)
    t1, t2, t3 = _split3(jnp.where(in_gate, log_f, 0.0))
    r_i = lax.broadcasted_iota(I32, (tm, tm), 0)
    c_i = lax.broadcasted_iota(I32, (tm, tm), 1)
    upto = jnp.where(c_i <= r_i, 1.0, 0.0).astype(BF16)
    parts = _dot(upto, jnp.concatenate([t1, t2, t3], axis=1))
    csum = parts[:, :LANES] + parts[:, LANES:2 * LANES] + parts[:, 2 * LANES:] + csum_scr[0:1, :]
    csum_scr[...] = jnp.broadcast_to(csum[tm - 1:tm, :], csum_scr.shape)

    nq = NSA_HEADS * HEAD_DIM
    r = _dot(xn, w_ref[:, 0:nq]) * scale
    for s in range(nq // LANES):
        slab = r[:, LANES * s:LANES * (s + 1)]
        qraw_ref[0, 2 * s] = slab[:, :HEAD_DIM].astype(BF16)
        qraw_ref[0, 2 * s + 1] = slab[:, HEAD_DIM:].astype(BF16)
        put_pair(qrot_ref, (0, s), rope(slab), 0.0)
    off = nq

    r = _dot(xn, w_ref[:, off:off + 6 * LANES])
    for t in range(2):
        slab = r[:, t * LANES:(t + 1) * LANES]
        kvc_ref[0, t, 0] = slab[:, :HEAD_DIM].astype(BF16)
        kvc_ref[0, t, 1] = slab[:, HEAD_DIM:].astype(BF16)
    tok = i * tm + lax.broadcasted_iota(I32, (tm, LANES), 0)
    block_hot = jnp.where(lane - HEAD_DIM == (tok >> (SEL_BLOCK.bit_length() - 1)), 1.0, 0.0)
    put_pair(ks_ref, (0,), rope(r[:, 2 * LANES:3 * LANES]), block_hot)
    put_pair(vs_ref, (0,), r[:, 3 * LANES:4 * LANES], one_at_64)
    put_pair(kw_ref, (0,), rope(r[:, 4 * LANES:5 * LANES]), 0.0)
    put_pair(vw_ref, (0,), r[:, 5 * LANES:6 * LANES], one_at_64)
    off += 6 * LANES

    fw = FOX_HEADS * HEAD_DIM
    ones3 = jnp.where((lane >= HEAD_DIM) & (lane < HEAD_DIM + 3), 1.0, 0.0)
    r = _dot(xn, w_ref[:, off:off + fw]) * scale
    for s in range(fw // LANES):
        put_pair(fq_ref, (0, s), r[:, LANES * s:LANES * (s + 1)], ones3)
    off += fw
    r = _dot(xn, w_ref[:, off:off + fw])
    for s in range(fw // LANES):
        slab = r[:, LANES * s:LANES * (s + 1)]
        for par in range(2):
            h = 2 * s + par
            neg_c = -csum[:, FGATE_LANE + h:FGATE_LANE + h + 1]
            c1, c2, c3 = _split3(neg_c)
            tail = jnp.where(lane == HEAD_DIM, c1.astype(F32), jnp.where(
                lane == HEAD_DIM + 1, c2.astype(F32), jnp.where(lane == HEAD_DIM + 2, c3.astype(F32), 0.0)))
            head = slab if par == 0 else pltpu.roll(slab, HEAD_DIM, 1)
            fk_ref[0, h] = jnp.where(low, head, tail).astype(BF16)
    off += fw
    r = _dot(xn, w_ref[:, off:off + fw])
    for s in range(fw // LANES):
        put_pair(fv_ref, (0, s), r[:, LANES * s:LANES * (s + 1)], one_at_64)
    off += fw

    step = min(512, d_model)
    for ref in (sgn_ref, sgf_ref):
        for c in range(0, d_model, step):
            r = _dot(xn, w_ref[:, off + c:off + c + step])
            ref[0, :, c:c + step] = jax.nn.sigmoid(r).astype(BF16)
        off += d_model


def _inproj(x, norm_g, w_all, rope_c, rope_s1, rope_s2, f_bias_row, tm):
    B, S, D = x.shape
    ncol = w_all.shape[1]
    G = NSA_KV_GROUPS
    wide = lambda heads: jax.ShapeDtypeStruct((B, heads, S, LANES), BF16)
    pair_shape = lambda heads: jax.ShapeDtypeStruct((B, heads // 2, 2, S, LANES), BF16)
    pair_spec = lambda heads: pl.BlockSpec((1, heads // 2, 2, tm, LANES), lambda b, i: (b, 0, 0, i, 0))
    kvspec = pl.BlockSpec((1, G, tm, LANES), lambda b, i: (b, 0, i, 0))
    tabspec = pl.BlockSpec((1, tm, LANES), lambda b, i: (b, i, 0))
    gspec = pl.BlockSpec((1, tm, D), lambda b, i: (b, i, 0))
    return pl.pallas_call(
        functools.partial(_inproj_kernel, d_model=D),
        grid=(B, S // tm),
        in_specs=[
            pl.BlockSpec((1, tm, D), lambda b, i: (b, i, 0)),
            pl.BlockSpec((1, D), lambda b, i: (0, 0)),
            pl.BlockSpec((D, ncol), lambda b, i: (0, 0)),
            tabspec, tabspec, tabspec,
            pl.BlockSpec((1, LANES), lambda b, i: (0, 0)),
        ],
        out_specs=[
            pl.BlockSpec((1, NSA_HEADS, tm, HEAD_DIM), lambda b, i: (b, 0, i, 0)),
            pair_spec(NSA_HEADS),
            pl.BlockSpec((1, 2, G, tm, HEAD_DIM), lambda b, i: (b, 0, 0, i, 0)),
            kvspec, kvspec, kvspec, kvspec,
            pair_spec(FOX_HEADS),
            pl.BlockSpec((1, FOX_HEADS, tm, LANES), lambda b, i: (b, 0, i, 0)),
            pair_spec(FOX_HEADS),
            gspec, gspec,
            pl.BlockSpec((1, 2, tm, LANES), lambda b, i: (b, 0, i, 0)),
        ],
        out_shape=[
            jax.ShapeDtypeStruct((B, NSA_HEADS, S, HEAD_DIM), BF16),
            pair_shape(NSA_HEADS),
            jax.ShapeDtypeStruct((B, 2, G, S, HEAD_DIM), BF16),
            wide(G), wide(G), wide(G), wide(G),
            pair_shape(FOX_HEADS), wide(FOX_HEADS), pair_shape(FOX_HEADS),
            jax.ShapeDtypeStruct((B, S, D), BF16), jax.ShapeDtypeStruct((B, S, D), BF16),
            jax.ShapeDtypeStruct((B, 2, S, LANES), F32),
        ],
        scratch_shapes=[pltpu.VMEM((SUBLANES, LANES), F32)],
        compiler_params=_params("parallel", "arbitrary"),
        name="inproj",
    )(x, norm_g, w_all, rope_c, rope_s1, rope_s2, f_bias_row)


def _compress_kernel(r_ref, pe_ref, w1_ref, w2_ref, o_ref):
    half = CMP_STRIDE * HEAD_DIM
    rows = r_ref[0, 0, 0].astype(F32)
    lo = (rows + pe_ref[0, :, :half]).astype(BF16)
    hi = (rows + pe_ref[0, :, half:]).astype(BF16)
    a = _dot(lo, w1_ref[0, :half])
    b = _dot(hi, w1_ref[0, half:])
    n = a.shape[0]
    pre = a + pltpu.roll(b, n - 1, 0)
    act = jax.nn.gelu(pre, approximate=True).astype(BF16)
    o_ref[0, 0, 0] = _dot(act, w2_ref[0]).astype(BF16)


def _compress(kv_rows, pe, w1, w2):
    B, _, G, n_rows, width = kv_rows.shape
    return pl.pallas_call(
        _compress_kernel,
        grid=(B, 2, G),
        in_specs=[
            pl.BlockSpec((1, 1, 1, n_rows, width), lambda b, t, g: (b, t, g, 0, 0)),
            pl.BlockSpec((1, 1, 2 * width), lambda b, t, g: (t, 0, 0)),
            pl.BlockSpec((1, 2 * width, LANES), lambda b, t, g: (t, 0, 0)),
            pl.BlockSpec((1, LANES, HEAD_DIM), lambda b, t, g: (t, 0, 0)),
        ],
        out_specs=pl.BlockSpec((1, 1, 1, n_rows, HEAD_DIM), lambda b, t, g: (b, t, g, 0, 0)),
        out_shape=jax.ShapeDtypeStruct((B, 2, G, n_rows, HEAD_DIM), BF16),
        compiler_params=_params("parallel", "parallel", "parallel"),
        name="nsa_compress",
    )(kv_rows, pe, w1, w2)


def _softmax_rows(logits, ok):
    l = jnp.where(ok, logits, NEG)
    m = jnp.max(l, axis=-1, keepdims=True)
    e = jnp.where(ok, jnp.exp(l - m), 0.0)
    s = jnp.sum(e, axis=-1, keepdims=True)
    return e / jnp.where(s > 0.0, s, 1.0)


def _nsa_kernel(qraw_ref, qrot_ref, kcmp_ref, vcmp_ref, ks_ref, vs_ref, kw_ref, vw_ref,
                misc_ref, overlap_ref, o_ref, *, seq, chunk):
    qi = pl.program_id(2)
    n_full = qi * NSA_Q_TILE // chunk
    for last in range(seq // chunk):
        @pl.when(n_full == last)
        def _():
            _nsa_tile(qraw_ref, qrot_ref, kcmp_ref, vcmp_ref, ks_ref, vs_ref, kw_ref, vw_ref,
                      misc_ref, overlap_ref, o_ref, qi, last, seq=seq, chunk=chunk)


def _nsa_tile(qraw_ref, qrot_ref, kcmp_ref, vcmp_ref, ks_ref, vs_ref, kw_ref, vw_ref,
              misc_ref, overlap_ref, o_ref, qi, last, *, seq, chunk):
    tq_n = NSA_Q_TILE
    hpg = HEADS_PER_GROUP
    rows_n = hpg * tq_n
    n_sb = seq // SEL_BLOCK
    n_sel = min(N_SEL, n_sb)
    t0 = qi * tq_n
    tq = t0 + lax.broadcasted_iota(I32, (tq_n, 1), 0)
    qrot = qrot_ref[0].reshape(rows_n, LANES).astype(F32)
    lane_r = lax.broadcasted_iota(I32, (rows_n, LANES), 1)
    row_in_tile = lax.broadcasted_iota(I32, (rows_n, 1), 0) & (tq_n - 1)

    q_win = jnp.where(lane_r == HEAD_DIM, 1.0, qrot).astype(BF16)
    wlen = WINDOW + tq_n
    w0 = pl.multiple_of(t0, tq_n)
    kw = kw_ref[0, 0, pl.ds(w0, wlen), :]
    vw = vw_ref[0, 0, pl.ds(w0, wlen), :]
    sw = _dot_nt(q_win, kw)
    col = lax.broadcasted_iota(I32, (rows_n, tq_n), 1)
    s_old = jnp.where(col > row_in_tile, sw[:, :tq_n], NEG)
    s_new = jnp.where(col <= row_in_tile, sw[:, WINDOW:], NEG)
    s_mid = [sw[:, tq_n:WINDOW]] if WINDOW > tq_n else []
    sw = jnp.concatenate([s_old] + s_mid + [s_new], axis=1)
    mw = jnp.broadcast_to(jnp.max(sw, axis=-1, keepdims=True), (rows_n, LANES))
    acc_w = _dot(jnp.exp(sw - jnp.tile(mw, (1, wlen // LANES))).astype(BF16), vw)

    q = qraw_ref[0].reshape(rows_n, HEAD_DIM)
    n_c = kcmp_ref.shape[3]
    lg = _dot_nt(q, kcmp_ref[0, 0, 0])
    cidx = lax.broadcasted_iota(I32, (tq_n, n_c), 1)
    cmask = cidx * CMP_STRIDE + (CMP_BLOCK - 1) <= tq
    vcmp = vcmp_ref[0, 0, 0]
    o_c = []
    p_sum = jnp.zeros((tq_n, n_c), F32)
    for h in range(hpg):
        p = _softmax_rows(lg[h * tq_n:(h + 1) * tq_n], cmask)
        p_sum = p_sum + p
        o_c.append(_dot(p.astype(BF16), vcmp))

    p_hi = p_sum.astype(BF16)
    p_lo = (p_sum - p_hi.astype(F32)).astype(BF16)
    imp = (_dot_nt(overlap_ref[...], p_hi) + _dot_nt(overlap_ref[...], p_lo))[:n_sb]
    j = lax.broadcasted_iota(I32, (n_sb, tq_n), 0)
    t_row = t0 + lax.broadcasted_iota(I32, (n_sb, tq_n), 1)
    cur = t_row >> (SEL_BLOCK.bit_length() - 1)
    forced = jnp.where(j == 0, 1.0, jnp.where(j == cur, 1.0, jnp.where(j == cur - 1, 1.0, 0.0)))
    score = jnp.where(forced > 0.0, imp + FORCE_BONUS, jnp.where(j * SEL_BLOCK <= t_row, imp, NEG))
    beaten = jnp.zeros((n_sb, tq_n), F32)
    for jp in range(n_sb):
        row = score[jp:jp + 1, :]
        wins_ties = jnp.where(row >= score, 1.0, 0.0)
        wins = jnp.where(row > score, 1.0, 0.0)
        beaten = beaten + jnp.where(j > jp, wins_ties, wins)
    penalty = jnp.where(beaten < n_sel, 0.0, NEG)
    penalty = jnp.concatenate([penalty, jnp.zeros((LANES - n_sb, tq_n), F32)], axis=0).T
    penalty = pltpu.roll(penalty, HEAD_DIM, 1)

    q_sel = jnp.where(lane_r < HEAD_DIM, qrot, jnp.concatenate([penalty] * hpg, axis=0)).astype(BF16)

    logits = [_dot_nt(q_sel, ks_ref[0, 0, c * chunk:(c + 1) * chunk, :]) for c in range(last + 1)]
    kpos = last * chunk + lax.broadcasted_iota(I32, (rows_n, chunk), 1)
    logits[last] = jnp.where(kpos <= t0 + row_in_tile, logits[last], NEG)
    m = jnp.max(logits[0], axis=-1, keepdims=True)
    for s in logits[1:]:
        m = jnp.maximum(m, jnp.max(s, axis=-1, keepdims=True))
    m = jnp.tile(jnp.broadcast_to(m, (rows_n, LANES)), (1, chunk // LANES))
    acc_s = _dot(jnp.exp(logits[0] - m).astype(BF16), vs_ref[0, 0, 0:chunk, :])
    for c in range(1, last + 1):
        acc_s = acc_s + _dot(jnp.exp(logits[c] - m).astype(BF16), vs_ref[0, 0, c * chunk:(c + 1) * chunk, :])

    g = misc_ref[0, 0]
    for h in range(hpg):
        a_s = acc_s[h * tq_n:(h + 1) * tq_n]
        a_w = acc_w[h * tq_n:(h + 1) * tq_n]
        o_s = a_s[:, :HEAD_DIM] / a_s[:, HEAD_DIM:HEAD_DIM + 1]
        o_w = a_w[:, :HEAD_DIM] / a_w[:, HEAD_DIM:HEAD_DIM + 1]
        o_h = g[:, 3 * h:3 * h + 1] * o_c[h] + g[:, 3 * h + 1:3 * h + 2] * o_s + g[:, 3 * h + 2:3 * h + 3] * o_w
        o_ref[0, :, HEAD_DIM * h:HEAD_DIM * (h + 1)] = o_h.astype(BF16)


def _nsa(qraw, qrot, cmp_kv, ks, vs, kw_pad, vw_pad, misc, overlap):
    B, _, S, _ = qraw.shape
    G = NSA_KV_GROUPS
    n_c = cmp_kv.shape[3]
    chunk = min(KV_CHUNK, S)
    kvspec = pl.BlockSpec((1, 1, S, LANES), lambda b, g, i: (b, g, 0, 0))
    padspec = pl.BlockSpec((1, 1, S + WINDOW, LANES), lambda b, g, i: (b, g, 0, 0))
    return pl.pallas_call(
        functools.partial(_nsa_kernel, seq=S, chunk=chunk),
        grid=(B, G, S // NSA_Q_TILE),
        in_specs=[
            pl.BlockSpec((1, HEADS_PER_GROUP, NSA_Q_TILE, HEAD_DIM), lambda b, g, i: (b, g, i, 0)),
            pl.BlockSpec((1, HEADS_PER_GROUP, NSA_Q_TILE, LANES), lambda b, g, i: (b, g, i, 0)),
            pl.BlockSpec((1, 1, 1, n_c, HEAD_DIM), lambda b, g, i: (b, 0, g, 0, 0)),
            pl.BlockSpec((1, 1, 1, n_c, HEAD_DIM), lambda b, g, i: (b, 1, g, 0, 0)),
            kvspec, kvspec, padspec, padspec,
            pl.BlockSpec((1, 1, NSA_Q_TILE, LANES), lambda b, g, i: (b, g, i, 0)),
            pl.BlockSpec((LANES, n_c), lambda b, g, i: (0, 0)),
        ],
        out_specs=pl.BlockSpec((1, NSA_Q_TILE, HEADS_PER_GROUP * HEAD_DIM), lambda b, g, i: (b, i, g)),
        out_shape=jax.ShapeDtypeStruct((B, S, NSA_HEADS * HEAD_DIM), BF16),
        compiler_params=_params("parallel", "parallel", "arbitrary"),
        name="nsa_attention",
    )(qraw, qrot, cmp_kv, cmp_kv, ks, vs, kw_pad, vw_pad, misc, overlap)


def _fox_kernel(q_ref, k_ref, v_ref, o_ref):
    tile = q_ref.shape[2]
    qi = pl.program_id(1)
    for last in range(k_ref.shape[2] // tile):
        @pl.when(qi == last)
        def _():
            _fox_tile(q_ref, k_ref, v_ref, o_ref, last, tile)


def _fox_tile(q_ref, k_ref, v_ref, o_ref, last, tile):
    causal = (lax.broadcasted_iota(I32, (tile, tile), 1) <= lax.broadcasted_iota(I32, (tile, tile), 0))
    for h in range(FOX_HEADS):
        q = q_ref[0, h]
        logits = [_dot_nt(q, k_ref[0, h, c * tile:(c + 1) * tile, :]) for c in range(last + 1)]
        logits[last] = jnp.where(causal, logits[last], NEG)
        m = jnp.max(logits[0], axis=-1, keepdims=True)
        for s in logits[1:]:
            m = jnp.maximum(m, jnp.max(s, axis=-1, keepdims=True))
        m = jnp.tile(jnp.broadcast_to(m, (tile, LANES)), (1, tile // LANES))
        acc = _dot(jnp.exp(logits[0] - m).astype(BF16), v_ref[0, h, 0:tile, :])
        for c in range(1, last + 1):
            acc = acc + _dot(jnp.exp(logits[c] - m).astype(BF16), v_ref[0, h, c * tile:(c + 1) * tile, :])
        o_ref[0, :, HEAD_DIM * h:HEAD_DIM * (h + 1)] = (acc[:, :HEAD_DIM] / acc[:, HEAD_DIM:HEAD_DIM + 1]).astype(BF16)


def _fox(fq, fk, fv):
    B, H, S, _ = fq.shape
    tile = min(FOX_TILE, S)
    return pl.pallas_call(
        _fox_kernel,
        grid=(B, S // tile),
        in_specs=[
            pl.BlockSpec((1, H, tile, LANES), lambda b, i: (b, 0, i, 0)),
            pl.BlockSpec((1, H, S, LANES), lambda b, i: (b, 0, 0, 0)),
            pl.BlockSpec((1, H, S, LANES), lambda b, i: (b, 0, 0, 0)),
        ],
        out_specs=pl.BlockSpec((1, tile, H * HEAD_DIM), lambda b, i: (b, i, 0)),
        out_shape=jax.ShapeDtypeStruct((B, S, H * HEAD_DIM), BF16),
        compiler_params=_params("parallel", "arbitrary"),
        name="fox_attention",
    )(fq, fk, fv)


def _merge_kernel(x_ref, on_ref, of_ref, sgn_ref, sgf_ref, wn_ref, wf_ref, wo_ref, g2_ref,
                  wr_ref, br_ref, x1_ref, h2_ref, ri_ref, rw_ref, cnt_ref, carry_scr):
    i = pl.program_id(0)
    tm = x_ref.shape[0]

    @pl.when(i == 0)
    def _():
        carry_scr[...] = jnp.zeros_like(carry_scr)

    a = _dot(on_ref[...], wn_ref[...])
    b = _dot(of_ref[...], wf_ref[...])
    mixed = (sgn_ref[...].astype(F32) * a + sgf_ref[...].astype(F32) * b).astype(BF16)
    x1 = x_ref[...] + _dot(mixed, wo_ref[...])
    x1_ref[...] = x1
    h2 = _rms(x1, g2_ref[...])
    h2_ref[...] = h2

    hh = h2.astype(BF16)
    hl = (h2 - hh.astype(F32)).astype(BF16)
    wr_hi, wr_lo = wr_ref[:, :LANES], wr_ref[:, LANES:]
    logits = _dot(hh, wr_hi) + _dot(hl, wr_hi) + _dot(hh, wr_lo) + br_ref[...]
    lane = lax.broadcasted_iota(I32, (tm, LANES), 1)
    lane_f = lane.astype(F32)
    l = jnp.where(lane < N_EXPERTS, logits, BELOW_NEG)
    idxs, vals, hots = [], [], []
    for _ in range(TOP_K):
        m = jnp.max(l, axis=-1, keepdims=True)
        idx = jnp.min(jnp.where(l == m, lane_f, float(LANES)), axis=-1, keepdims=True)
        hot = lane_f == idx
        idxs.append(idx)
        vals.append(m)
        hots.append(hot)
        l = jnp.where(hot, BELOW_NEG, l)
    exps = [jnp.exp(v - vals[0]) for v in vals]
    den = exps[0]
    for e in exps[1:]:
        den = den + e

    chosen = jnp.zeros((tm, LANES), F32)
    for hot in hots:
        chosen = chosen + jnp.where(hot, 1.0, 0.0)
    r_i = lax.broadcasted_iota(I32, (tm, tm), 0)
    c_i = lax.broadcasted_iota(I32, (tm, tm), 1)
    earlier = jnp.where(c_i < r_i, 1.0, 0.0).astype(BF16)
    before = _dot(earlier, chosen.astype(BF16)) + carry_scr[0:1, :]
    carry_scr[...] = carry_scr[...] + jnp.sum(chosen, axis=0, keepdims=True)
    cnt_ref[...] = carry_scr[...]

    ri = jnp.zeros((tm, LANES), F32)
    rw = jnp.zeros((tm, LANES), F32)
    for k in range(TOP_K):
        rank = jnp.sum(jnp.where(hots[k], before, 0.0), axis=-1, keepdims=True)
        ri = jnp.where(lane == k, idxs[k], ri)
        ri = jnp.where(lane == TOP_K + k, rank, ri)
        rw = jnp.where(lane == k, exps[k] / den, rw)
    ri_ref[...] = ri.astype(I32)
    rw_ref[...] = rw


def _merge(x2d, o_nsa, o_fox, sgn, sgf, wn, wf, wo, g2, wr_pair, br, tm):
    T, D = x2d.shape
    wq = o_nsa.shape[1]
    row = lambda w: pl.BlockSpec((tm, w), lambda i: (i, 0))
    full = lambda a: pl.BlockSpec(a.shape, lambda i: (0,) * a.ndim)
    return pl.pallas_call(
        _merge_kernel,
        grid=(T // tm,),
        in_specs=[row(D), row(wq), row(wq), row(D), row(D), full(wn), full(wf), full(wo), full(g2),
                  full(wr_pair), full(br)],
        out_specs=[row(D), row(D), row(LANES), row(LANES), pl.BlockSpec((SUBLANES, LANES), lambda i: (0, 0))],
        out_shape=[
            jax.ShapeDtypeStruct((T, D), F32), jax.ShapeDtypeStruct((T, D), F32),
            jax.ShapeDtypeStruct((T, LANES), I32), jax.ShapeDtypeStruct((T, LANES), F32),
            jax.ShapeDtypeStruct((SUBLANES, LANES), F32),
        ],
        scratch_shapes=[pltpu.VMEM((SUBLANES, LANES), F32)],
        compiler_params=_params("arbitrary"),
        name="merge_router",
    )(x2d, o_nsa, o_fox, sgn, sgf, wn, wf, wo, g2, wr_pair, br)


def _dest_kernel(ri_ref, cnt_ref, dest_ref, tab_ref):
    tm = ri_ref.shape[0]
    shift = MOE_ROWS.bit_length() - 1
    cnt = cnt_ref[...].astype(I32)
    padded = ((cnt + (MOE_ROWS - 1)) >> shift) << shift
    lane8 = lax.broadcasted_iota(I32, (SUBLANES, LANES), 1)
    pend = padded
    sh = 1
    while sh < LANES:
        pend = pend + jnp.where(lane8 >= sh, pltpu.roll(pend, sh, 1), 0)
        sh *= 2
    pstart = pend - padded
    row8 = lax.broadcasted_iota(I32, (SUBLANES, LANES), 0)
    tab_ref[...] = jnp.where(row8 == 0, cnt, jnp.where(row8 == 1, padded, jnp.where(row8 == 2, pstart, pend)))

    ri = ri_ref[...]
    lane = lax.broadcasted_iota(I32, (tm, LANES), 1)
    start_row = pstart[0:1, :].astype(F32)
    dest = jnp.zeros((tm, LANES), F32)
    for k in range(TOP_K):
        hot = lane == ri[:, k:k + 1]
        base = jnp.sum(jnp.where(hot, start_row, 0.0), axis=-1, keepdims=True)
        dest = jnp.where(lane == k, base + ri[:, TOP_K + k:TOP_K + k + 1].astype(F32), dest)
    dest_ref[0] = dest.T[:SUBLANES].astype(I32)


def _dest(ri, cnt, tm):
    T = ri.shape[0]
    return pl.pallas_call(
        _dest_kernel,
        grid=(T // tm,),
        in_specs=[pl.BlockSpec((tm, LANES), lambda i: (i, 0)), pl.BlockSpec((SUBLANES, LANES), lambda i: (0, 0))],
        out_specs=[pl.BlockSpec((1, SUBLANES, tm), lambda i: (i, 0, 0)),
                   pl.BlockSpec((SUBLANES, LANES), lambda i: (0, 0))],
        out_shape=[jax.ShapeDtypeStruct((T // tm, SUBLANES, tm), I32), jax.ShapeDtypeStruct((SUBLANES, LANES), I32)],
        compiler_params=_params("arbitrary"),
        name="moe_dest",
    )(ri, cnt)


def _pad_copies(tab_ref, zbuf, xin_hbm, zsem):
    out = []
    for e in range(N_EXPERTS):
        cnt = tab_ref[e]
        pad = tab_ref[LANES + e] - cnt
        base = tab_ref[2 * LANES + e] + cnt
        piece = 1
        while piece < MOE_ROWS:
            cond = (pad & piece) != 0
            if piece < SUBLANES:
                for r in range(piece):
                    out.append((cond, pltpu.make_async_copy(
                        zbuf.at[pl.ds(0, 1)], xin_hbm.at[pl.ds(base + r, 1)], zsem)))
            else:
                out.append((cond, pltpu.make_async_copy(
                    zbuf.at[pl.ds(0, piece)], xin_hbm.at[pl.ds(pl.multiple_of(base, SUBLANES), piece)], zsem)))
            base = base + jnp.where(cond, piece, 0)
            piece *= 2
    last = N_EXPERTS - 1
    n_live = (tab_ref[2 * LANES + last] + tab_ref[LANES + last]) // MOE_ROWS
    n_blocks = xin_hbm.shape[0] // MOE_ROWS
    for blk in range(n_blocks - N_EXPERTS, n_blocks):
        out.append((blk >= n_live, pltpu.make_async_copy(
            zbuf, xin_hbm.at[pl.ds(blk * MOE_ROWS, MOE_ROWS)], zsem)))
    return out


def _dispatch_kernel(tab_ref, dest_hbm, h2_hbm, xin_hbm, idx_smem, hbuf, zbuf, lsem, ssem, isem, zsem):
    i = pl.program_id(0)
    n = pl.num_programs(0)
    tm = hbuf.shape[1]

    def load(j):
        return (pltpu.make_async_copy(h2_hbm.at[pl.ds(j * tm, tm)], hbuf.at[j % 3], lsem.at[j % 3]),
                pltpu.make_async_copy(dest_hbm.at[j], idx_smem.at[pl.ds((j % 3) * (TOP_K * tm), TOP_K * tm)],
                                      isem.at[j % 3]))

    def wait_scatters(j):
        for _ in range(TOP_K):
            pltpu.make_async_copy(hbuf.at[j % 3], xin_hbm.at[pl.ds(0, tm)], ssem.at[j % 3]).wait()

    @pl.when(i == 0)
    def _():
        for cp in load(0):
            cp.start()
        zbuf[...] = jnp.zeros_like(zbuf)
        for cond, cp in _pad_copies(tab_ref, zbuf, xin_hbm, zsem):
            @pl.when(cond)
            def _():
                cp.start()

    @pl.when(i + 1 < n)
    def _():
        for cp in load(i + 1):
            cp.start()

    for cp in load(i):
        cp.wait()
    for slot in range(3):
        @pl.when(i % 3 == slot)
        def _():
            def issue(t, _):
                for k in range(TOP_K):
                    d = idx_smem[slot * (TOP_K * tm) + k * tm + t]
                    pltpu.make_async_copy(hbuf.at[slot, pl.ds(t, 1)], xin_hbm.at[pl.ds(d, 1)],
                                          ssem.at[slot]).start(priority=k % 2)
                return 0

            lax.fori_loop(0, tm, issue, 0)

    @pl.when(i > 0)
    def _():
        wait_scatters(i - 1)

    @pl.when(i == n - 1)
    def _():
        wait_scatters(i)
        for cond, cp in _pad_copies(tab_ref, zbuf, xin_hbm, zsem):
            @pl.when(cond)
            def _():
                cp.wait()


def _dispatch(tab_flat, dest_tiles, h2p, n_rows, tm):
    T, W = h2p.shape
    return pl.pallas_call(
        _dispatch_kernel,
        grid_spec=pltpu.PrefetchScalarGridSpec(
            num_scalar_prefetch=1,
            grid=(T // tm,),
            in_specs=[pl.BlockSpec(memory_space=pl.ANY), pl.BlockSpec(memory_space=pl.ANY)],
            out_specs=pl.BlockSpec(memory_space=pl.ANY),
            scratch_shapes=[
                pltpu.SMEM((3 * TOP_K * tm,), I32),
                pltpu.VMEM((3, tm, W), F32),
                pltpu.VMEM((MOE_ROWS, W), F32),
                pltpu.SemaphoreType.DMA((3,)), pltpu.SemaphoreType.DMA((3,)), pltpu.SemaphoreType.DMA((3,)),
                pltpu.SemaphoreType.DMA,
            ],
        ),
        out_shape=jax.ShapeDtypeStruct((n_rows, W), F32),
        compiler_params=_params("arbitrary"),
        name="moe_dispatch",
    )(tab_flat, dest_tiles, h2p)


def _expert_kernel(be_ref, bs_ref, nv_ref, x_ref, w1_ref, b1_ref, w2_ref, b2_ref, y_ref, w1b, w2b, *, d_ff):
    s = pl.program_id(0)
    live = s < nv_ref[0]
    fresh = (s == 0) | (be_ref[s] != be_ref[jnp.maximum(s - 1, 0)])

    @pl.when(live & fresh)
    def _():
        w1b[...] = w1_ref[0].astype(BF16)
        w2b[...] = w2_ref[0].astype(BF16)

    @pl.when(live)
    def _():
        hc = _dot(x_ref[...].astype(BF16), w1b[...]) + b1_ref[0]
        gt = jnp.minimum(hc[:, :d_ff], SWIGLU_LIMIT)
        up = jnp.clip(hc[:, d_ff:], -SWIGLU_LIMIT, SWIGLU_LIMIT)
        glu = gt * jax.nn.sigmoid(SWIGLU_ALPHA * gt)
        act = ((up + 1.0) * glu).astype(BF16)
        y_ref[...] = _dot(act, w2b[...]) + b2_ref[0]

    @pl.when(jnp.logical_not(live))
    def _():
        y_ref[...] = jnp.zeros_like(y_ref)


def _experts(blk_e, blk_src, n_live, xin, w1, b1, w2, b2):
    n_rows, W = xin.shape
    E, D, two_f = w1.shape
    d_ff = two_f // 2
    return pl.pallas_call(
        functools.partial(_expert_kernel, d_ff=d_ff),
        grid_spec=pltpu.PrefetchScalarGridSpec(
            num_scalar_prefetch=3,
            grid=(n_rows // MOE_ROWS,),
            in_specs=[
                pl.BlockSpec((MOE_ROWS, W), lambda s, be, bs, nv: (bs[s], 0)),
                pl.BlockSpec((1, D, two_f), lambda s, be, bs, nv: (be[s], 0, 0)),
                pl.BlockSpec((1, 1, two_f), lambda s, be, bs, nv: (be[s], 0, 0)),
                pl.BlockSpec((1, d_ff, D), lambda s, be, bs, nv: (be[s], 0, 0)),
                pl.BlockSpec((1, 1, D), lambda s, be, bs, nv: (be[s], 0, 0)),
            ],
            out_specs=pl.BlockSpec((MOE_ROWS, W), lambda s, be, bs, nv: (s, 0)),
            scratch_shapes=[pltpu.VMEM((D, two_f), BF16), pltpu.VMEM((d_ff, D), BF16)],
        ),
        out_shape=jax.ShapeDtypeStruct((n_rows, W), F32),
        compiler_params=_params("arbitrary"),
        name="moe_experts",
    )(blk_e, blk_src, n_live, xin, w1, b1, w2, b2)


def _combine_kernel(dest_hbm, x1_ref, rw_ref, gf_ref, y_hbm, o_ref, idx_smem, ybuf, gsem, isem):
    i = pl.program_id(0)
    n = pl.num_programs(0)
    tm = x1_ref.shape[0]

    def idx_copy(j):
        return pltpu.make_async_copy(dest_hbm.at[j], idx_smem.at[pl.ds((j % 2) * (TOP_K * tm), TOP_K * tm)],
                                     isem.at[j % 2])

    def issue_into(slot):
        def issue(t, _):
            for k in range(TOP_K):
                d = idx_smem[slot * (TOP_K * tm) + k * tm + t]
                pltpu.make_async_copy(y_hbm.at[pl.ds(d, 1)], ybuf.at[slot, k, pl.ds(t, 1)],
                                      gsem.at[slot]).start(priority=k % 2)
            return 0

        lax.fori_loop(0, tm, issue, 0)

    def issue_gathers(j):
        for slot in range(2):
            @pl.when(j % 2 == slot)
            def _():
                issue_into(slot)

    @pl.when(i == 0)
    def _():
        idx_copy(0).start()
        idx_copy(0).wait()
        issue_into(0)

        @pl.when(n > 1)
        def _():
            idx_copy(1).start()

    @pl.when(i + 1 < n)
    def _():
        idx_copy(i + 1).wait()
        issue_gathers(i + 1)

        @pl.when(i + 2 < n)
        def _():
            idx_copy(i + 2).start()

    slot = i % 2
    for k in range(TOP_K):
        pltpu.make_async_copy(y_hbm.at[pl.ds(0, tm)], ybuf.at[slot, k], gsem.at[slot]).wait()

    rw = rw_ref[...]
    x2 = x1_ref[...]
    for k in range(TOP_K):
        x2 = x2 + rw[:, k:k + 1] * ybuf[slot, k]
    o_ref[...] = _rms(x2, gf_ref[...])


def _combine(dest_tiles, x1, rw, gf, y, tm):
    T, D = x1.shape
    W = y.shape[1]
    return pl.pallas_call(
        _combine_kernel,
        grid=(T // tm,),
        in_specs=[
            pl.BlockSpec(memory_space=pl.ANY),
            pl.BlockSpec((tm, D), lambda i: (i, 0)),
            pl.BlockSpec((tm, LANES), lambda i: (i, 0)),
            pl.BlockSpec((1, D), lambda i: (0, 0)),
            pl.BlockSpec(memory_space=pl.ANY),
        ],
        out_specs=pl.BlockSpec((tm, D), lambda i: (i, 0)),
        out_shape=jax.ShapeDtypeStruct((T, D), F32),
        scratch_shapes=[
            pltpu.SMEM((2 * TOP_K * tm,), I32),
            pltpu.VMEM((2, TOP_K, tm, W), F32),
            pltpu.SemaphoreType.DMA((2,)), pltpu.SemaphoreType.DMA((2,)),
        ],
        compiler_params=_params("arbitrary"),
        name="moe_combine",
    )(dest_tiles, x1, rw, gf, y)


def _rope_tables(positions):
    half = ROT_DIM // 2
    freqs = ROPE_THETA ** (-jnp.arange(0, ROT_DIM, 2, dtype=F32) / ROT_DIM)
    ang = positions.astype(F32)[..., None] * freqs
    cos, sin = jnp.cos(ang), jnp.sin(ang)
    rest = HEAD_DIM - ROT_DIM
    ones, zeros = jnp.ones_like(cos[..., :1]), jnp.zeros_like(cos[..., :1])
    fill = lambda v, n: jnp.broadcast_to(v, v.shape[:-1] + (n,))
    per_head = lambda parts: jnp.tile(jnp.concatenate(parts, axis=-1), (1, 1, LANES // HEAD_DIM))
    c = per_head([cos, cos, fill(ones, rest)])
    s1 = per_head([-sin, fill(zeros, half + rest)])
    s2 = per_head([fill(zeros, half), sin, fill(zeros, rest)])
    return c, s1, s2


def _arrange_w_in(w_in, d_model):
    nq, nkv, ng = NSA_HEADS * HEAD_DIM, NSA_KV_GROUPS * HEAD_DIM, NSA_HEADS * 3
    fw = FOX_HEADS * HEAD_DIM
    sizes = [nq] + [nkv] * 6 + [ng, fw, fw, fw, FOX_HEADS, d_model, d_model]
    offs = np.concatenate([[0], np.cumsum(sizes)])
    piece = lambda k: w_in[:, offs[k]:offs[k + 1]]
    gates, ff = piece(7), piece(11)
    per_group = HEADS_PER_GROUP * 3
    zeros = lambda n: jnp.zeros((w_in.shape[0], n), w_in.dtype)
    misc0 = jnp.concatenate([gates[:, :per_group], zeros(FGATE_LANE - per_group), ff,
                             zeros(LANES - FGATE_LANE - FOX_HEADS)], axis=1)
    misc1 = jnp.concatenate([gates[:, per_group:], zeros(LANES - per_group)], axis=1)
    cols = [piece(k) for k in range(7)] + [piece(8), piece(9), piece(10), piece(12), piece(13), misc0, misc1]
    return jnp.concatenate(cols, axis=1).astype(BF16)


def _split_bf16(a):
    hi = a.astype(BF16)
    return hi, (a - hi.astype(F32)).astype(BF16)


def kernel(x, positions, norm1_g, w_in, cmp_pos_emb, cmp_w1, cmp_w2, fox_f_bias, w_proj_nsa, w_proj_fox, w_out,
           norm2_g, router_w, router_b, expert_w1, expert_b1, expert_w2, expert_b2, norm_f_g):
    B, S, D = x.shape
    T = B * S
    assert norm1_g.shape[0] == 1, "the combine kernel fuses the output norm, so it must follow the only layer"
    assert S % KV_CHUNK == 0 or S < KV_CHUNK
    assert S // SEL_BLOCK <= LANES - HEAD_DIM and S % (CMP_STRIDE * SUBLANES) == 0
    tm_proj = min(512, S)
    assert WINDOW % tm_proj == 0 or tm_proj % WINDOW == 0
    tm_tok = min(512, T)
    l = 0

    rope_c, rope_s1, rope_s2 = _rope_tables(positions)
    n_sb = S // SEL_BLOCK
    n_rows16 = S // CMP_STRIDE
    ci = np.arange(n_rows16)[None, :] * CMP_STRIDE
    sj = np.arange(LANES)[:, None] * SEL_BLOCK
    overlap = jnp.asarray(((ci < sj + SEL_BLOCK) & (ci + CMP_BLOCK > sj) & (np.arange(LANES)[:, None] < n_sb)
                           & (np.arange(n_rows16)[None, :] < n_rows16 - 1)).astype(np.float32), BF16)

    w_all = _arrange_w_in(w_in[l], D)
    fb_row = jnp.zeros((1, LANES), F32).at[0, FGATE_LANE:FGATE_LANE + FOX_HEADS].set(fox_f_bias[l].astype(F32))
    (qraw, qrot, kvc, ks, vs, kw, vw, fq, fk, fv, sgn, sgf, misc) = _inproj(
        x, norm1_g[l][None, :], w_all, rope_c, rope_s1, rope_s2, fb_row, tm_proj)
    qrot = qrot.reshape(B, NSA_HEADS, S, LANES)
    fq = fq.reshape(B, FOX_HEADS, S, LANES)
    fv = fv.reshape(B, FOX_HEADS, S, LANES)

    kv_rows = kvc.reshape(B, 2, NSA_KV_GROUPS, n_rows16, CMP_STRIDE * HEAD_DIM)
    pe = cmp_pos_emb[l].reshape(2, 1, CMP_BLOCK * HEAD_DIM).astype(F32)
    cw1 = jnp.pad(cmp_w1[l], ((0, 0), (0, 0), (0, LANES - HEAD_DIM))).astype(BF16)
    cw2 = jnp.pad(cmp_w2[l], ((0, 0), (0, LANES - HEAD_DIM), (0, 0))).astype(BF16)
    cmp_kv = _compress(kv_rows, pe, cw1, cw2)

    front_k = jnp.zeros((B, NSA_KV_GROUPS, WINDOW, LANES), BF16).at[..., HEAD_DIM].set(NEG)
    kw_pad = jnp.concatenate([front_k, kw], axis=2)
    vw_pad = jnp.concatenate([jnp.zeros((B, NSA_KV_GROUPS, WINDOW, LANES), BF16), vw], axis=2)
    o_nsa = _nsa(qraw, qrot, cmp_kv, ks, vs, kw_pad, vw_pad, misc, overlap)
    o_fox = _fox(fq, fk, fv)

    wr = jnp.pad(router_w[l], ((0, 0), (0, LANES - N_EXPERTS)))
    wr_pair = jnp.concatenate(_split_bf16(wr), axis=1)
    br = jnp.pad(router_b[l], (0, LANES - N_EXPERTS))[None, :].astype(F32)
    x1, h2, ri, rw, cnt = _merge(
        x.reshape(T, D), o_nsa.reshape(T, -1), o_fox.reshape(T, -1), sgn.reshape(T, D), sgf.reshape(T, D),
        w_proj_nsa[l].astype(BF16), w_proj_fox[l].astype(BF16), w_out[l].astype(BF16),
        norm2_g[l][None, :], wr_pair, br, tm_tok)

    dest, tab = _dest(ri, cnt, tm_tok)
    dest_tiles = dest[:, :TOP_K, :].reshape(T // tm_tok, TOP_K * tm_tok)
    pend = tab[3, :N_EXPERTS]
    n_blocks = (T * TOP_K) // MOE_ROWS + N_EXPERTS
    n_live = (pend[N_EXPERTS - 1] // MOE_ROWS).astype(I32)
    blk_src = jnp.minimum(jnp.arange(n_blocks, dtype=I32), n_live - 1)
    blk_e = jnp.sum((pend[None, :] <= (blk_src * MOE_ROWS)[:, None]).astype(I32), axis=1)
    blk_e = jnp.minimum(blk_e, N_EXPERTS - 1)

    xin = _dispatch(tab[:3].reshape(-1), dest_tiles, h2,
                    n_blocks * MOE_ROWS, tm_tok)
    y = _experts(blk_e, blk_src, n_live[None], xin, expert_w1[l], expert_b1[l][:, None, :],
                 expert_w2[l], expert_b2[l][:, None, :])
    out = _combine(dest_tiles, x1, rw, norm_f_g[None, :], y, tm_tok)
    return out.reshape(B, S, D)
```

```python
import functools

import numpy as np
import jax
import jax.numpy as jnp
from jax import lax
from jax.experimental import pallas as pl
from jax.experimental.pallas import tpu as pltpu

HEAD_DIM = 64
NSA_HEADS = 8
NSA_KV_GROUPS = 2
HEADS_PER_GROUP = NSA_HEADS // NSA_KV_GROUPS
FOX_HEADS = 8
ROT_DIM = HEAD_DIM // 4
ROPE_THETA = 500000.0
CMP_BLOCK = 32
CMP_STRIDE = 16
SEL_BLOCK = 64
N_SEL = 8
WINDOW = 512
N_EXPERTS = 32
TOP_K = 4
SWIGLU_ALPHA = 1.702
SWIGLU_LIMIT = 7.0
NORM_EPS = 1e-6
NEG = -1e30
BELOW_NEG = -3e38
FORCE_BONUS = 1e4

LANES = 128
SUBLANES = 8
NSA_Q_TILE = 256
KV_CHUNK = 512
FOX_TILE = 512
MOE_ROWS = 512
FGATE_LANE = 32
VMEM_LIMIT = 56 * 1024 * 1024

F32 = jnp.float32
BF16 = jnp.bfloat16
I32 = jnp.int32


def _dot(a, b):
    return jnp.dot(a, b, preferred_element_type=F32)


def _dot_nt(a, b):
    return lax.dot_general(a, b, (((1,), (1,)), ((), ())), preferred_element_type=F32)


def _params(*sem):
    return pltpu.CompilerParams(dimension_semantics=sem, vmem_limit_bytes=VMEM_LIMIT)


def _rms(x, g):
    return x * lax.rsqrt(jnp.mean(x * x, axis=-1, keepdims=True) + NORM_EPS) * g


def _split3(a):
    t1 = a.astype(BF16)
    r1 = a - t1.astype(F32)
    t2 = r1.astype(BF16)
    t3 = (r1 - t2.astype(F32)).astype(BF16)
    return t1, t2, t3


def _inproj_kernel(x_ref, g_ref, w_ref, cs_ref, spread_ref, fb_ref,
                   qraw_ref, qrot_ref, kvc_ref, ks_ref, vs_ref, kw_ref, vw_ref,
                   fq_ref, fk_ref, fv_ref, sgn_ref, sgf_ref, misc_ref, csum_scr, *, d_model):
    scale = HEAD_DIM ** -0.5
    i = pl.program_id(1)
    tm = x_ref.shape[1]
    xn = _rms(x_ref[0], g_ref[...]).astype(BF16)
    lane = lax.broadcasted_iota(I32, (tm, LANES), 1)
    tabs = _dot(cs_ref[0], spread_ref[...])
    rc = tabs[:, :LANES] + jnp.where((lane & (HEAD_DIM - 1)) >= ROT_DIM, 1.0, 0.0)
    rs1, rs2 = tabs[:, LANES:2 * LANES], tabs[:, 2 * LANES:]
    low = lane < HEAD_DIM
    one_at_64 = jnp.where(lane == HEAD_DIM, 1.0, 0.0)

    def rope(slab):
        half = ROT_DIM // 2
        return slab * rc + pltpu.roll(slab, LANES - half, 1) * rs1 + pltpu.roll(slab, half, 1) * rs2

    def put_pair(ref, lead, slab, tail):
        ref[lead + (0,)] = jnp.where(low, slab, tail).astype(BF16)
        ref[lead + (1,)] = jnp.where(low, pltpu.roll(slab, HEAD_DIM, 1), tail).astype(BF16)

    misc_off = w_ref.shape[1] - 2 * LANES
    r = _dot(xn, w_ref[:, misc_off:])
    m0 = r[:, :LANES]
    z = m0 + fb_ref[...]
    log_f = jnp.minimum(z, 0.0) - jnp.log(1.0 + jnp.exp(-jnp.abs(z)))
    misc_ref[0, 0] = jax.nn.sigmoid(m0)
    misc_ref[0, 1] = jax.nn.sigmoid(r[:, LANES:])

    @pl.when(i == 0)
    def _():
        csum_scr[...] = jnp.zeros_like(csum_scr)

    in_gate = (lane >= FGATE_LANE) & (lane < FGATE_LANE + FOX_HEADS)
    t1, t2, t3 = _split3(jnp.where(in_gate, log_f, 0.0))
    r_i = lax.broadcasted_iota(I32, (tm, tm), 0)
    c_i = lax.broadcasted_iota(I32, (tm, tm), 1)
    upto = jnp.where(c_i <= r_i, 1.0, 0.0).astype(BF16)
    parts = _dot(upto, jnp.concatenate([t1, t2, t3], axis=1))
    csum = parts[:, :LANES] + parts[:, LANES:2 * LANES] + parts[:, 2 * LANES:] + csum_scr[0:1, :]
    csum_scr[...] = jnp.broadcast_to(csum[tm - 1:tm, :], csum_scr.shape)

    nq = NSA_HEADS * HEAD_DIM
    r = _dot(xn, w_ref[:, 0:nq]) * scale
    for s in range(nq // LANES):
        slab = r[:, LANES * s:LANES * (s + 1)]
        qraw_ref[0, 2 * s] = slab[:, :HEAD_DIM].astype(BF16)
        qraw_ref[0, 2 * s + 1] = slab[:, HEAD_DIM:].astype(BF16)
        put_pair(qrot_ref, (0, s), rope(slab), 0.0)
    off = nq

    r = _dot(xn, w_ref[:, off:off + 6 * LANES])
    for t in range(2):
        slab = r[:, t * LANES:(t + 1) * LANES]
        kvc_ref[0, t, 0] = slab[:, :HEAD_DIM].astype(BF16)
        kvc_ref[0, t, 1] = slab[:, HEAD_DIM:].astype(BF16)
    tok = i * tm + lax.broadcasted_iota(I32, (tm, LANES), 0)
    block_hot = jnp.where(lane - HEAD_DIM == (tok >> (SEL_BLOCK.bit_length() - 1)), 1.0, 0.0)
    put_pair(ks_ref, (0,), rope(r[:, 2 * LANES:3 * LANES]), block_hot)
    put_pair(vs_ref, (0,), r[:, 3 * LANES:4 * LANES], one_at_64)
    put_pair(kw_ref, (0,), rope(r[:, 4 * LANES:5 * LANES]), 0.0)
    put_pair(vw_ref, (0,), r[:, 5 * LANES:6 * LANES], one_at_64)
    off += 6 * LANES

    fw = FOX_HEADS * HEAD_DIM
    ones3 = jnp.where((lane >= HEAD_DIM) & (lane < HEAD_DIM + 3), 1.0, 0.0)
    r = _dot(xn, w_ref[:, off:off + fw]) * scale
    for s in range(fw // LANES):
        put_pair(fq_ref, (0, s), r[:, LANES * s:LANES * (s + 1)], ones3)
    off += fw
    r = _dot(xn, w_ref[:, off:off + fw])
    for s in range(fw // LANES):
        slab = r[:, LANES * s:LANES * (s + 1)]
        for par in range(2):
            h = 2 * s + par
            neg_c = -csum[:, FGATE_LANE + h:FGATE_LANE + h + 1]
            c1, c2, c3 = _split3(neg_c)
            tail = jnp.where(lane == HEAD_DIM, c1.astype(F32), jnp.where(
                lane == HEAD_DIM + 1, c2.astype(F32), jnp.where(lane == HEAD_DIM + 2, c3.astype(F32), 0.0)))
            head = slab if par == 0 else pltpu.roll(slab, HEAD_DIM, 1)
            fk_ref[0, h] = jnp.where(low, head, tail).astype(BF16)
    off += fw
    r = _dot(xn, w_ref[:, off:off + fw])
    for s in range(fw // LANES):
        put_pair(fv_ref, (0, s), r[:, LANES * s:LANES * (s + 1)], one_at_64)
    off += fw

    step = min(512, d_model)
    for ref in (sgn_ref, sgf_ref):
        for c in range(0, d_model, step):
            r = _dot(xn, w_ref[:, off + c:off + c + step])
            ref[0, :, c:c + step] = jax.nn.sigmoid(r).astype(BF16)
        off += d_model


def _inproj(x, norm_g, w_all, rope_cs, rope_spread, f_bias_row, tm):
    B, S, D = x.shape
    ncol = w_all.shape[1]
    G = NSA_KV_GROUPS
    wide = lambda heads: jax.ShapeDtypeStruct((B, heads, S, LANES), BF16)
    pair_shape = lambda heads: jax.ShapeDtypeStruct((B, heads // 2, 2, S, LANES), BF16)
    pair_spec = lambda heads: pl.BlockSpec((1, heads // 2, 2, tm, LANES), lambda b, i: (b, 0, 0, i, 0))
    kvspec = pl.BlockSpec((1, G, tm, LANES), lambda b, i: (b, 0, i, 0))
    gspec = pl.BlockSpec((1, tm, D), lambda b, i: (b, i, 0))
    return pl.pallas_call(
        functools.partial(_inproj_kernel, d_model=D),
        grid=(B, S // tm),
        in_specs=[
            pl.BlockSpec((1, tm, D), lambda b, i: (b, i, 0)),
            pl.BlockSpec((1, D), lambda b, i: (0, 0)),
            pl.BlockSpec((D, ncol), lambda b, i: (0, 0)),
            pl.BlockSpec((1, tm, rope_cs.shape[2]), lambda b, i: (b, i, 0)),
            pl.BlockSpec(rope_spread.shape, lambda b, i: (0, 0)),
            pl.BlockSpec((1, LANES), lambda b, i: (0, 0)),
        ],
        out_specs=[
            pl.BlockSpec((1, NSA_HEADS, tm, HEAD_DIM), lambda b, i: (b, 0, i, 0)),
            pair_spec(NSA_HEADS),
            pl.BlockSpec((1, 2, G, tm, HEAD_DIM), lambda b, i: (b, 0, 0, i, 0)),
            kvspec, kvspec, kvspec, kvspec,
            pair_spec(FOX_HEADS),
            pl.BlockSpec((1, FOX_HEADS, tm, LANES), lambda b, i: (b, 0, i, 0)),
            pair_spec(FOX_HEADS),
            gspec, gspec,
            pl.BlockSpec((1, 2, tm, LANES), lambda b, i: (b, 0, i, 0)),
        ],
        out_shape=[
            jax.ShapeDtypeStruct((B, NSA_HEADS, S, HEAD_DIM), BF16),
            pair_shape(NSA_HEADS),
            jax.ShapeDtypeStruct((B, 2, G, S, HEAD_DIM), BF16),
            wide(G), wide(G), wide(G), wide(G),
            pair_shape(FOX_HEADS), wide(FOX_HEADS), pair_shape(FOX_HEADS),
            jax.ShapeDtypeStruct((B, S, D), BF16), jax.ShapeDtypeStruct((B, S, D), BF16),
            jax.ShapeDtypeStruct((B, 2, S, LANES), F32),
        ],
        scratch_shapes=[pltpu.VMEM((SUBLANES, LANES), F32)],
        compiler_params=_params("parallel", "arbitrary"),
        name="inproj",
    )(x, norm_g, w_all, rope_cs, rope_spread, f_bias_row)


def _compress_kernel(r_ref, pe_ref, w1_ref, w2_ref, o_ref):
    half = CMP_STRIDE * HEAD_DIM
    rows = r_ref[0, 0, 0].astype(F32)
    lo = (rows + pe_ref[0, :, :half]).astype(BF16)
    hi = (rows + pe_ref[0, :, half:]).astype(BF16)
    a = _dot(lo, w1_ref[0, :half])
    b = _dot(hi, w1_ref[0, half:])
    n = a.shape[0]
    pre = a + pltpu.roll(b, n - 1, 0)
    act = jax.nn.gelu(pre, approximate=True).astype(BF16)
    o_ref[0, 0, 0] = _dot(act, w2_ref[0]).astype(BF16)


def _compress(kv_rows, pe, w1, w2):
    B, _, G, n_rows, width = kv_rows.shape
    return pl.pallas_call(
        _compress_kernel,
        grid=(B, 2, G),
        in_specs=[
            pl.BlockSpec((1, 1, 1, n_rows, width), lambda b, t, g: (b, t, g, 0, 0)),
            pl.BlockSpec((1, 1, 2 * width), lambda b, t, g: (t, 0, 0)),
            pl.BlockSpec((1, 2 * width, LANES), lambda b, t, g: (t, 0, 0)),
            pl.BlockSpec((1, LANES, HEAD_DIM), lambda b, t, g: (t, 0, 0)),
        ],
        out_specs=pl.BlockSpec((1, 1, 1, n_rows, HEAD_DIM), lambda b, t, g: (b, t, g, 0, 0)),
        out_shape=jax.ShapeDtypeStruct((B, 2, G, n_rows, HEAD_DIM), BF16),
        compiler_params=_params("parallel", "parallel", "parallel"),
        name="nsa_compress",
    )(kv_rows, pe, w1, w2)


def _softmax_rows(logits, ok):
    l = jnp.where(ok, logits, NEG)
    m = jnp.max(l, axis=-1, keepdims=True)
    e = jnp.where(ok, jnp.exp(l - m), 0.0)
    s = jnp.sum(e, axis=-1, keepdims=True)
    return e / jnp.where(s > 0.0, s, 1.0)


def _nsa_kernel(qraw_ref, qrot_ref, kcmp_ref, vcmp_ref, ks_ref, vs_ref, kw_ref, vw_ref,
                misc_ref, overlap_ref, o_ref, *, seq, chunk):
    qi = pl.program_id(2)
    n_full = qi * NSA_Q_TILE // chunk
    n_front = WINDOW // NSA_Q_TILE
    args = (qraw_ref, qrot_ref, kcmp_ref, vcmp_ref, ks_ref, vs_ref, kw_ref, vw_ref, misc_ref, overlap_ref, o_ref)
    for front in range(n_front):
        @pl.when(qi == front)
        def _():
            _nsa_tile(*args, front, front * NSA_Q_TILE // chunk, front=True, seq=seq, chunk=chunk)

    for last in range(n_front * NSA_Q_TILE // chunk, seq // chunk):
        @pl.when((n_full == last) & (qi >= n_front))
        def _():
            _nsa_tile(*args, qi, last, front=False, seq=seq, chunk=chunk)


def _nsa_tile(qraw_ref, qrot_ref, kcmp_ref, vcmp_ref, ks_ref, vs_ref, kw_ref, vw_ref,
              misc_ref, overlap_ref, o_ref, qi, last, *, front, seq, chunk):
    tq_n = NSA_Q_TILE
    hpg = HEADS_PER_GROUP
    rows_n = hpg * tq_n
    n_sb = seq // SEL_BLOCK
    n_sel = min(N_SEL, n_sb)
    t0 = qi * tq_n
    tq = t0 + lax.broadcasted_iota(I32, (tq_n, 1), 0)
    q_win = qrot_ref[0].reshape(rows_n, LANES)
    qrot = q_win.astype(F32)
    lane_r = lax.broadcasted_iota(I32, (rows_n, LANES), 1)
    row_in_tile = lax.broadcasted_iota(I32, (rows_n, 1), 0) & (tq_n - 1)

    col = lax.broadcasted_iota(I32, (rows_n, tq_n), 1)
    if front:
        wlen = t0 + tq_n
        kw, vw = kw_ref[0, 0, 0:wlen, :], vw_ref[0, 0, 0:wlen, :]
        sw = _dot_nt(q_win, kw)
        parts = [sw[:, :t0]] if t0 else []
    else:
        wlen = WINDOW + tq_n
        w0 = pl.multiple_of(t0 - WINDOW, tq_n)
        kw, vw = kw_ref[0, 0, pl.ds(w0, wlen), :], vw_ref[0, 0, pl.ds(w0, wlen), :]
        sw = _dot_nt(q_win, kw)
        parts = [jnp.where(col > row_in_tile, sw[:, :tq_n], NEG)]
        if WINDOW > tq_n:
            parts.append(sw[:, tq_n:WINDOW])
    parts.append(jnp.where(col <= row_in_tile, sw[:, wlen - tq_n:], NEG))
    sw = jnp.concatenate(parts, axis=1)
    mw = jnp.broadcast_to(jnp.max(sw, axis=-1, keepdims=True), (rows_n, LANES))
    acc_w = _dot(jnp.exp(sw - jnp.tile(mw, (1, wlen // LANES))).astype(BF16), vw)

    q = qraw_ref[0].reshape(rows_n, HEAD_DIM)
    n_c = kcmp_ref.shape[3]
    lg = _dot_nt(q, kcmp_ref[0, 0, 0])
    cidx = lax.broadcasted_iota(I32, (tq_n, n_c), 1)
    cmask = cidx * CMP_STRIDE + (CMP_BLOCK - 1) <= tq
    vcmp = vcmp_ref[0, 0, 0]
    o_c = []
    p_sum = jnp.zeros((tq_n, n_c), F32)
    for h in range(hpg):
        p = _softmax_rows(lg[h * tq_n:(h + 1) * tq_n], cmask)
        p_sum = p_sum + p
        o_c.append(_dot(p.astype(BF16), vcmp))

    p_hi = p_sum.astype(BF16)
    p_lo = (p_sum - p_hi.astype(F32)).astype(BF16)
    imp = (_dot_nt(overlap_ref[...], p_hi) + _dot_nt(overlap_ref[...], p_lo))[:n_sb]
    j = lax.broadcasted_iota(I32, (n_sb, tq_n), 0)
    t_row = t0 + lax.broadcasted_iota(I32, (n_sb, tq_n), 1)
    cur = t_row >> (SEL_BLOCK.bit_length() - 1)
    forced = jnp.where(j == 0, 1.0, jnp.where(j == cur, 1.0, jnp.where(j == cur - 1, 1.0, 0.0)))
    score = jnp.where(forced > 0.0, imp + FORCE_BONUS, jnp.where(j * SEL_BLOCK <= t_row, imp, NEG))
    beaten = jnp.zeros((n_sb, tq_n), F32)
    for jp in range(n_sb):
        row = score[jp:jp + 1, :]
        wins_ties = jnp.where(row >= score, 1.0, 0.0)
        wins = jnp.where(row > score, 1.0, 0.0)
        beaten = beaten + jnp.where(j > jp, wins_ties, wins)
    penalty = jnp.where(beaten < n_sel, 0.0, NEG)
    penalty = jnp.concatenate([penalty, jnp.zeros((LANES - n_sb, tq_n), F32)], axis=0).T
    penalty = pltpu.roll(penalty, HEAD_DIM, 1)

    q_sel = jnp.where(lane_r < HEAD_DIM, qrot, jnp.concatenate([penalty] * hpg, axis=0)).astype(BF16)

    logits = [_dot_nt(q_sel, ks_ref[0, 0, c * chunk:(c + 1) * chunk, :]) for c in range(last + 1)]
    kpos = last * chunk + lax.broadcasted_iota(I32, (rows_n, chunk), 1)
    logits[last] = jnp.where(kpos <= t0 + row_in_tile, logits[last], NEG)
    m = jnp.max(logits[0], axis=-1, keepdims=True)
    for s in logits[1:]:
        m = jnp.maximum(m, jnp.max(s, axis=-1, keepdims=True))
    m = jnp.tile(jnp.broadcast_to(m, (rows_n, LANES)), (1, chunk // LANES))
    acc_s = _dot(jnp.exp(logits[0] - m).astype(BF16), vs_ref[0, 0, 0:chunk, :])
    for c in range(1, last + 1):
        acc_s = acc_s + _dot(jnp.exp(logits[c] - m).astype(BF16), vs_ref[0, 0, c * chunk:(c + 1) * chunk, :])

    g = misc_ref[0, 0]
    for h in range(hpg):
        a_s = acc_s[h * tq_n:(h + 1) * tq_n]
        a_w = acc_w[h * tq_n:(h + 1) * tq_n]
        o_s = a_s[:, :HEAD_DIM] / a_s[:, HEAD_DIM:HEAD_DIM + 1]
        o_w = a_w[:, :HEAD_DIM] / a_w[:, HEAD_DIM:HEAD_DIM + 1]
        o_h = g[:, 3 * h:3 * h + 1] * o_c[h] + g[:, 3 * h + 1:3 * h + 2] * o_s + g[:, 3 * h + 2:3 * h + 3] * o_w
        o_ref[0, :, HEAD_DIM * h:HEAD_DIM * (h + 1)] = o_h.astype(BF16)


def _nsa(qraw, qrot, cmp_kv, ks, vs, kw, vw, misc, overlap):
    B, _, S, _ = qraw.shape
    G = NSA_KV_GROUPS
    n_c = cmp_kv.shape[3]
    chunk = min(KV_CHUNK, S)
    kvspec = pl.BlockSpec((1, 1, S, LANES), lambda b, g, i: (b, g, 0, 0))
    return pl.pallas_call(
        functools.partial(_nsa_kernel, seq=S, chunk=chunk),
        grid=(B, G, S // NSA_Q_TILE),
        in_specs=[
            pl.BlockSpec((1, HEADS_PER_GROUP, NSA_Q_TILE, HEAD_DIM), lambda b, g, i: (b, g, i, 0)),
            pl.BlockSpec((1, HEADS_PER_GROUP, NSA_Q_TILE, LANES), lambda b, g, i: (b, g, i, 0)),
            pl.BlockSpec((1, 1, 1, n_c, HEAD_DIM), lambda b, g, i: (b, 0, g, 0, 0)),
            pl.BlockSpec((1, 1, 1, n_c, HEAD_DIM), lambda b, g, i: (b, 1, g, 0, 0)),
            kvspec, kvspec, kvspec, kvspec,
            pl.BlockSpec((1, 1, NSA_Q_TILE, LANES), lambda b, g, i: (b, g, i, 0)),
            pl.BlockSpec((LANES, n_c), lambda b, g, i: (0, 0)),
        ],
        out_specs=pl.BlockSpec((1, NSA_Q_TILE, HEADS_PER_GROUP * HEAD_DIM), lambda b, g, i: (b, i, g)),
        out_shape=jax.ShapeDtypeStruct((B, S, NSA_HEADS * HEAD_DIM), BF16),
        compiler_params=_params("parallel", "parallel", "arbitrary"),
        name="nsa_attention",
    )(qraw, qrot, cmp_kv, cmp_kv, ks, vs, kw, vw, misc, overlap)


def _fox_kernel(q_ref, k_ref, v_ref, o_ref):
    tile = q_ref.shape[2]
    qi = pl.program_id(1)
    for last in range(k_ref.shape[2] // tile):
        @pl.when(qi == last)
        def _():
            _fox_tile(q_ref, k_ref, v_ref, o_ref, last, tile)


def _fox_tile(q_ref, k_ref, v_ref, o_ref, last, tile):
    causal = (lax.broadcasted_iota(I32, (tile, tile), 1) <= lax.broadcasted_iota(I32, (tile, tile), 0))
    for h in range(FOX_HEADS):
        q = q_ref[0, h]
        logits = [_dot_nt(q, k_ref[0, h, c * tile:(c + 1) * tile, :]) for c in range(last + 1)]
        logits[last] = jnp.where(causal, logits[last], NEG)
        m = jnp.max(logits[0], axis=-1, keepdims=True)
        for s in logits[1:]:
            m = jnp.maximum(m, jnp.max(s, axis=-1, keepdims=True))
        m = jnp.tile(jnp.broadcast_to(m, (tile, LANES)), (1, tile // LANES))
        acc = _dot(jnp.exp(logits[0] - m).astype(BF16), v_ref[0, h, 0:tile, :])
        for c in range(1, last + 1):
            acc = acc + _dot(jnp.exp(logits[c] - m).astype(BF16), v_ref[0, h, c * tile:(c + 1) * tile, :])
        o_ref[0, :, HEAD_DIM * h:HEAD_DIM * (h + 1)] = (acc[:, :HEAD_DIM] / acc[:, HEAD_DIM:HEAD_DIM + 1]).astype(BF16)


def _fox(fq, fk, fv):
    B, H, S, _ = fq.shape
    tile = min(FOX_TILE, S)
    return pl.pallas_call(
        _fox_kernel,
        grid=(B, S // tile),
        in_specs=[
            pl.BlockSpec((1, H, tile, LANES), lambda b, i: (b, 0, i, 0)),
            pl.BlockSpec((1, H, S, LANES), lambda b, i: (b, 0, 0, 0)),
            pl.BlockSpec((1, H, S, LANES), lambda b, i: (b, 0, 0, 0)),
        ],
        out_specs=pl.BlockSpec((1, tile, H * HEAD_DIM), lambda b, i: (b, i, 0)),
        out_shape=jax.ShapeDtypeStruct((B, S, H * HEAD_DIM), BF16),
        compiler_params=_params("parallel", "arbitrary"),
        name="fox_attention",
    )(fq, fk, fv)


def _merge_kernel(x_ref, on_ref, of_ref, sgn_ref, sgf_ref, wn_ref, wf_ref, wo_ref, g2_ref,
                  wr_ref, br_ref, x1_ref, h2_ref, ri_ref, rw_ref, cnt_ref, carry_scr):
    i = pl.program_id(0)
    tm = x_ref.shape[0]

    @pl.when(i == 0)
    def _():
        carry_scr[...] = jnp.zeros_like(carry_scr)

    a = _dot(on_ref[...], wn_ref[...])
    b = _dot(of_ref[...], wf_ref[...])
    mixed = (sgn_ref[...].astype(F32) * a + sgf_ref[...].astype(F32) * b).astype(BF16)
    x1 = x_ref[...] + _dot(mixed, wo_ref[...])
    x1_ref[...] = x1
    h2 = _rms(x1, g2_ref[...])
    h2_ref[...] = h2

    hh = h2.astype(BF16)
    hl = (h2 - hh.astype(F32)).astype(BF16)
    wr_hi, wr_lo = wr_ref[:, :LANES], wr_ref[:, LANES:]
    logits = _dot(hh, wr_hi) + _dot(hl, wr_hi) + _dot(hh, wr_lo) + br_ref[...]
    lane = lax.broadcasted_iota(I32, (tm, LANES), 1)
    lane_f = lane.astype(F32)
    l = jnp.where(lane < N_EXPERTS, logits, BELOW_NEG)
    idxs, vals, hots = [], [], []
    for _ in range(TOP_K):
        m = jnp.max(l, axis=-1, keepdims=True)
        idx = jnp.min(jnp.where(l == m, lane_f, float(LANES)), axis=-1, keepdims=True)
        hot = lane_f == idx
        idxs.append(idx)
        vals.append(m)
        hots.append(hot)
        l = jnp.where(hot, BELOW_NEG, l)
    exps = [jnp.exp(v - vals[0]) for v in vals]
    den = exps[0]
    for e in exps[1:]:
        den = den + e

    chosen = jnp.zeros((tm, LANES), F32)
    for hot in hots:
        chosen = chosen + jnp.where(hot, 1.0, 0.0)
    r_i = lax.broadcasted_iota(I32, (tm, tm), 0)
    c_i = lax.broadcasted_iota(I32, (tm, tm), 1)
    earlier = jnp.where(c_i < r_i, 1.0, 0.0).astype(BF16)
    before = _dot(earlier, chosen.astype(BF16)) + carry_scr[0:1, :]
    carry_scr[...] = carry_scr[...] + jnp.sum(chosen, axis=0, keepdims=True)
    cnt_ref[...] = carry_scr[...]

    ri = jnp.zeros((tm, LANES), F32)
    rw = jnp.zeros((tm, LANES), F32)
    for k in range(TOP_K):
        rank = jnp.sum(jnp.where(hots[k], before, 0.0), axis=-1, keepdims=True)
        ri = jnp.where(lane == k, idxs[k], ri)
        ri = jnp.where(lane == TOP_K + k, rank, ri)
        rw = jnp.where(lane == k, exps[k] / den, rw)
    ri_ref[...] = ri.astype(I32)
    rw_ref[...] = rw


def _merge(x2d, o_nsa, o_fox, sgn, sgf, wn, wf, wo, g2, wr_pair, br, tm):
    T, D = x2d.shape
    wq = o_nsa.shape[1]
    row = lambda w: pl.BlockSpec((tm, w), lambda i: (i, 0))
    full = lambda a: pl.BlockSpec(a.shape, lambda i: (0,) * a.ndim)
    return pl.pallas_call(
        _merge_kernel,
        grid=(T // tm,),
        in_specs=[row(D), row(wq), row(wq), row(D), row(D), full(wn), full(wf), full(wo), full(g2),
                  full(wr_pair), full(br)],
        out_specs=[row(D), row(D), row(LANES), row(LANES), pl.BlockSpec((SUBLANES, LANES), lambda i: (0, 0))],
        out_shape=[
            jax.ShapeDtypeStruct((T, D), F32), jax.ShapeDtypeStruct((T, D), F32),
            jax.ShapeDtypeStruct((T, LANES), I32), jax.ShapeDtypeStruct((T, LANES), F32),
            jax.ShapeDtypeStruct((SUBLANES, LANES), F32),
        ],
        scratch_shapes=[pltpu.VMEM((SUBLANES, LANES), F32)],
        compiler_params=_params("arbitrary"),
        name="merge_router",
    )(x2d, o_nsa, o_fox, sgn, sgf, wn, wf, wo, g2, wr_pair, br)


def _dest_kernel(ri_ref, cnt_ref, dest_ref, tab_ref):
    tm = ri_ref.shape[0]
    shift = MOE_ROWS.bit_length() - 1
    cnt = cnt_ref[...].astype(I32)
    padded = ((cnt + (MOE_ROWS - 1)) >> shift) << shift
    lane8 = lax.broadcasted_iota(I32, (SUBLANES, LANES), 1)
    pend = padded
    sh = 1
    while sh < LANES:
        pend = pend + jnp.where(lane8 >= sh, pltpu.roll(pend, sh, 1), 0)
        sh *= 2
    pstart = pend - padded
    row8 = lax.broadcasted_iota(I32, (SUBLANES, LANES), 0)
    tab_ref[...] = jnp.where(row8 == 0, cnt, jnp.where(row8 == 1, padded, jnp.where(row8 == 2, pstart, pend)))

    ri = ri_ref[...]
    lane = lax.broadcasted_iota(I32, (tm, LANES), 1)
    start_row = pstart[0:1, :].astype(F32)
    dest = jnp.zeros((tm, LANES), F32)
    for k in range(TOP_K):
        hot = lane == ri[:, k:k + 1]
        base = jnp.sum(jnp.where(hot, start_row, 0.0), axis=-1, keepdims=True)
        dest = jnp.where(lane == k, base + ri[:, TOP_K + k:TOP_K + k + 1].astype(F32), dest)
    dest_ref[0] = dest.T[:SUBLANES].astype(I32)


def _dest(ri, cnt, tm):
    T = ri.shape[0]
    return pl.pallas_call(
        _dest_kernel,
        grid=(T // tm,),
        in_specs=[pl.BlockSpec((tm, LANES), lambda i: (i, 0)), pl.BlockSpec((SUBLANES, LANES), lambda i: (0, 0))],
        out_specs=[pl.BlockSpec((1, SUBLANES, tm), lambda i: (i, 0, 0)),
                   pl.BlockSpec((SUBLANES, LANES), lambda i: (0, 0))],
        out_shape=[jax.ShapeDtypeStruct((T // tm, SUBLANES, tm), I32), jax.ShapeDtypeStruct((SUBLANES, LANES), I32)],
        compiler_params=_params("arbitrary"),
        name="moe_dest",
    )(ri, cnt)


def _pad_copies(tab_ref, zbuf, xin_hbm, zsem):
    out = []
    for e in range(N_EXPERTS):
        cnt = tab_ref[e]
        pad = tab_ref[LANES + e] - cnt
        base = tab_ref[2 * LANES + e] + cnt
        piece = 1
        while piece < MOE_ROWS:
            cond = (pad & piece) != 0
            if piece < SUBLANES:
                for r in range(piece):
                    out.append((cond, pltpu.make_async_copy(
                        zbuf.at[pl.ds(0, 1)], xin_hbm.at[pl.ds(base + r, 1)], zsem)))
            else:
                out.append((cond, pltpu.make_async_copy(
                    zbuf.at[pl.ds(0, piece)], xin_hbm.at[pl.ds(pl.multiple_of(base, SUBLANES), piece)], zsem)))
            base = base + jnp.where(cond, piece, 0)
            piece *= 2
    last = N_EXPERTS - 1
    n_live = (tab_ref[2 * LANES + last] + tab_ref[LANES + last]) // MOE_ROWS
    n_blocks = xin_hbm.shape[0] // MOE_ROWS
    for blk in range(n_blocks - N_EXPERTS, n_blocks):
        out.append((blk >= n_live, pltpu.make_async_copy(
            zbuf, xin_hbm.at[pl.ds(blk * MOE_ROWS, MOE_ROWS)], zsem)))
    return out


def _dispatch_kernel(tab_ref, dest_hbm, h2_hbm, xin_hbm, idx_smem, hbuf, zbuf, lsem, ssem, isem, zsem):
    i = pl.program_id(0)
    n = pl.num_programs(0)
    tm = hbuf.shape[1]

    def load(j):
        return (pltpu.make_async_copy(h2_hbm.at[pl.ds(j * tm, tm)], hbuf.at[j % 3], lsem.at[j % 3]),
                pltpu.make_async_copy(dest_hbm.at[j], idx_smem.at[pl.ds((j % 3) * (TOP_K * tm), TOP_K * tm)],
                                      isem.at[j % 3]))

    def wait_scatters(j):
        for _ in range(TOP_K):
            pltpu.make_async_copy(hbuf.at[j % 3], xin_hbm.at[pl.ds(0, tm)], ssem.at[j % 3]).wait()

    @pl.when(i == 0)
    def _():
        for cp in load(0):
            cp.start()
        zbuf[...] = jnp.zeros_like(zbuf)
        for cond, cp in _pad_copies(tab_ref, zbuf, xin_hbm, zsem):
            @pl.when(cond)
            def _():
                cp.start()

    @pl.when(i + 1 < n)
    def _():
        for cp in load(i + 1):
            cp.start()

    for cp in load(i):
        cp.wait()
    for slot in range(3):
        @pl.when(i % 3 == slot)
        def _():
            def issue(t, _):
                for k in range(TOP_K):
                    d = idx_smem[slot * (TOP_K * tm) + k * tm + t]
                    pltpu.make_async_copy(hbuf.at[slot, pl.ds(t, 1)], xin_hbm.at[pl.ds(d, 1)],
                                          ssem.at[slot]).start(priority=k % 2)
                return 0

            lax.fori_loop(0, tm, issue, 0)

    @pl.when(i > 0)
    def _():
        wait_scatters(i - 1)

    @pl.when(i == n - 1)
    def _():
        wait_scatters(i)
        for cond, cp in _pad_copies(tab_ref, zbuf, xin_hbm, zsem):
            @pl.when(cond)
            def _():
                cp.wait()


def _dispatch(tab_flat, dest_tiles, h2p, n_rows, tm):
    T, W = h2p.shape
    return pl.pallas_call(
        _dispatch_kernel,
        grid_spec=pltpu.PrefetchScalarGridSpec(
            num_scalar_prefetch=1,
            grid=(T // tm,),
            in_specs=[pl.BlockSpec(memory_space=pl.ANY), pl.BlockSpec(memory_space=pl.ANY)],
            out_specs=pl.BlockSpec(memory_space=pl.ANY),
            scratch_shapes=[
                pltpu.SMEM((3 * TOP_K * tm,), I32),
                pltpu.VMEM((3, tm, W), F32),
                pltpu.VMEM((MOE_ROWS, W), F32),
                pltpu.SemaphoreType.DMA((3,)), pltpu.SemaphoreType.DMA((3,)), pltpu.SemaphoreType.DMA((3,)),
                pltpu.SemaphoreType.DMA,
            ],
        ),
        out_shape=jax.ShapeDtypeStruct((n_rows, W), F32),
        compiler_params=_params("arbitrary"),
        name="moe_dispatch",
    )(tab_flat, dest_tiles, h2p)


def _expert_kernel(be_ref, bs_ref, nv_ref, x_ref, w1_ref, b1_ref, w2_ref, b2_ref, y_ref, w1b, w2b, *, d_ff):
    s = pl.program_id(0)
    live = s < nv_ref[0]
    fresh = (s == 0) | (be_ref[s] != be_ref[jnp.maximum(s - 1, 0)])

    @pl.when(live & fresh)
    def _():
        w1b[...] = w1_ref[0].astype(BF16)
        w2b[...] = w2_ref[0].astype(BF16)

    @pl.when(live)
    def _():
        hc = _dot(x_ref[...].astype(BF16), w1b[...]) + b1_ref[0]
        gt = jnp.minimum(hc[:, :d_ff], SWIGLU_LIMIT)
        up = jnp.clip(hc[:, d_ff:], -SWIGLU_LIMIT, SWIGLU_LIMIT)
        glu = gt * jax.nn.sigmoid(SWIGLU_ALPHA * gt)
        act = ((up + 1.0) * glu).astype(BF16)
        y_ref[...] = _dot(act, w2b[...]) + b2_ref[0]

    @pl.when(jnp.logical_not(live))
    def _():
        y_ref[...] = jnp.zeros_like(y_ref)


def _experts(blk_e, blk_src, n_live, xin, w1, b1, w2, b2):
    n_rows, W = xin.shape
    E, D, two_f = w1.shape
    d_ff = two_f // 2
    return pl.pallas_call(
        functools.partial(_expert_kernel, d_ff=d_ff),
        grid_spec=pltpu.PrefetchScalarGridSpec(
            num_scalar_prefetch=3,
            grid=(n_rows // MOE_ROWS,),
            in_specs=[
                pl.BlockSpec((MOE_ROWS, W), lambda s, be, bs, nv: (bs[s], 0)),
                pl.BlockSpec((1, D, two_f), lambda s, be, bs, nv: (be[s], 0, 0)),
                pl.BlockSpec((1, 1, two_f), lambda s, be, bs, nv: (be[s], 0, 0)),
                pl.BlockSpec((1, d_ff, D), lambda s, be, bs, nv: (be[s], 0, 0)),
                pl.BlockSpec((1, 1, D), lambda s, be, bs, nv: (be[s], 0, 0)),
            ],
            out_specs=pl.BlockSpec((MOE_ROWS, W), lambda s, be, bs, nv: (s, 0)),
            scratch_shapes=[pltpu.VMEM((D, two_f), BF16), pltpu.VMEM((d_ff, D), BF16)],
        ),
        out_shape=jax.ShapeDtypeStruct((n_rows, W), F32),
        compiler_params=_params("arbitrary"),
        name="moe_experts",
    )(blk_e, blk_src, n_live, xin, w1, b1, w2, b2)


def _combine_kernel(dest_hbm, x1_ref, rw_ref, gf_ref, y_hbm, o_ref, idx_smem, ybuf, gsem, isem):
    i = pl.program_id(0)
    n = pl.num_programs(0)
    tm = x1_ref.shape[0]

    def idx_copy(j):
        return pltpu.make_async_copy(dest_hbm.at[j], idx_smem.at[pl.ds((j % 2) * (TOP_K * tm), TOP_K * tm)],
                                     isem.at[j % 2])

    def issue_into(slot):
        def issue(t, _):
            for k in range(TOP_K):
                d = idx_smem[slot * (TOP_K * tm) + k * tm + t]
                pltpu.make_async_copy(y_hbm.at[pl.ds(d, 1)], ybuf.at[slot, k, pl.ds(t, 1)],
                                      gsem.at[slot]).start(priority=k % 2)
            return 0

        lax.fori_loop(0, tm, issue, 0)

    def issue_gathers(j):
        for slot in range(2):
            @pl.when(j % 2 == slot)
            def _():
                issue_into(slot)

    @pl.when(i == 0)
    def _():
        idx_copy(0).start()
        idx_copy(0).wait()
        issue_into(0)

        @pl.when(n > 1)
        def _():
            idx_copy(1).start()

    @pl.when(i + 1 < n)
    def _():
        idx_copy(i + 1).wait()
        issue_gathers(i + 1)

        @pl.when(i + 2 < n)
        def _():
            idx_copy(i + 2).start()

    slot = i % 2
    for k in range(TOP_K):
        pltpu.make_async_copy(y_hbm.at[pl.ds(0, tm)], ybuf.at[slot, k], gsem.at[slot]).wait()

    rw = rw_ref[...]
    x2 = x1_ref[...]
    for k in range(TOP_K):
        x2 = x2 + rw[:, k:k + 1] * ybuf[slot, k]
    o_ref[...] = _rms(x2, gf_ref[...])


def _combine(dest_tiles, x1, rw, gf, y, tm):
    T, D = x1.shape
    W = y.shape[1]
    return pl.pallas_call(
        _combine_kernel,
        grid=(T // tm,),
        in_specs=[
            pl.BlockSpec(memory_space=pl.ANY),
            pl.BlockSpec((tm, D), lambda i: (i, 0)),
            pl.BlockSpec((tm, LANES), lambda i: (i, 0)),
            pl.BlockSpec((1, D), lambda i: (0, 0)),
            pl.BlockSpec(memory_space=pl.ANY),
        ],
        out_specs=pl.BlockSpec((tm, D), lambda i: (i, 0)),
        out_shape=jax.ShapeDtypeStruct((T, D), F32),
        scratch_shapes=[
            pltpu.SMEM((2 * TOP_K * tm,), I32),
            pltpu.VMEM((2, TOP_K, tm, W), F32),
            pltpu.SemaphoreType.DMA((2,)), pltpu.SemaphoreType.DMA((2,)),
        ],
        compiler_params=_params("arbitrary"),
        name="moe_combine",
    )(dest_tiles, x1, rw, gf, y)


def _rope_tables(positions):
    half = ROT_DIM // 2
    freqs = ROPE_THETA ** (-jnp.arange(0, ROT_DIM, 2, dtype=F32) / ROT_DIM)
    ang = positions.astype(F32)[..., None] * freqs
    cs = jnp.concatenate([jnp.cos(ang), jnp.sin(ang)], axis=-1)
    cs3 = jnp.concatenate(_split3(cs), axis=-1)
    spread = np.zeros((3 * ROT_DIM, 3 * LANES), np.float32)
    for term in range(3):
        for i in range(half):
            for head0 in range(0, LANES, HEAD_DIM):
                spread[term * ROT_DIM + i, [head0 + i, head0 + half + i]] = 1.0
                spread[term * ROT_DIM + half + i, LANES + head0 + i] = -1.0
                spread[term * ROT_DIM + half + i, 2 * LANES + head0 + half + i] = 1.0
    return cs3, jnp.asarray(spread, BF16)


def _arrange_w_in(w_in, d_model):
    nq, nkv, ng = NSA_HEADS * HEAD_DIM, NSA_KV_GROUPS * HEAD_DIM, NSA_HEADS * 3
    fw = FOX_HEADS * HEAD_DIM
    sizes = [nq] + [nkv] * 6 + [ng, fw, fw, fw, FOX_HEADS, d_model, d_model]
    offs = np.concatenate([[0], np.cumsum(sizes)])
    piece = lambda k: w_in[:, offs[k]:offs[k + 1]]
    gates, ff = piece(7), piece(11)
    per_group = HEADS_PER_GROUP * 3
    zeros = lambda n: jnp.zeros((w_in.shape[0], n), w_in.dtype)
    misc0 = jnp.concatenate([gates[:, :per_group], zeros(FGATE_LANE - per_group), ff,
                             zeros(LANES - FGATE_LANE - FOX_HEADS)], axis=1)
    misc1 = jnp.concatenate([gates[:, per_group:], zeros(LANES - per_group)], axis=1)
    cols = [piece(k) for k in range(7)] + [piece(8), piece(9), piece(10), piece(12), piece(13), misc0, misc1]
    return jnp.concatenate(cols, axis=1).astype(BF16)


def _split_bf16(a):
    hi = a.astype(BF16)
    return hi, (a - hi.astype(F32)).astype(BF16)


def kernel(x, positions, norm1_g, w_in, cmp_pos_emb, cmp_w1, cmp_w2, fox_f_bias, w_proj_nsa, w_proj_fox, w_out,
           norm2_g, router_w, router_b, expert_w1, expert_b1, expert_w2, expert_b2, norm_f_g):
    B, S, D = x.shape
    T = B * S
    assert norm1_g.shape[0] == 1, "the combine kernel fuses the output norm, so it must follow the only layer"
    assert S % KV_CHUNK == 0 or S < KV_CHUNK
    assert S // SEL_BLOCK <= LANES - HEAD_DIM and S % (CMP_STRIDE * SUBLANES) == 0
    tm_proj = min(512, S)
    assert WINDOW % tm_proj == 0 or tm_proj % WINDOW == 0
    tm_tok = min(512, T)
    l = 0

    rope_cs, rope_spread = _rope_tables(positions)
    n_sb = S // SEL_BLOCK
    n_rows16 = S // CMP_STRIDE
    ci = np.arange(n_rows16)[None, :] * CMP_STRIDE
    sj = np.arange(LANES)[:, None] * SEL_BLOCK
    overlap = jnp.asarray(((ci < sj + SEL_BLOCK) & (ci + CMP_BLOCK > sj) & (np.arange(LANES)[:, None] < n_sb)
                           & (np.arange(n_rows16)[None, :] < n_rows16 - 1)).astype(np.float32), BF16)

    w_all = _arrange_w_in(w_in[l], D)
    fb_row = jnp.zeros((1, LANES), F32).at[0, FGATE_LANE:FGATE_LANE + FOX_HEADS].set(fox_f_bias[l].astype(F32))
    (qraw, qrot, kvc, ks, vs, kw, vw, fq, fk, fv, sgn, sgf, misc) = _inproj(
        x, norm1_g[l][None, :], w_all, rope_cs, rope_spread, fb_row, tm_proj)
    qrot = qrot.reshape(B, NSA_HEADS, S, LANES)
    fq = fq.reshape(B, FOX_HEADS, S, LANES)
    fv = fv.reshape(B, FOX_HEADS, S, LANES)

    kv_rows = kvc.reshape(B, 2, NSA_KV_GROUPS, n_rows16, CMP_STRIDE * HEAD_DIM)
    pe = cmp_pos_emb[l].reshape(2, 1, CMP_BLOCK * HEAD_DIM).astype(F32)
    cw1 = jnp.pad(cmp_w1[l], ((0, 0), (0, 0), (0, LANES - HEAD_DIM))).astype(BF16)
    cw2 = jnp.pad(cmp_w2[l], ((0, 0), (0, LANES - HEAD_DIM), (0, 0))).astype(BF16)
    cmp_kv = _compress(kv_rows, pe, cw1, cw2)

    o_nsa = _nsa(qraw, qrot, cmp_kv, ks, vs, kw, vw, misc, overlap)
    o_fox = _fox(fq, fk, fv)

    wr = jnp.pad(router_w[l], ((0, 0), (0, LANES - N_EXPERTS)))
    wr_pair = jnp.concatenate(_split_bf16(wr), axis=1)
    br = jnp.pad(router_b[l], (0, LANES - N_EXPERTS))[None, :].astype(F32)
    x1, h2, ri, rw, cnt = _merge(
        x.reshape(T, D), o_nsa.reshape(T, -1), o_fox.reshape(T, -1), sgn.reshape(T, D), sgf.reshape(T, D),
        w_proj_nsa[l].astype(BF16), w_proj_fox[l].astype(BF16), w_out[l].astype(BF16),
        norm2_g[l][None, :], wr_pair, br, tm_tok)

    dest, tab = _dest(ri, cnt, tm_tok)
    dest_tiles = dest[:, :TOP_K, :].reshape(T // tm_tok, TOP_K * tm_tok)
    pend = tab[3, :N_EXPERTS]
    n_blocks = (T * TOP_K) // MOE_ROWS + N_EXPERTS
    n_live = (pend[N_EXPERTS - 1] // MOE_ROWS).astype(I32)
    blk_src = jnp.minimum(jnp.arange(n_blocks, dtype=I32), n_live - 1)
    blk_e = jnp.sum((pend[None, :] <= (blk_src * MOE_ROWS)[:, None]).astype(I32), axis=1)
    blk_e = jnp.minimum(blk_e, N_EXPERTS - 1)

    xin = _dispatch(tab[:3].reshape(-1), dest_tiles, h2,
                    n_blocks * MOE_ROWS, tm_tok)
    y = _experts(blk_e, blk_src, n_live[None], xin, expert_w1[l], expert_b1[l][:, None, :],
                 expert_w2[l], expert_b2[l][:, None, :])
    out = _combine(dest_tiles, x1, rw, norm_f_g[None, :], y, tm_tok)
    return out.reshape(B, S, D)
```

```python
import functools

import numpy as np
import jax
import jax.numpy as jnp
from jax import lax
from jax.experimental import pallas as pl
from jax.experimental.pallas import tpu as pltpu

HEAD_DIM = 64
NSA_HEADS = 8
NSA_KV_GROUPS = 2
HEADS_PER_GROUP = NSA_HEADS // NSA_KV_GROUPS
FOX_HEADS = 8
ROT_DIM = HEAD_DIM // 4
ROPE_THETA = 500000.0
CMP_BLOCK = 32
CMP_STRIDE = 16
SEL_BLOCK = 64
N_SEL = 8
WINDOW = 512
N_EXPERTS = 32
TOP_K = 4
SWIGLU_ALPHA = 1.702
SWIGLU_LIMIT = 7.0
NORM_EPS = 1e-6
NEG = -1e30
BELOW_NEG = -3e38
FORCE_BONUS = 1e4

LANES = 128
SUBLANES = 8
NSA_Q_TILE = 256
KV_CHUNK = 512
FOX_TILE = 512
MOE_ROWS = 512
FGATE_LANE = 32
VMEM_LIMIT = 56 * 1024 * 1024

F32 = jnp.float32
BF16 = jnp.bfloat16
I32 = jnp.int32


def _dot(a, b):
    return jnp.dot(a, b, preferred_element_type=F32)


def _dot_nt(a, b):
    return lax.dot_general(a, b, (((1,), (1,)), ((), ())), preferred_element_type=F32)


def _params(*sem):
    return pltpu.CompilerParams(dimension_semantics=sem, vmem_limit_bytes=VMEM_LIMIT)


def _rms(x, g):
    return x * lax.rsqrt(jnp.mean(x * x, axis=-1, keepdims=True) + NORM_EPS) * g


def _split3(a):
    t1 = a.astype(BF16)
    r1 = a - t1.astype(F32)
    t2 = r1.astype(BF16)
    t3 = (r1 - t2.astype(F32)).astype(BF16)
    return t1, t2, t3


def _inproj_kernel(x_ref, g_ref, w_ref, cs_ref, spread_ref, fb_ref,
                   qraw_ref, qrot_ref, kvc_ref, ks_ref, vs_ref, kw_ref, vw_ref,
                   fq_ref, fk_ref, fv_ref, sgn_ref, sgf_ref, misc_ref, csum_scr, *, d_model):
    scale = HEAD_DIM ** -0.5
    i = pl.program_id(1)
    tm = x_ref.shape[1]
    xn = _rms(x_ref[0], g_ref[...]).astype(BF16)
    lane = lax.broadcasted_iota(I32, (tm, LANES), 1)
    tabs = _dot(cs_ref[0], spread_ref[...])
    rc = tabs[:, :LANES] + jnp.where((lane & (HEAD_DIM - 1)) >= ROT_DIM, 1.0, 0.0)
    rs1, rs2 = tabs[:, LANES:2 * LANES], tabs[:, 2 * LANES:]
    low = lane < HEAD_DIM
    one_at_64 = jnp.where(lane == HEAD_DIM, 1.0, 0.0)

    def rope(slab):
        half = ROT_DIM // 2
        return slab * rc + pltpu.roll(slab, LANES - half, 1) * rs1 + pltpu.roll(slab, half, 1) * rs2

    def put_pair(ref, lead, slab, tail):
        ref[lead + (0,)] = jnp.where(low, slab, tail).astype(BF16)
        ref[lead + (1,)] = jnp.where(low, pltpu.roll(slab, HEAD_DIM, 1), tail).astype(BF16)

    misc_off = w_ref.shape[1] - 2 * LANES
    r = _dot(xn, w_ref[:, misc_off:])
    m0 = r[:, :LANES]
    z = m0 + fb_ref[...]
    log_f = jnp.minimum(z, 0.0) - jnp.log(1.0 + jnp.exp(-jnp.abs(z)))
    misc_ref[0, 0] = jax.nn.sigmoid(m0)
    misc_ref[0, 1] = jax.nn.sigmoid(r[:, LANES:])

    @pl.when(i == 0)
    def _():
        csum_scr[...] = jnp.zeros_like(csum_scr)

    in_gate = (lane >= FGATE_LANE) & (lane < FGATE_LANE + FOX_HEADS)
    t1, t2, t3 = _split3(jnp.where(in_gate, log_f, 0.0))
    r_i = lax.broadcasted_iota(I32, (tm, tm), 0)
    c_i = lax.broadcasted_iota(I32, (tm, tm), 1)
    upto = jnp.where(c_i <= r_i, 1.0, 0.0).astype(BF16)
    parts = _dot(upto, jnp.concatenate([t1, t2, t3], axis=1))
    csum = parts[:, :LANES] + parts[:, LANES:2 * LANES] + parts[:, 2 * LANES:] + csum_scr[0:1, :]
    csum_scr[...] = jnp.broadcast_to(csum[tm - 1:tm, :], csum_scr.shape)

    nq = NSA_HEADS * HEAD_DIM
    r = _dot(xn, w_ref[:, 0:nq]) * scale
    for s in range(nq // LANES):
        slab = r[:, LANES * s:LANES * (s + 1)]
        qraw_ref[0, 2 * s] = slab[:, :HEAD_DIM].astype(BF16)
        qraw_ref[0, 2 * s + 1] = slab[:, HEAD_DIM:].astype(BF16)
        put_pair(qrot_ref, (0, s), rope(slab), 0.0)
    off = nq

    r = _dot(xn, w_ref[:, off:off + 6 * LANES])
    for t in range(2):
        slab = r[:, t * LANES:(t + 1) * LANES]
        kvc_ref[0, t, 0] = slab[:, :HEAD_DIM].astype(BF16)
        kvc_ref[0, t, 1] = slab[:, HEAD_DIM:].astype(BF16)
    tok = i * tm + lax.broadcasted_iota(I32, (tm, LANES), 0)
    block_hot = jnp.where(lane - HEAD_DIM == (tok >> (SEL_BLOCK.bit_length() - 1)), 1.0, 0.0)
    put_pair(ks_ref, (0,), rope(r[:, 2 * LANES:3 * LANES]), block_hot)
    put_pair(vs_ref, (0,), r[:, 3 * LANES:4 * LANES], one_at_64)
    put_pair(kw_ref, (0,), rope(r[:, 4 * LANES:5 * LANES]), 0.0)
    put_pair(vw_ref, (0,), r[:, 5 * LANES:6 * LANES], one_at_64)
    off += 6 * LANES

    fw = FOX_HEADS * HEAD_DIM
    ones3 = jnp.where((lane >= HEAD_DIM) & (lane < HEAD_DIM + 3), 1.0, 0.0)
    r = _dot(xn, w_ref[:, off:off + fw]) * scale
    for s in range(fw // LANES):
        put_pair(fq_ref, (0, s), r[:, LANES * s:LANES * (s + 1)], ones3)
    off += fw
    r = _dot(xn, w_ref[:, off:off + fw])
    for s in range(fw // LANES):
        slab = r[:, LANES * s:LANES * (s + 1)]
        for par in range(2):
            h = 2 * s + par
            neg_c = -csum[:, FGATE_LANE + h:FGATE_LANE + h + 1]
            c1, c2, c3 = _split3(neg_c)
            tail = jnp.where(lane == HEAD_DIM, c1.astype(F32), jnp.where(
                lane == HEAD_DIM + 1, c2.astype(F32), jnp.where(lane == HEAD_DIM + 2, c3.astype(F32), 0.0)))
            head = slab if par == 0 else pltpu.roll(slab, HEAD_DIM, 1)
            fk_ref[0, h] = jnp.where(low, head, tail).astype(BF16)
    off += fw
    r = _dot(xn, w_ref[:, off:off + fw])
    for s in range(fw // LANES):
        put_pair(fv_ref, (0, s), r[:, LANES * s:LANES * (s + 1)], one_at_64)
    off += fw

    step = min(512, d_model)
    for ref in (sgn_ref, sgf_ref):
        for c in range(0, d_model, step):
            r = _dot(xn, w_ref[:, off + c:off + c + step])
            ref[0, :, c:c + step] = jax.nn.sigmoid(r).astype(BF16)
        off += d_model


def _inproj(x, norm_g, w_all, rope_cs, rope_spread, f_bias_row, tm):
    B, S, D = x.shape
    ncol = w_all.shape[1]
    G = NSA_KV_GROUPS
    wide = lambda heads: jax.ShapeDtypeStruct((B, heads, S, LANES), BF16)
    pair_shape = lambda heads: jax.ShapeDtypeStruct((B, heads // 2, 2, S, LANES), BF16)
    pair_spec = lambda heads: pl.BlockSpec((1, heads // 2, 2, tm, LANES), lambda b, i: (b, 0, 0, i, 0))
    kvspec = pl.BlockSpec((1, G, tm, LANES), lambda b, i: (b, 0, i, 0))
    gspec = pl.BlockSpec((1, tm, D), lambda b, i: (b, i, 0))
    return pl.pallas_call(
        functools.partial(_inproj_kernel, d_model=D),
        grid=(B, S // tm),
        in_specs=[
            pl.BlockSpec((1, tm, D), lambda b, i: (b, i, 0)),
            pl.BlockSpec((1, D), lambda b, i: (0, 0)),
            pl.BlockSpec((D, ncol), lambda b, i: (0, 0)),
            pl.BlockSpec((1, tm, rope_cs.shape[2]), lambda b, i: (b, i, 0)),
            pl.BlockSpec(rope_spread.shape, lambda b, i: (0, 0)),
            pl.BlockSpec((1, LANES), lambda b, i: (0, 0)),
        ],
        out_specs=[
            pl.BlockSpec((1, NSA_HEADS, tm, HEAD_DIM), lambda b, i: (b, 0, i, 0)),
            pair_spec(NSA_HEADS),
            pl.BlockSpec((1, 2, G, tm, HEAD_DIM), lambda b, i: (b, 0, 0, i, 0)),
            kvspec, kvspec, kvspec, kvspec,
            pair_spec(FOX_HEADS),
            pl.BlockSpec((1, FOX_HEADS, tm, LANES), lambda b, i: (b, 0, i, 0)),
            pair_spec(FOX_HEADS),
            gspec, gspec,
            pl.BlockSpec((1, 2, tm, LANES), lambda b, i: (b, 0, i, 0)),
        ],
        out_shape=[
            jax.ShapeDtypeStruct((B, NSA_HEADS, S, HEAD_DIM), BF16),
            pair_shape(NSA_HEADS),
            jax.ShapeDtypeStruct((B, 2, G, S, HEAD_DIM), BF16),
            wide(G), wide(G), wide(G), wide(G),
            pair_shape(FOX_HEADS), wide(FOX_HEADS), pair_shape(FOX_HEADS),
            jax.ShapeDtypeStruct((B, S, D), BF16), jax.ShapeDtypeStruct((B, S, D), BF16),
            jax.ShapeDtypeStruct((B, 2, S, LANES), F32),
        ],
        scratch_shapes=[pltpu.VMEM((SUBLANES, LANES), F32)],
        compiler_params=_params("parallel", "arbitrary"),
        name="inproj",
    )(x, norm_g, w_all, rope_cs, rope_spread, f_bias_row)


def _compress_kernel(r_ref, pe_ref, w1_ref, w2_ref, o_ref):
    half = CMP_STRIDE * HEAD_DIM
    rows = r_ref[0, 0, 0].astype(F32)
    lo = (rows + pe_ref[0, :, :half]).astype(BF16)
    hi = (rows + pe_ref[0, :, half:]).astype(BF16)
    a = _dot(lo, w1_ref[0, :half])
    b = _dot(hi, w1_ref[0, half:])
    n = a.shape[0]
    pre = a + pltpu.roll(b, n - 1, 0)
    act = jax.nn.gelu(pre, approximate=True).astype(BF16)
    o_ref[0, 0, 0] = _dot(act, w2_ref[0]).astype(BF16)


def _compress(kv_rows, pe, w1, w2):
    B, _, G, n_rows, width = kv_rows.shape
    return pl.pallas_call(
        _compress_kernel,
        grid=(B, 2, G),
        in_specs=[
            pl.BlockSpec((1, 1, 1, n_rows, width), lambda b, t, g: (b, t, g, 0, 0)),
            pl.BlockSpec((1, 1, 2 * width), lambda b, t, g: (t, 0, 0)),
            pl.BlockSpec((1, 2 * width, LANES), lambda b, t, g: (t, 0, 0)),
            pl.BlockSpec((1, LANES, HEAD_DIM), lambda b, t, g: (t, 0, 0)),
        ],
        out_specs=pl.BlockSpec((1, 1, 1, n_rows, HEAD_DIM), lambda b, t, g: (b, t, g, 0, 0)),
        out_shape=jax.ShapeDtypeStruct((B, 2, G, n_rows, HEAD_DIM), BF16),
        compiler_params=_params("parallel", "parallel", "parallel"),
        name="nsa_compress",
    )(kv_rows, pe, w1, w2)


def _softmax_rows(logits, ok):
    l = jnp.where(ok, logits, NEG)
    m = jnp.max(l, axis=-1, keepdims=True)
    e = jnp.where(ok, jnp.exp(l - m), 0.0)
    s = jnp.sum(e, axis=-1, keepdims=True)
    return e / jnp.where(s > 0.0, s, 1.0)


def _nsa_kernel(qraw_ref, qrot_ref, kcmp_ref, vcmp_ref, ks_ref, vs_ref, kw_ref, vw_ref,
                misc_ref, overlap_ref, o_ref, *, seq, chunk):
    qi = pl.program_id(2)
    n_full = qi * NSA_Q_TILE // chunk
    n_front = WINDOW // NSA_Q_TILE
    args = (qraw_ref, qrot_ref, kcmp_ref, vcmp_ref, ks_ref, vs_ref, kw_ref, vw_ref, misc_ref, overlap_ref, o_ref)
    for front in range(n_front):
        @pl.when(qi == front)
        def _():
            _nsa_tile(*args, front, front * NSA_Q_TILE // chunk, front=True, seq=seq, chunk=chunk)

    for last in range(n_front * NSA_Q_TILE // chunk, seq // chunk):
        @pl.when((n_full == last) & (qi >= n_front))
        def _():
            _nsa_tile(*args, qi, last, front=False, seq=seq, chunk=chunk)


def _nsa_tile(qraw_ref, qrot_ref, kcmp_ref, vcmp_ref, ks_ref, vs_ref, kw_ref, vw_ref,
              misc_ref, overlap_ref, o_ref, qi, last, *, front, seq, chunk):
    tq_n = NSA_Q_TILE
    hpg = HEADS_PER_GROUP
    rows_n = hpg * tq_n
    n_sb = seq // SEL_BLOCK
    n_sel = min(N_SEL, n_sb)
    t0 = qi * tq_n
    tq = t0 + lax.broadcasted_iota(I32, (tq_n, 1), 0)
    q_win = qrot_ref[0].reshape(rows_n, LANES)
    qrot = q_win.astype(F32)
    lane_r = lax.broadcasted_iota(I32, (rows_n, LANES), 1)
    row_in_tile = lax.broadcasted_iota(I32, (rows_n, 1), 0) & (tq_n - 1)

    col = lax.broadcasted_iota(I32, (rows_n, tq_n), 1)
    if front:
        wlen = t0 + tq_n
        kw, vw = kw_ref[0, 0, 0:wlen, :], vw_ref[0, 0, 0:wlen, :]
        sw = _dot_nt(q_win, kw)
        parts = [sw[:, :t0]] if t0 else []
    else:
        wlen = WINDOW + tq_n
        w0 = pl.multiple_of(t0 - WINDOW, tq_n)
        kw, vw = kw_ref[0, 0, pl.ds(w0, wlen), :], vw_ref[0, 0, pl.ds(w0, wlen), :]
        sw = _dot_nt(q_win, kw)
        parts = [jnp.where(col > row_in_tile, sw[:, :tq_n], NEG)]
        if WINDOW > tq_n:
            parts.append(sw[:, tq_n:WINDOW])
    parts.append(jnp.where(col <= row_in_tile, sw[:, wlen - tq_n:], NEG))
    sw = jnp.concatenate(parts, axis=1)
    mw = jnp.broadcast_to(jnp.max(sw, axis=-1, keepdims=True), (rows_n, LANES))
    acc_w = _dot(jnp.exp(sw - jnp.tile(mw, (1, wlen // LANES))).astype(BF16), vw)

    q = qraw_ref[0].reshape(rows_n, HEAD_DIM)
    n_c = kcmp_ref.shape[3]
    lg = _dot_nt(q, kcmp_ref[0, 0, 0])
    cidx = lax.broadcasted_iota(I32, (tq_n, n_c), 1)
    cmask = cidx * CMP_STRIDE + (CMP_BLOCK - 1) <= tq
    vcmp = vcmp_ref[0, 0, 0]
    o_c = []
    p_sum = jnp.zeros((tq_n, n_c), F32)
    for h in range(hpg):
        p = _softmax_rows(lg[h * tq_n:(h + 1) * tq_n], cmask)
        p_sum = p_sum + p
        o_c.append(_dot(p.astype(BF16), vcmp))

    p_hi = p_sum.astype(BF16)
    p_lo = (p_sum - p_hi.astype(F32)).astype(BF16)
    imp = (_dot_nt(overlap_ref[...], p_hi) + _dot_nt(overlap_ref[...], p_lo))[:n_sb]
    j = lax.broadcasted_iota(I32, (n_sb, tq_n), 0)
    t_row = t0 + lax.broadcasted_iota(I32, (n_sb, tq_n), 1)
    cur = t_row >> (SEL_BLOCK.bit_length() - 1)
    forced = jnp.where(j == 0, 1.0, jnp.where(j == cur, 1.0, jnp.where(j == cur - 1, 1.0, 0.0)))
    score = jnp.where(forced > 0.0, imp + FORCE_BONUS, jnp.where(j * SEL_BLOCK <= t_row, imp, NEG))
    beaten = jnp.zeros((n_sb, tq_n), F32)
    for jp in range(n_sb):
        row = score[jp:jp + 1, :]
        wins_ties = jnp.where(row >= score, 1.0, 0.0)
        wins = jnp.where(row > score, 1.0, 0.0)
        beaten = beaten + jnp.where(j > jp, wins_ties, wins)
    penalty = jnp.where(beaten < n_sel, 0.0, NEG)
    penalty = jnp.concatenate([penalty, jnp.zeros((LANES - n_sb, tq_n), F32)], axis=0).T
    penalty = pltpu.roll(penalty, HEAD_DIM, 1)

    q_sel = jnp.where(lane_r < HEAD_DIM, qrot, jnp.concatenate([penalty] * hpg, axis=0)).astype(BF16)

    logits = [_dot_nt(q_sel, ks_ref[0, 0, c * chunk:(c + 1) * chunk, :]) for c in range(last + 1)]
    kpos = last * chunk + lax.broadcasted_iota(I32, (rows_n, chunk), 1)
    logits[last] = jnp.where(kpos <= t0 + row_in_tile, logits[last], NEG)
    m = jnp.max(logits[0], axis=-1, keepdims=True)
    for s in logits[1:]:
        m = jnp.maximum(m, jnp.max(s, axis=-1, keepdims=True))
    m = jnp.tile(jnp.broadcast_to(m, (rows_n, LANES)), (1, chunk // LANES))
    acc_s = _dot(jnp.exp(logits[0] - m).astype(BF16), vs_ref[0, 0, 0:chunk, :])
    for c in range(1, last + 1):
        acc_s = acc_s + _dot(jnp.exp(logits[c] - m).astype(BF16), vs_ref[0, 0, c * chunk:(c + 1) * chunk, :])

    g = misc_ref[0, 0]
    for h in range(hpg):
        a_s = acc_s[h * tq_n:(h + 1) * tq_n]
        a_w = acc_w[h * tq_n:(h + 1) * tq_n]
        o_s = a_s[:, :HEAD_DIM] / a_s[:, HEAD_DIM:HEAD_DIM + 1]
        o_w = a_w[:, :HEAD_DIM] / a_w[:, HEAD_DIM:HEAD_DIM + 1]
        o_h = g[:, 3 * h:3 * h + 1] * o_c[h] + g[:, 3 * h + 1:3 * h + 2] * o_s + g[:, 3 * h + 2:3 * h + 3] * o_w
        o_ref[0, :, HEAD_DIM * h:HEAD_DIM * (h + 1)] = o_h.astype(BF16)


def _nsa(qraw, qrot, cmp_kv, ks, vs, kw, vw, misc, overlap):
    B, _, S, _ = qraw.shape
    G = NSA_KV_GROUPS
    n_c = cmp_kv.shape[3]
    chunk = min(KV_CHUNK, S)
    kvspec = pl.BlockSpec((1, 1, S, LANES), lambda b, g, i: (b, g, 0, 0))
    return pl.pallas_call(
        functools.partial(_nsa_kernel, seq=S, chunk=chunk),
        grid=(B, G, S // NSA_Q_TILE),
        in_specs=[
            pl.BlockSpec((1, HEADS_PER_GROUP, NSA_Q_TILE, HEAD_DIM), lambda b, g, i: (b, g, i, 0)),
            pl.BlockSpec((1, HEADS_PER_GROUP, NSA_Q_TILE, LANES), lambda b, g, i: (b, g, i, 0)),
            pl.BlockSpec((1, 1, 1, n_c, HEAD_DIM), lambda b, g, i: (b, 0, g, 0, 0)),
            pl.BlockSpec((1, 1, 1, n_c, HEAD_DIM), lambda b, g, i: (b, 1, g, 0, 0)),
            kvspec, kvspec, kvspec, kvspec,
            pl.BlockSpec((1, 1, NSA_Q_TILE, LANES), lambda b, g, i: (b, g, i, 0)),
            pl.BlockSpec((LANES, n_c), lambda b, g, i: (0, 0)),
        ],
        out_specs=pl.BlockSpec((1, NSA_Q_TILE, HEADS_PER_GROUP * HEAD_DIM), lambda b, g, i: (b, i, g)),
        out_shape=jax.ShapeDtypeStruct((B, S, NSA_HEADS * HEAD_DIM), BF16),
        compiler_params=_params("parallel", "parallel", "arbitrary"),
        name="nsa_attention",
    )(qraw, qrot, cmp_kv, cmp_kv, ks, vs, kw, vw, misc, overlap)


def _fox_kernel(q_ref, k_ref, v_ref, o_ref):
    tile = q_ref.shape[2]
    qi = pl.program_id(1)
    for last in range(k_ref.shape[2] // tile):
        @pl.when(qi == last)
        def _():
            _fox_tile(q_ref, k_ref, v_ref, o_ref, last, tile)


def _fox_tile(q_ref, k_ref, v_ref, o_ref, last, tile):
    causal = (lax.broadcasted_iota(I32, (tile, tile), 1) <= lax.broadcasted_iota(I32, (tile, tile), 0))
    for h in range(FOX_HEADS):
        q = q_ref[0, h]
        logits = [_dot_nt(q, k_ref[0, h, c * tile:(c + 1) * tile, :]) for c in range(last + 1)]
        logits[last] = jnp.where(causal, logits[last], NEG)
        m = jnp.max(logits[0], axis=-1, keepdims=True)
        for s in logits[1:]:
            m = jnp.maximum(m, jnp.max(s, axis=-1, keepdims=True))
        m = jnp.tile(jnp.broadcast_to(m, (tile, LANES)), (1, tile // LANES))
        acc = _dot(jnp.exp(logits[0] - m).astype(BF16), v_ref[0, h, 0:tile, :])
        for c in range(1, last + 1):
            acc = acc + _dot(jnp.exp(logits[c] - m).astype(BF16), v_ref[0, h, c * tile:(c + 1) * tile, :])
        o_ref[0, :, HEAD_DIM * h:HEAD_DIM * (h + 1)] = (acc[:, :HEAD_DIM] / acc[:, HEAD_DIM:HEAD_DIM + 1]).astype(BF16)


def _fox(fq, fk, fv):
    B, H, S, _ = fq.shape
    tile = min(FOX_TILE, S)
    return pl.pallas_call(
        _fox_kernel,
        grid=(B, S // tile),
        in_specs=[
            pl.BlockSpec((1, H, tile, LANES), lambda b, i: (b, 0, i, 0)),
            pl.BlockSpec((1, H, S, LANES), lambda b, i: (b, 0, 0, 0)),
            pl.BlockSpec((1, H, S, LANES), lambda b, i: (b, 0, 0, 0)),
        ],
        out_specs=pl.BlockSpec((1, tile, H * HEAD_DIM), lambda b, i: (b, i, 0)),
        out_shape=jax.ShapeDtypeStruct((B, S, H * HEAD_DIM), BF16),
        compiler_params=_params("parallel", "arbitrary"),
        name="fox_attention",
    )(fq, fk, fv)


def _merge_kernel(x_ref, on_ref, of_ref, sgn_ref, sgf_ref, wn_ref, wf_ref, wo_ref, g2_ref,
                  wr_ref, br_ref, x1_ref, h2_ref, ri_ref, rw_ref, cnt_ref, carry_scr):
    i = pl.program_id(0)
    tm = x_ref.shape[0]

    @pl.when(i == 0)
    def _():
        carry_scr[...] = jnp.zeros_like(carry_scr)

    a = _dot(on_ref[...], wn_ref[...])
    b = _dot(of_ref[...], wf_ref[...])
    mixed = (sgn_ref[...].astype(F32) * a + sgf_ref[...].astype(F32) * b).astype(BF16)
    x1 = x_ref[...] + _dot(mixed, wo_ref[...])
    x1_ref[...] = x1
    h2 = _rms(x1, g2_ref[...])
    h2_ref[...] = h2

    hh = h2.astype(BF16)
    hl = (h2 - hh.astype(F32)).astype(BF16)
    wr_hi, wr_lo = wr_ref[:, :LANES], wr_ref[:, LANES:]
    logits = _dot(hh, wr_hi) + _dot(hl, wr_hi) + _dot(hh, wr_lo) + br_ref[...]
    lane = lax.broadcasted_iota(I32, (tm, LANES), 1)
    lane_f = lane.astype(F32)
    l = jnp.where(lane < N_EXPERTS, logits, BELOW_NEG)
    idxs, vals, hots = [], [], []
    for _ in range(TOP_K):
        m = jnp.max(l, axis=-1, keepdims=True)
        idx = jnp.min(jnp.where(l == m, lane_f, float(LANES)), axis=-1, keepdims=True)
        hot = lane_f == idx
        idxs.append(idx)
        vals.append(m)
        hots.append(hot)
        l = jnp.where(hot, BELOW_NEG, l)
    exps = [jnp.exp(v - vals[0]) for v in vals]
    den = exps[0]
    for e in exps[1:]:
        den = den + e

    chosen = jnp.zeros((tm, LANES), F32)
    for hot in hots:
        chosen = chosen + jnp.where(hot, 1.0, 0.0)
    r_i = lax.broadcasted_iota(I32, (tm, tm), 0)
    c_i = lax.broadcasted_iota(I32, (tm, tm), 1)
    earlier = jnp.where(c_i < r_i, 1.0, 0.0).astype(BF16)
    before = _dot(earlier, chosen.astype(BF16)) + carry_scr[0:1, :]
    carry_scr[...] = carry_scr[...] + jnp.sum(chosen, axis=0, keepdims=True)
    cnt_ref[...] = carry_scr[...]

    ri = jnp.zeros((tm, LANES), F32)
    rw = jnp.zeros((tm, LANES), F32)
    for k in range(TOP_K):
        rank = jnp.sum(jnp.where(hots[k], before, 0.0), axis=-1, keepdims=True)
        ri = jnp.where(lane == k, idxs[k], ri)
        ri = jnp.where(lane == TOP_K + k, rank, ri)
        rw = jnp.where(lane == k, exps[k] / den, rw)
    ri_ref[...] = ri.astype(I32)
    rw_ref[...] = rw


def _merge(x2d, o_nsa, o_fox, sgn, sgf, wn, wf, wo, g2, wr_pair, br, tm):
    T, D = x2d.shape
    wq = o_nsa.shape[1]
    row = lambda w: pl.BlockSpec((tm, w), lambda i: (i, 0))
    full = lambda a: pl.BlockSpec(a.shape, lambda i: (0,) * a.ndim)
    return pl.pallas_call(
        _merge_kernel,
        grid=(T // tm,),
        in_specs=[row(D), row(wq), row(wq), row(D), row(D), full(wn), full(wf), full(wo), full(g2),
                  full(wr_pair), full(br)],
        out_specs=[row(D), row(D), row(LANES), row(LANES), pl.BlockSpec((SUBLANES, LANES), lambda i: (0, 0))],
        out_shape=[
            jax.ShapeDtypeStruct((T, D), F32), jax.ShapeDtypeStruct((T, D), F32),
            jax.ShapeDtypeStruct((T, LANES), I32), jax.ShapeDtypeStruct((T, LANES), F32),
            jax.ShapeDtypeStruct((SUBLANES, LANES), F32),
        ],
        scratch_shapes=[pltpu.VMEM((SUBLANES, LANES), F32)],
        compiler_params=_params("arbitrary"),
        name="merge_router",
    )(x2d, o_nsa, o_fox, sgn, sgf, wn, wf, wo, g2, wr_pair, br)


def _dest_kernel(ri_ref, cnt_ref, dest_ref, tab_ref):
    tm = ri_ref.shape[0]
    shift = MOE_ROWS.bit_length() - 1
    cnt = cnt_ref[...].astype(I32)
    padded = ((cnt + (MOE_ROWS - 1)) >> shift) << shift
    lane8 = lax.broadcasted_iota(I32, (SUBLANES, LANES), 1)
    pend = padded
    sh = 1
    while sh < LANES:
        pend = pend + jnp.where(lane8 >= sh, pltpu.roll(pend, sh, 1), 0)
        sh *= 2
    pstart = pend - padded
    row8 = lax.broadcasted_iota(I32, (SUBLANES, LANES), 0)
    tab_ref[...] = jnp.where(row8 == 0, cnt, jnp.where(row8 == 1, padded, jnp.where(row8 == 2, pstart, pend)))

    ri = ri_ref[...]
    lane = lax.broadcasted_iota(I32, (tm, LANES), 1)
    start_row = pstart[0:1, :].astype(F32)
    dest = jnp.zeros((tm, LANES), F32)
    for k in range(TOP_K):
        hot = lane == ri[:, k:k + 1]
        base = jnp.sum(jnp.where(hot, start_row, 0.0), axis=-1, keepdims=True)
        dest = jnp.where(lane == k, base + ri[:, TOP_K + k:TOP_K + k + 1].astype(F32), dest)
    dest_ref[0] = dest.T[:SUBLANES].astype(I32)


def _dest(ri, cnt, tm):
    T = ri.shape[0]
    return pl.pallas_call(
        _dest_kernel,
        grid=(T // tm,),
        in_specs=[pl.BlockSpec((tm, LANES), lambda i: (i, 0)), pl.BlockSpec((SUBLANES, LANES), lambda i: (0, 0))],
        out_specs=[pl.BlockSpec((1, SUBLANES, tm), lambda i: (i, 0, 0)),
                   pl.BlockSpec((SUBLANES, LANES), lambda i: (0, 0))],
        out_shape=[jax.ShapeDtypeStruct((T // tm, SUBLANES, tm), I32), jax.ShapeDtypeStruct((SUBLANES, LANES), I32)],
        compiler_params=_params("arbitrary"),
        name="moe_dest",
    )(ri, cnt)


def _pad_copies(tab_ref, zbuf, xin_hbm, zsem):
    out = []
    for e in range(N_EXPERTS):
        cnt = tab_ref[e]
        pad = tab_ref[LANES + e] - cnt
        base = tab_ref[2 * LANES + e] + cnt
        piece = 1
        while piece < MOE_ROWS:
            cond = (pad & piece) != 0
            if piece < SUBLANES:
                for r in range(piece):
                    out.append((cond, pltpu.make_async_copy(
                        zbuf.at[pl.ds(0, 1)], xin_hbm.at[pl.ds(base + r, 1)], zsem)))
            else:
                out.append((cond, pltpu.make_async_copy(
                    zbuf.at[pl.ds(0, piece)], xin_hbm.at[pl.ds(pl.multiple_of(base, SUBLANES), piece)], zsem)))
            base = base + jnp.where(cond, piece, 0)
            piece *= 2
    last = N_EXPERTS - 1
    n_live = (tab_ref[2 * LANES + last] + tab_ref[LANES + last]) // MOE_ROWS
    n_blocks = xin_hbm.shape[0] // MOE_ROWS
    for blk in range(n_blocks - N_EXPERTS, n_blocks):
        out.append((blk >= n_live, pltpu.make_async_copy(
            zbuf, xin_hbm.at[pl.ds(blk * MOE_ROWS, MOE_ROWS)], zsem)))
    return out


def _dispatch_kernel(tab_ref, dest_hbm, h2_hbm, xin_hbm, idx_smem, hbuf, zbuf, lsem, ssem, isem, zsem):
    i = pl.program_id(0)
    n = pl.num_programs(0)
    tm = hbuf.shape[1]

    def load(j):
        return (pltpu.make_async_copy(h2_hbm.at[pl.ds(j * tm, tm)], hbuf.at[j % 3], lsem.at[j % 3]),
                pltpu.make_async_copy(dest_hbm.at[j], idx_smem.at[pl.ds((j % 3) * (TOP_K * tm), TOP_K * tm)],
                                      isem.at[j % 3]))

    def wait_scatters(j):
        for _ in range(TOP_K):
            pltpu.make_async_copy(hbuf.at[j % 3], xin_hbm.at[pl.ds(0, tm)], ssem.at[j % 3]).wait()

    @pl.when(i == 0)
    def _():
        for cp in load(0):
            cp.start()
        zbuf[...] = jnp.zeros_like(zbuf)
        for cond, cp in _pad_copies(tab_ref, zbuf, xin_hbm, zsem):
            @pl.when(cond)
            def _():
                cp.start()

    @pl.when(i + 1 < n)
    def _():
        for cp in load(i + 1):
            cp.start()

    for cp in load(i):
        cp.wait()
    for slot in range(3):
        @pl.when(i % 3 == slot)
        def _():
            def issue(t, _):
                for k in range(TOP_K):
                    d = idx_smem[slot * (TOP_K * tm) + k * tm + t]
                    pltpu.make_async_copy(hbuf.at[slot, pl.ds(t, 1)], xin_hbm.at[pl.ds(d, 1)],
                                          ssem.at[slot]).start(priority=k % 2)
                return 0

            lax.fori_loop(0, tm, issue, 0)

    @pl.when(i > 0)
    def _():
        wait_scatters(i - 1)

    @pl.when(i == n - 1)
    def _():
        wait_scatters(i)
        for cond, cp in _pad_copies(tab_ref, zbuf, xin_hbm, zsem):
            @pl.when(cond)
            def _():
                cp.wait()


def _dispatch(tab_flat, dest_tiles, h2p, n_rows, tm):
    T, W = h2p.shape
    return pl.pallas_call(
        _dispatch_kernel,
        grid_spec=pltpu.PrefetchScalarGridSpec(
            num_scalar_prefetch=1,
            grid=(T // tm,),
            in_specs=[pl.BlockSpec(memory_space=pl.ANY), pl.BlockSpec(memory_space=pl.ANY)],
            out_specs=pl.BlockSpec(memory_space=pl.ANY),
            scratch_shapes=[
                pltpu.SMEM((3 * TOP_K * tm,), I32),
                pltpu.VMEM((3, tm, W), F32),
                pltpu.VMEM((MOE_ROWS, W), F32),
                pltpu.SemaphoreType.DMA((3,)), pltpu.SemaphoreType.DMA((3,)), pltpu.SemaphoreType.DMA((3,)),
                pltpu.SemaphoreType.DMA,
            ],
        ),
        out_shape=jax.ShapeDtypeStruct((n_rows, W), F32),
        compiler_params=_params("arbitrary"),
        name="moe_dispatch",
    )(tab_flat, dest_tiles, h2p)


def _expert_kernel(be_ref, bs_ref, nx_ref, nv_ref, x_ref, w1_hbm, b1_ref, w2_hbm, b2_ref, y_ref,
                   w1f, w2f, w1b, w2b, slot_ref, sem1, sem2, *, d_ff):
    s = pl.program_id(0)
    live = s < nv_ref[0]
    fresh = (s == 0) | (be_ref[s] != be_ref[jnp.maximum(s - 1, 0)])

    def fetch(e, p):
        return (pltpu.make_async_copy(w1_hbm.at[e], w1f.at[p], sem1.at[p]),
                pltpu.make_async_copy(w2_hbm.at[e], w2f.at[p], sem2.at[p]))

    @pl.when(s == 0)
    def _():
        slot_ref[0] = 0
        for cp in fetch(be_ref[0], 0):
            cp.start()

    @pl.when(live & fresh)
    def _():
        p = slot_ref[0]
        for cp in fetch(be_ref[s], p):
            cp.wait()
        w1b[...] = w1f[p].astype(BF16)
        w2b[...] = w2f[p].astype(BF16)

        @pl.when(nx_ref[s] >= 0)
        def _():
            for cp in fetch(nx_ref[s], 1 - p):
                cp.start()

        slot_ref[0] = 1 - p

    @pl.when(live)
    def _():
        hc = _dot(x_ref[...].astype(BF16), w1b[...]) + b1_ref[0]
        gt = jnp.minimum(hc[:, :d_ff], SWIGLU_LIMIT)
        up = jnp.clip(hc[:, d_ff:], -SWIGLU_LIMIT, SWIGLU_LIMIT)
        glu = gt * jax.nn.sigmoid(SWIGLU_ALPHA * gt)
        act = ((up + 1.0) * glu).astype(BF16)
        y_ref[...] = _dot(act, w2b[...]) + b2_ref[0]

    @pl.when(jnp.logical_not(live))
    def _():
        y_ref[...] = jnp.zeros_like(y_ref)


def _experts(blk_e, blk_src, blk_next, n_live, xin, w1, b1, w2, b2):
    n_rows, W = xin.shape
    E, D, two_f = w1.shape
    d_ff = two_f // 2
    return pl.pallas_call(
        functools.partial(_expert_kernel, d_ff=d_ff),
        grid_spec=pltpu.PrefetchScalarGridSpec(
            num_scalar_prefetch=4,
            grid=(n_rows // MOE_ROWS,),
            in_specs=[
                pl.BlockSpec((MOE_ROWS, W), lambda s, be, bs, nx, nv: (bs[s], 0)),
                pl.BlockSpec(memory_space=pl.ANY),
                pl.BlockSpec((1, 1, two_f), lambda s, be, bs, nx, nv: (be[s], 0, 0)),
                pl.BlockSpec(memory_space=pl.ANY),
                pl.BlockSpec((1, 1, D), lambda s, be, bs, nx, nv: (be[s], 0, 0)),
            ],
            out_specs=pl.BlockSpec((MOE_ROWS, W), lambda s, be, bs, nx, nv: (s, 0)),
            scratch_shapes=[
                pltpu.VMEM((2, D, two_f), F32), pltpu.VMEM((2, d_ff, D), F32),
                pltpu.VMEM((D, two_f), BF16), pltpu.VMEM((d_ff, D), BF16),
                pltpu.SMEM((1,), I32), pltpu.SemaphoreType.DMA((2,)), pltpu.SemaphoreType.DMA((2,)),
            ],
        ),
        out_shape=jax.ShapeDtypeStruct((n_rows, W), F32),
        compiler_params=_params("arbitrary"),
        name="moe_experts",
    )(blk_e, blk_src, blk_next, n_live, xin, w1, b1, w2, b2)


def _combine_kernel(dest_hbm, x1_ref, rw_ref, gf_ref, y_hbm, o_ref, idx_smem, ybuf, gsem, isem):
    i = pl.program_id(0)
    n = pl.num_programs(0)
    tm = x1_ref.shape[0]

    def idx_copy(j):
        return pltpu.make_async_copy(dest_hbm.at[j], idx_smem.at[pl.ds((j % 2) * (TOP_K * tm), TOP_K * tm)],
                                     isem.at[j % 2])

    def issue_into(slot):
        def issue(t, _):
            for k in range(TOP_K):
                d = idx_smem[slot * (TOP_K * tm) + k * tm + t]
                pltpu.make_async_copy(y_hbm.at[pl.ds(d, 1)], ybuf.at[slot, k, pl.ds(t, 1)],
                                      gsem.at[slot]).start(priority=k % 2)
            return 0

        lax.fori_loop(0, tm, issue, 0)

    def issue_gathers(j):
        for slot in range(2):
            @pl.when(j % 2 == slot)
            def _():
                issue_into(slot)

    @pl.when(i == 0)
    def _():
        idx_copy(0).start()
        idx_copy(0).wait()
        issue_into(0)

        @pl.when(n > 1)
        def _():
            idx_copy(1).start()

    @pl.when(i + 1 < n)
    def _():
        idx_copy(i + 1).wait()
        issue_gathers(i + 1)

        @pl.when(i + 2 < n)
        def _():
            idx_copy(i + 2).start()

    slot = i % 2
    for k in range(TOP_K):
        pltpu.make_async_copy(y_hbm.at[pl.ds(0, tm)], ybuf.at[slot, k], gsem.at[slot]).wait()

    rw = rw_ref[...]
    x2 = x1_ref[...]
    for k in range(TOP_K):
        x2 = x2 + rw[:, k:k + 1] * ybuf[slot, k]
    o_ref[...] = _rms(x2, gf_ref[...])


def _combine(dest_tiles, x1, rw, gf, y, tm):
    T, D = x1.shape
    W = y.shape[1]
    return pl.pallas_call(
        _combine_kernel,
        grid=(T // tm,),
        in_specs=[
            pl.BlockSpec(memory_space=pl.ANY),
            pl.BlockSpec((tm, D), lambda i: (i, 0)),
            pl.BlockSpec((tm, LANES), lambda i: (i, 0)),
            pl.BlockSpec((1, D), lambda i: (0, 0)),
            pl.BlockSpec(memory_space=pl.ANY),
        ],
        out_specs=pl.BlockSpec((tm, D), lambda i: (i, 0)),
        out_shape=jax.ShapeDtypeStruct((T, D), F32),
        scratch_shapes=[
            pltpu.SMEM((2 * TOP_K * tm,), I32),
            pltpu.VMEM((2, TOP_K, tm, W), F32),
            pltpu.SemaphoreType.DMA((2,)), pltpu.SemaphoreType.DMA((2,)),
        ],
        compiler_params=_params("arbitrary"),
        name="moe_combine",
    )(dest_tiles, x1, rw, gf, y)


def _rope_tables(positions):
    half = ROT_DIM // 2
    freqs = ROPE_THETA ** (-jnp.arange(0, ROT_DIM, 2, dtype=F32) / ROT_DIM)
    ang = positions.astype(F32)[..., None] * freqs
    cs = jnp.concatenate([jnp.cos(ang), jnp.sin(ang)], axis=-1)
    cs3 = jnp.concatenate(_split3(cs), axis=-1)
    spread = np.zeros((3 * ROT_DIM, 3 * LANES), np.float32)
    for term in range(3):
        for i in range(half):
            for head0 in range(0, LANES, HEAD_DIM):
                spread[term * ROT_DIM + i, [head0 + i, head0 + half + i]] = 1.0
                spread[term * ROT_DIM + half + i, LANES + head0 + i] = -1.0
                spread[term * ROT_DIM + half + i, 2 * LANES + head0 + half + i] = 1.0
    return cs3, jnp.asarray(spread, BF16)


def _arrange_w_in(w_in, d_model):
    nq, nkv, ng = NSA_HEADS * HEAD_DIM, NSA_KV_GROUPS * HEAD_DIM, NSA_HEADS * 3
    fw = FOX_HEADS * HEAD_DIM
    sizes = [nq] + [nkv] * 6 + [ng, fw, fw, fw, FOX_HEADS, d_model, d_model]
    offs = np.concatenate([[0], np.cumsum(sizes)])
    piece = lambda k: w_in[:, offs[k]:offs[k + 1]]
    gates, ff = piece(7), piece(11)
    per_group = HEADS_PER_GROUP * 3
    zeros = lambda n: jnp.zeros((w_in.shape[0], n), w_in.dtype)
    misc0 = jnp.concatenate([gates[:, :per_group], zeros(FGATE_LANE - per_group), ff,
                             zeros(LANES - FGATE_LANE - FOX_HEADS)], axis=1)
    misc1 = jnp.concatenate([gates[:, per_group:], zeros(LANES - per_group)], axis=1)
    cols = [piece(k) for k in range(7)] + [piece(8), piece(9), piece(10), piece(12), piece(13), misc0, misc1]
    return jnp.concatenate(cols, axis=1).astype(BF16)


def _split_bf16(a):
    hi = a.astype(BF16)
    return hi, (a - hi.astype(F32)).astype(BF16)


def kernel(x, positions, norm1_g, w_in, cmp_pos_emb, cmp_w1, cmp_w2, fox_f_bias, w_proj_nsa, w_proj_fox, w_out,
           norm2_g, router_w, router_b, expert_w1, expert_b1, expert_w2, expert_b2, norm_f_g):
    B, S, D = x.shape
    T = B * S
    assert norm1_g.shape[0] == 1, "the combine kernel fuses the output norm, so it must follow the only layer"
    assert S % KV_CHUNK == 0 or S < KV_CHUNK
    assert S // SEL_BLOCK <= LANES - HEAD_DIM and S % (CMP_STRIDE * SUBLANES) == 0
    tm_proj = min(512, S)
    assert WINDOW % tm_proj == 0 or tm_proj % WINDOW == 0
    tm_tok = min(512, T)
    l = 0

    rope_cs, rope_spread = _rope_tables(positions)
    n_sb = S // SEL_BLOCK
    n_rows16 = S // CMP_STRIDE
    ci = np.arange(n_rows16)[None, :] * CMP_STRIDE
    sj = np.arange(LANES)[:, None] * SEL_BLOCK
    overlap = jnp.asarray(((ci < sj + SEL_BLOCK) & (ci + CMP_BLOCK > sj) & (np.arange(LANES)[:, None] < n_sb)
                           & (np.arange(n_rows16)[None, :] < n_rows16 - 1)).astype(np.float32), BF16)

    w_all = _arrange_w_in(w_in[l], D)
    fb_row = jnp.zeros((1, LANES), F32).at[0, FGATE_LANE:FGATE_LANE + FOX_HEADS].set(fox_f_bias[l].astype(F32))
    (qraw, qrot, kvc, ks, vs, kw, vw, fq, fk, fv, sgn, sgf, misc) = _inproj(
        x, norm1_g[l][None, :], w_all, rope_cs, rope_spread, fb_row, tm_proj)
    qrot = qrot.reshape(B, NSA_HEADS, S, LANES)
    fq = fq.reshape(B, FOX_HEADS, S, LANES)
    fv = fv.reshape(B, FOX_HEADS, S, LANES)

    kv_rows = kvc.reshape(B, 2, NSA_KV_GROUPS, n_rows16, CMP_STRIDE * HEAD_DIM)
    pe = cmp_pos_emb[l].reshape(2, 1, CMP_BLOCK * HEAD_DIM).astype(F32)
    cw1 = jnp.pad(cmp_w1[l], ((0, 0), (0, 0), (0, LANES - HEAD_DIM))).astype(BF16)
    cw2 = jnp.pad(cmp_w2[l], ((0, 0), (0, LANES - HEAD_DIM), (0, 0))).astype(BF16)
    cmp_kv = _compress(kv_rows, pe, cw1, cw2)

    o_nsa = _nsa(qraw, qrot, cmp_kv, ks, vs, kw, vw, misc, overlap)
    o_fox = _fox(fq, fk, fv)

    wr = jnp.pad(router_w[l], ((0, 0), (0, LANES - N_EXPERTS)))
    wr_pair = jnp.concatenate(_split_bf16(wr), axis=1)
    br = jnp.pad(router_b[l], (0, LANES - N_EXPERTS))[None, :].astype(F32)
    x1, h2, ri, rw, cnt = _merge(
        x.reshape(T, D), o_nsa.reshape(T, -1), o_fox.reshape(T, -1), sgn.reshape(T, D), sgf.reshape(T, D),
        w_proj_nsa[l].astype(BF16), w_proj_fox[l].astype(BF16), w_out[l].astype(BF16),
        norm2_g[l][None, :], wr_pair, br, tm_tok)

    dest, tab = _dest(ri, cnt, tm_tok)
    dest_tiles = dest[:, :TOP_K, :].reshape(T // tm_tok, TOP_K * tm_tok)
    pend = tab[3, :N_EXPERTS]
    n_blocks = (T * TOP_K) // MOE_ROWS + N_EXPERTS
    n_live = (pend[N_EXPERTS - 1] // MOE_ROWS).astype(I32)
    blk_src = jnp.minimum(jnp.arange(n_blocks, dtype=I32), n_live - 1)
    blk_e = jnp.sum((pend[None, :] <= (blk_src * MOE_ROWS)[:, None]).astype(I32), axis=1)
    blk_e = jnp.minimum(blk_e, N_EXPERTS - 1)
    after = jnp.sum(jnp.where(blk_e[:, None] == jnp.arange(N_EXPERTS)[None, :], pend[None, :], 0), axis=1)
    blk_next = jnp.minimum(jnp.sum((pend[None, :] <= after[:, None]).astype(I32), axis=1), N_EXPERTS - 1)
    blk_next = jnp.where(after < n_live * MOE_ROWS, blk_next, -1).astype(I32)

    xin = _dispatch(tab[:3].reshape(-1), dest_tiles, h2,
                    n_blocks * MOE_ROWS, tm_tok)
    y = _experts(blk_e, blk_src, blk_next, n_live[None], xin, expert_w1[l], expert_b1[l][:, None, :],
                 expert_w2[l], expert_b2[l][:, None, :])
    out = _combine(dest_tiles, x1, rw, norm_f_g[None, :], y, tm_tok)
    return out.reshape(B, S, D)
```

```python
import functools

import numpy as np
import jax
import jax.numpy as jnp
from jax import lax
from jax.experimental import pallas as pl
from jax.experimental.pallas import tpu as pltpu

HEAD_DIM = 64
NSA_HEADS = 8
NSA_KV_GROUPS = 2
HEADS_PER_GROUP = NSA_HEADS // NSA_KV_GROUPS
FOX_HEADS = 8
ROT_DIM = HEAD_DIM // 4
ROPE_THETA = 500000.0
CMP_BLOCK = 32
CMP_STRIDE = 16
SEL_BLOCK = 64
N_SEL = 8
WINDOW = 512
N_EXPERTS = 32
TOP_K = 4
SWIGLU_ALPHA = 1.702
SWIGLU_LIMIT = 7.0
NORM_EPS = 1e-6
NEG = -1e30
BELOW_NEG = -3e38
FORCE_BONUS = 1e4

LANES = 128
SUBLANES = 8
TOKEN_TILE = 512
PROJ_COLS = 512
NSA_Q_TILE = 256
KV_CHUNK = 512
FOX_TILE = 512
MOE_ROWS = 512
FGATE_LANE = 32
VMEM_LIMIT = 56 * 1024 * 1024

F32 = jnp.float32
BF16 = jnp.bfloat16
I32 = jnp.int32


def _dot(a, b):
    return jnp.dot(a, b, preferred_element_type=F32)


def _dot_nt(a, b):
    return lax.dot_general(a, b, (((1,), (1,)), ((), ())), preferred_element_type=F32)


def _params(*sem):
    return pltpu.CompilerParams(dimension_semantics=sem, vmem_limit_bytes=VMEM_LIMIT)


def _rms(x, g):
    return x * lax.rsqrt(jnp.mean(x * x, axis=-1, keepdims=True) + NORM_EPS) * g


def _split3(a):
    t1 = a.astype(BF16)
    r1 = a - t1.astype(F32)
    t2 = r1.astype(BF16)
    t3 = (r1 - t2.astype(F32)).astype(BF16)
    return t1, t2, t3


def _inproj_kernel(x_ref, g_ref, w_ref, cs_ref, spread_ref, fb_ref,
                   qraw_ref, qrot_ref, kvc_ref, ks_ref, vs_ref, kw_ref, vw_ref,
                   fq_ref, fk_ref, fv_ref, sgn_ref, sgf_ref, misc_ref, csum_scr, *, d_model):
    scale = HEAD_DIM ** -0.5
    i = pl.program_id(1)
    tm = x_ref.shape[1]
    xn = _rms(x_ref[0], g_ref[...]).astype(BF16)
    lane = lax.broadcasted_iota(I32, (tm, LANES), 1)
    tabs = _dot(cs_ref[0], spread_ref[...])
    rc = tabs[:, :LANES] + jnp.where((lane & (HEAD_DIM - 1)) >= ROT_DIM, 1.0, 0.0)
    rs1, rs2 = tabs[:, LANES:2 * LANES], tabs[:, 2 * LANES:]
    low = lane < HEAD_DIM
    one_at_64 = jnp.where(lane == HEAD_DIM, 1.0, 0.0)

    def rope(slab):
        half = ROT_DIM // 2
        return slab * rc + pltpu.roll(slab, LANES - half, 1) * rs1 + pltpu.roll(slab, half, 1) * rs2

    def put_pair(ref, lead, slab, tail):
        ref[lead + (0,)] = jnp.where(low, slab, tail).astype(BF16)
        ref[lead + (1,)] = jnp.where(low, pltpu.roll(slab, HEAD_DIM, 1), tail).astype(BF16)

    misc_off = w_ref.shape[1] - 2 * LANES
    r = _dot(xn, w_ref[:, misc_off:])
    m0 = r[:, :LANES]
    z = m0 + fb_ref[...]
    log_f = jnp.minimum(z, 0.0) - jnp.log(1.0 + jnp.exp(-jnp.abs(z)))
    misc_ref[0, 0] = jax.nn.sigmoid(m0)
    misc_ref[0, 1] = jax.nn.sigmoid(r[:, LANES:])

    @pl.when(i == 0)
    def _():
        csum_scr[...] = jnp.zeros_like(csum_scr)

    in_gate = (lane >= FGATE_LANE) & (lane < FGATE_LANE + FOX_HEADS)
    t1, t2, t3 = _split3(jnp.where(in_gate, log_f, 0.0))
    r_i = lax.broadcasted_iota(I32, (tm, tm), 0)
    c_i = lax.broadcasted_iota(I32, (tm, tm), 1)
    upto = jnp.where(c_i <= r_i, 1.0, 0.0).astype(BF16)
    parts = _dot(upto, jnp.concatenate([t1, t2, t3], axis=1))
    csum = parts[:, :LANES] + parts[:, LANES:2 * LANES] + parts[:, 2 * LANES:] + csum_scr[0:1, :]
    csum_scr[...] = jnp.broadcast_to(csum[tm - 1:tm, :], csum_scr.shape)

    nq = NSA_HEADS * HEAD_DIM
    r = _dot(xn, w_ref[:, 0:nq]) * scale
    for s in range(nq // LANES):
        slab = r[:, LANES * s:LANES * (s + 1)]
        qraw_ref[0, 2 * s] = slab[:, :HEAD_DIM].astype(BF16)
        qraw_ref[0, 2 * s + 1] = slab[:, HEAD_DIM:].astype(BF16)
        put_pair(qrot_ref, (0, s), rope(slab), 0.0)
    off = nq

    r = _dot(xn, w_ref[:, off:off + 6 * LANES])
    for t in range(2):
        slab = r[:, t * LANES:(t + 1) * LANES]
        kvc_ref[0, t, 0] = slab[:, :HEAD_DIM].astype(BF16)
        kvc_ref[0, t, 1] = slab[:, HEAD_DIM:].astype(BF16)
    tok = i * tm + lax.broadcasted_iota(I32, (tm, LANES), 0)
    block_hot = jnp.where(lane - HEAD_DIM == (tok >> (SEL_BLOCK.bit_length() - 1)), 1.0, 0.0)
    put_pair(ks_ref, (0,), rope(r[:, 2 * LANES:3 * LANES]), block_hot)
    put_pair(vs_ref, (0,), r[:, 3 * LANES:4 * LANES], one_at_64)
    put_pair(kw_ref, (0,), rope(r[:, 4 * LANES:5 * LANES]), 0.0)
    put_pair(vw_ref, (0,), r[:, 5 * LANES:6 * LANES], one_at_64)
    off += 6 * LANES

    fw = FOX_HEADS * HEAD_DIM
    ones3 = jnp.where((lane >= HEAD_DIM) & (lane < HEAD_DIM + 3), 1.0, 0.0)
    r = _dot(xn, w_ref[:, off:off + fw]) * scale
    for s in range(fw // LANES):
        put_pair(fq_ref, (0, s), r[:, LANES * s:LANES * (s + 1)], ones3)
    off += fw
    r = _dot(xn, w_ref[:, off:off + fw])
    for s in range(fw // LANES):
        slab = r[:, LANES * s:LANES * (s + 1)]
        for par in range(2):
            h = 2 * s + par
            neg_c = -csum[:, FGATE_LANE + h:FGATE_LANE + h + 1]
            c1, c2, c3 = _split3(neg_c)
            tail = jnp.where(lane == HEAD_DIM, c1.astype(F32), jnp.where(
                lane == HEAD_DIM + 1, c2.astype(F32), jnp.where(lane == HEAD_DIM + 2, c3.astype(F32), 0.0)))
            head = slab if par == 0 else pltpu.roll(slab, HEAD_DIM, 1)
            fk_ref[0, h] = jnp.where(low, head, tail).astype(BF16)
    off += fw
    r = _dot(xn, w_ref[:, off:off + fw])
    for s in range(fw // LANES):
        put_pair(fv_ref, (0, s), r[:, LANES * s:LANES * (s + 1)], one_at_64)
    off += fw

    step = min(PROJ_COLS, d_model)
    for ref in (sgn_ref, sgf_ref):
        for c in range(0, d_model, step):
            r = _dot(xn, w_ref[:, off + c:off + c + step])
            ref[0, :, c:c + step] = jax.nn.sigmoid(r).astype(BF16)
        off += d_model


def _inproj(x, norm_g, w_all, rope_cs, rope_spread, f_bias_row, tm):
    B, S, D = x.shape
    ncol = w_all.shape[1]
    G = NSA_KV_GROUPS
    wide = lambda heads: jax.ShapeDtypeStruct((B, heads, S, LANES), BF16)
    pair_shape = lambda heads: jax.ShapeDtypeStruct((B, heads // 2, 2, S, LANES), BF16)
    pair_spec = lambda heads: pl.BlockSpec((1, heads // 2, 2, tm, LANES), lambda b, i: (b, 0, 0, i, 0))
    kvspec = pl.BlockSpec((1, G, tm, LANES), lambda b, i: (b, 0, i, 0))
    gspec = pl.BlockSpec((1, tm, D), lambda b, i: (b, i, 0))
    return pl.pallas_call(
        functools.partial(_inproj_kernel, d_model=D),
        grid=(B, S // tm),
        in_specs=[
            pl.BlockSpec((1, tm, D), lambda b, i: (b, i, 0)),
            pl.BlockSpec((1, D), lambda b, i: (0, 0)),
            pl.BlockSpec((D, ncol), lambda b, i: (0, 0)),
            pl.BlockSpec((1, tm, rope_cs.shape[2]), lambda b, i: (b, i, 0)),
            pl.BlockSpec(rope_spread.shape, lambda b, i: (0, 0)),
            pl.BlockSpec((1, LANES), lambda b, i: (0, 0)),
        ],
        out_specs=[
            pl.BlockSpec((1, NSA_HEADS, tm, HEAD_DIM), lambda b, i: (b, 0, i, 0)),
            pair_spec(NSA_HEADS),
            pl.BlockSpec((1, 2, G, tm, HEAD_DIM), lambda b, i: (b, 0, 0, i, 0)),
            kvspec, kvspec, kvspec, kvspec,
            pair_spec(FOX_HEADS),
            pl.BlockSpec((1, FOX_HEADS, tm, LANES), lambda b, i: (b, 0, i, 0)),
            pair_spec(FOX_HEADS),
            gspec, gspec,
            pl.BlockSpec((1, 2, tm, LANES), lambda b, i: (b, 0, i, 0)),
        ],
        out_shape=[
            jax.ShapeDtypeStruct((B, NSA_HEADS, S, HEAD_DIM), BF16),
            pair_shape(NSA_HEADS),
            jax.ShapeDtypeStruct((B, 2, G, S, HEAD_DIM), BF16),
            wide(G), wide(G), wide(G), wide(G),
            pair_shape(FOX_HEADS), wide(FOX_HEADS), pair_shape(FOX_HEADS),
            jax.ShapeDtypeStruct((B, S, D), BF16), jax.ShapeDtypeStruct((B, S, D), BF16),
            jax.ShapeDtypeStruct((B, 2, S, LANES), F32),
        ],
        scratch_shapes=[pltpu.VMEM((SUBLANES, LANES), F32)],
        compiler_params=_params("parallel", "arbitrary"),
        name="inproj",
    )(x, norm_g, w_all, rope_cs, rope_spread, f_bias_row)


def _compress_kernel(r_ref, pe_ref, w1_ref, w2_ref, o_ref):
    half = CMP_STRIDE * HEAD_DIM
    rows = r_ref[0, 0, 0].astype(F32)
    lo = (rows + pe_ref[0, :, :half]).astype(BF16)
    hi = (rows + pe_ref[0, :, half:]).astype(BF16)
    a = _dot(lo, w1_ref[0, :half])
    b = _dot(hi, w1_ref[0, half:])
    n = a.shape[0]
    pre = a + pltpu.roll(b, n - 1, 0)
    act = jax.nn.gelu(pre, approximate=True).astype(BF16)
    o_ref[0, 0, 0] = _dot(act, w2_ref[0]).astype(BF16)


def _compress(kv_rows, pe, w1, w2):
    B, _, G, n_rows, width = kv_rows.shape
    return pl.pallas_call(
        _compress_kernel,
        grid=(B, 2, G),
        in_specs=[
            pl.BlockSpec((1, 1, 1, n_rows, width), lambda b, t, g: (b, t, g, 0, 0)),
            pl.BlockSpec((1, 1, 2 * width), lambda b, t, g: (t, 0, 0)),
            pl.BlockSpec((1, 2 * width, LANES), lambda b, t, g: (t, 0, 0)),
            pl.BlockSpec((1, LANES, HEAD_DIM), lambda b, t, g: (t, 0, 0)),
        ],
        out_specs=pl.BlockSpec((1, 1, 1, n_rows, HEAD_DIM), lambda b, t, g: (b, t, g, 0, 0)),
        out_shape=jax.ShapeDtypeStruct((B, 2, G, n_rows, HEAD_DIM), BF16),
        compiler_params=_params("parallel", "parallel", "parallel"),
        name="nsa_compress",
    )(kv_rows, pe, w1, w2)


def _softmax_rows(logits, ok):
    l = jnp.where(ok, logits, NEG)
    m = jnp.max(l, axis=-1, keepdims=True)
    e = jnp.where(ok, jnp.exp(l - m), 0.0)
    s = jnp.sum(e, axis=-1, keepdims=True)
    return e / jnp.where(s > 0.0, s, 1.0)


def _nsa_kernel(qraw_ref, qrot_ref, kcmp_ref, vcmp_ref, ks_ref, vs_ref, kw_ref, vw_ref,
                misc_ref, overlap_ref, o_ref, *, seq, chunk):
    qi = pl.program_id(2)
    n_full = qi * NSA_Q_TILE // chunk
    n_front = WINDOW // NSA_Q_TILE
    args = (qraw_ref, qrot_ref, kcmp_ref, vcmp_ref, ks_ref, vs_ref, kw_ref, vw_ref, misc_ref, overlap_ref, o_ref)
    for front in range(n_front):
        @pl.when(qi == front)
        def _():
            _nsa_tile(*args, front, front * NSA_Q_TILE // chunk, front=True, seq=seq, chunk=chunk)

    for last in range(n_front * NSA_Q_TILE // chunk, seq // chunk):
        @pl.when((n_full == last) & (qi >= n_front))
        def _():
            _nsa_tile(*args, qi, last, front=False, seq=seq, chunk=chunk)


def _nsa_tile(qraw_ref, qrot_ref, kcmp_ref, vcmp_ref, ks_ref, vs_ref, kw_ref, vw_ref,
              misc_ref, overlap_ref, o_ref, qi, last, *, front, seq, chunk):
    tq_n = NSA_Q_TILE
    hpg = HEADS_PER_GROUP
    rows_n = hpg * tq_n
    n_sb = seq // SEL_BLOCK
    n_sel = min(N_SEL, n_sb)
    t0 = qi * tq_n
    tq = t0 + lax.broadcasted_iota(I32, (tq_n, 1), 0)
    q_win = qrot_ref[0].reshape(rows_n, LANES)
    qrot = q_win.astype(F32)
    lane_r = lax.broadcasted_iota(I32, (rows_n, LANES), 1)
    row_in_tile = lax.broadcasted_iota(I32, (rows_n, 1), 0) & (tq_n - 1)

    col = lax.broadcasted_iota(I32, (rows_n, tq_n), 1)
    if front:
        wlen = t0 + tq_n
        kw, vw = kw_ref[0, 0, 0:wlen, :], vw_ref[0, 0, 0:wlen, :]
        sw = _dot_nt(q_win, kw)
        parts = [sw[:, :t0]] if t0 else []
    else:
        wlen = WINDOW + tq_n
        w0 = pl.multiple_of(t0 - WINDOW, tq_n)
        kw, vw = kw_ref[0, 0, pl.ds(w0, wlen), :], vw_ref[0, 0, pl.ds(w0, wlen), :]
        sw = _dot_nt(q_win, kw)
        parts = [jnp.where(col > row_in_tile, sw[:, :tq_n], NEG)]
        if WINDOW > tq_n:
            parts.append(sw[:, tq_n:WINDOW])
    parts.append(jnp.where(col <= row_in_tile, sw[:, wlen - tq_n:], NEG))
    sw = jnp.concatenate(parts, axis=1)
    mw = jnp.broadcast_to(jnp.max(sw, axis=-1, keepdims=True), (rows_n, LANES))
    acc_w = _dot(jnp.exp(sw - jnp.tile(mw, (1, wlen // LANES))).astype(BF16), vw)

    q = qraw_ref[0].reshape(rows_n, HEAD_DIM)
    n_c = kcmp_ref.shape[3]
    lg = _dot_nt(q, kcmp_ref[0, 0, 0])
    cidx = lax.broadcasted_iota(I32, (tq_n, n_c), 1)
    cmask = cidx * CMP_STRIDE + (CMP_BLOCK - 1) <= tq
    vcmp = vcmp_ref[0, 0, 0]
    o_c = []
    p_sum = jnp.zeros((tq_n, n_c), F32)
    for h in range(hpg):
        p = _softmax_rows(lg[h * tq_n:(h + 1) * tq_n], cmask)
        p_sum = p_sum + p
        o_c.append(_dot(p.astype(BF16), vcmp))

    p_hi = p_sum.astype(BF16)
    p_lo = (p_sum - p_hi.astype(F32)).astype(BF16)
    imp = (_dot_nt(overlap_ref[...], p_hi) + _dot_nt(overlap_ref[...], p_lo))[:n_sb]
    j = lax.broadcasted_iota(I32, (n_sb, tq_n), 0)
    t_row = t0 + lax.broadcasted_iota(I32, (n_sb, tq_n), 1)
    cur = t_row >> (SEL_BLOCK.bit_length() - 1)
    forced = jnp.where(j == 0, 1.0, jnp.where(j == cur, 1.0, jnp.where(j == cur - 1, 1.0, 0.0)))
    score = jnp.where(forced > 0.0, imp + FORCE_BONUS, jnp.where(j * SEL_BLOCK <= t_row, imp, NEG))
    beaten = jnp.zeros((n_sb, tq_n), F32)
    for jp in range(n_sb):
        row = score[jp:jp + 1, :]
        wins_ties = jnp.where(row >= score, 1.0, 0.0)
        wins = jnp.where(row > score, 1.0, 0.0)
        beaten = beaten + jnp.where(j > jp, wins_ties, wins)
    penalty = jnp.where(beaten < n_sel, 0.0, NEG)
    penalty = jnp.concatenate([penalty, jnp.zeros((LANES - n_sb, tq_n), F32)], axis=0).T
    penalty = pltpu.roll(penalty, HEAD_DIM, 1)

    q_sel = jnp.where(lane_r < HEAD_DIM, qrot, jnp.concatenate([penalty] * hpg, axis=0)).astype(BF16)

    logits = [_dot_nt(q_sel, ks_ref[0, 0, c * chunk:(c + 1) * chunk, :]) for c in range(last + 1)]
    kpos = last * chunk + lax.broadcasted_iota(I32, (rows_n, chunk), 1)
    logits[last] = jnp.where(kpos <= t0 + row_in_tile, logits[last], NEG)
    m = jnp.max(logits[0], axis=-1, keepdims=True)
    for s in logits[1:]:
        m = jnp.maximum(m, jnp.max(s, axis=-1, keepdims=True))
    m = jnp.tile(jnp.broadcast_to(m, (rows_n, LANES)), (1, chunk // LANES))
    acc_s = _dot(jnp.exp(logits[0] - m).astype(BF16), vs_ref[0, 0, 0:chunk, :])
    for c in range(1, last + 1):
        acc_s = acc_s + _dot(jnp.exp(logits[c] - m).astype(BF16), vs_ref[0, 0, c * chunk:(c + 1) * chunk, :])

    g = misc_ref[0, 0]
    for h in range(hpg):
        a_s = acc_s[h * tq_n:(h + 1) * tq_n]
        a_w = acc_w[h * tq_n:(h + 1) * tq_n]
        o_s = a_s[:, :HEAD_DIM] / a_s[:, HEAD_DIM:HEAD_DIM + 1]
        o_w = a_w[:, :HEAD_DIM] / a_w[:, HEAD_DIM:HEAD_DIM + 1]
        o_h = g[:, 3 * h:3 * h + 1] * o_c[h] + g[:, 3 * h + 1:3 * h + 2] * o_s + g[:, 3 * h + 2:3 * h + 3] * o_w
        o_ref[0, :, HEAD_DIM * h:HEAD_DIM * (h + 1)] = o_h.astype(BF16)


def _nsa(qraw, qrot, cmp_kv, ks, vs, kw, vw, misc, overlap):
    B, _, S, _ = qraw.shape
    G = NSA_KV_GROUPS
    n_c = cmp_kv.shape[3]
    chunk = min(KV_CHUNK, S)
    kvspec = pl.BlockSpec((1, 1, S, LANES), lambda b, g, i: (b, g, 0, 0))
    return pl.pallas_call(
        functools.partial(_nsa_kernel, seq=S, chunk=chunk),
        grid=(B, G, S // NSA_Q_TILE),
        in_specs=[
            pl.BlockSpec((1, HEADS_PER_GROUP, NSA_Q_TILE, HEAD_DIM), lambda b, g, i: (b, g, i, 0)),
            pl.BlockSpec((1, HEADS_PER_GROUP, NSA_Q_TILE, LANES), lambda b, g, i: (b, g, i, 0)),
            pl.BlockSpec((1, 1, 1, n_c, HEAD_DIM), lambda b, g, i: (b, 0, g, 0, 0)),
            pl.BlockSpec((1, 1, 1, n_c, HEAD_DIM), lambda b, g, i: (b, 1, g, 0, 0)),
            kvspec, kvspec, kvspec, kvspec,
            pl.BlockSpec((1, 1, NSA_Q_TILE, LANES), lambda b, g, i: (b, g, i, 0)),
            pl.BlockSpec((LANES, n_c), lambda b, g, i: (0, 0)),
        ],
        out_specs=pl.BlockSpec((1, NSA_Q_TILE, HEADS_PER_GROUP * HEAD_DIM), lambda b, g, i: (b, i, g)),
        out_shape=jax.ShapeDtypeStruct((B, S, NSA_HEADS * HEAD_DIM), BF16),
        compiler_params=_params("parallel", "parallel", "arbitrary"),
        name="nsa_attention",
    )(qraw, qrot, cmp_kv, cmp_kv, ks, vs, kw, vw, misc, overlap)


def _fox_kernel(q_ref, k_ref, v_ref, o_ref):
    tile = q_ref.shape[2]
    qi = pl.program_id(1)
    for last in range(k_ref.shape[2] // tile):
        @pl.when(qi == last)
        def _():
            _fox_tile(q_ref, k_ref, v_ref, o_ref, last, tile)


def _fox_tile(q_ref, k_ref, v_ref, o_ref, last, tile):
    causal = (lax.broadcasted_iota(I32, (tile, tile), 1) <= lax.broadcasted_iota(I32, (tile, tile), 0))
    for h in range(FOX_HEADS):
        q = q_ref[0, h]
        logits = [_dot_nt(q, k_ref[0, h, c * tile:(c + 1) * tile, :]) for c in range(last + 1)]
        logits[last] = jnp.where(causal, logits[last], NEG)
        m = jnp.max(logits[0], axis=-1, keepdims=True)
        for s in logits[1:]:
            m = jnp.maximum(m, jnp.max(s, axis=-1, keepdims=True))
        m = jnp.tile(jnp.broadcast_to(m, (tile, LANES)), (1, tile // LANES))
        acc = _dot(jnp.exp(logits[0] - m).astype(BF16), v_ref[0, h, 0:tile, :])
        for c in range(1, last + 1):
            acc = acc + _dot(jnp.exp(logits[c] - m).astype(BF16), v_ref[0, h, c * tile:(c + 1) * tile, :])
        o_ref[0, :, HEAD_DIM * h:HEAD_DIM * (h + 1)] = (acc[:, :HEAD_DIM] / acc[:, HEAD_DIM:HEAD_DIM + 1]).astype(BF16)


def _fox(fq, fk, fv):
    B, H, S, _ = fq.shape
    tile = min(FOX_TILE, S)
    return pl.pallas_call(
        _fox_kernel,
        grid=(B, S // tile),
        in_specs=[
            pl.BlockSpec((1, H, tile, LANES), lambda b, i: (b, 0, i, 0)),
            pl.BlockSpec((1, H, S, LANES), lambda b, i: (b, 0, 0, 0)),
            pl.BlockSpec((1, H, S, LANES), lambda b, i: (b, 0, 0, 0)),
        ],
        out_specs=pl.BlockSpec((1, tile, H * HEAD_DIM), lambda b, i: (b, i, 0)),
        out_shape=jax.ShapeDtypeStruct((B, S, H * HEAD_DIM), BF16),
        compiler_params=_params("parallel", "arbitrary"),
        name="fox_attention",
    )(fq, fk, fv)


def _merge_kernel(x_ref, on_ref, of_ref, sgn_ref, sgf_ref, wn_ref, wf_ref, wo_ref, g2_ref,
                  wr_ref, br_ref, x1_ref, h2_ref, ri_ref, rw_ref, cnt_ref, carry_scr):
    i = pl.program_id(0)
    tm = x_ref.shape[0]

    @pl.when(i == 0)
    def _():
        carry_scr[...] = jnp.zeros_like(carry_scr)

    a = _dot(on_ref[...], wn_ref[...])
    b = _dot(of_ref[...], wf_ref[...])
    mixed = (sgn_ref[...].astype(F32) * a + sgf_ref[...].astype(F32) * b).astype(BF16)
    x1 = x_ref[...] + _dot(mixed, wo_ref[...])
    x1_ref[...] = x1
    h2 = _rms(x1, g2_ref[...])
    h2_ref[...] = h2

    hh = h2.astype(BF16)
    hl = (h2 - hh.astype(F32)).astype(BF16)
    wr_hi, wr_lo = wr_ref[:, :LANES], wr_ref[:, LANES:]
    logits = _dot(hh, wr_hi) + _dot(hl, wr_hi) + _dot(hh, wr_lo) + br_ref[...]
    lane = lax.broadcasted_iota(I32, (tm, LANES), 1)
    lane_f = lane.astype(F32)
    l = jnp.where(lane < N_EXPERTS, logits, BELOW_NEG)
    idxs, vals, hots = [], [], []
    for _ in range(TOP_K):
        m = jnp.max(l, axis=-1, keepdims=True)
        idx = jnp.min(jnp.where(l == m, lane_f, float(LANES)), axis=-1, keepdims=True)
        hot = lane_f == idx
        idxs.append(idx)
        vals.append(m)
        hots.append(hot)
        l = jnp.where(hot, BELOW_NEG, l)
    exps = [jnp.exp(v - vals[0]) for v in vals]
    den = exps[0]
    for e in exps[1:]:
        den = den + e

    chosen = jnp.zeros((tm, LANES), F32)
    for hot in hots:
        chosen = chosen + jnp.where(hot, 1.0, 0.0)
    r_i = lax.broadcasted_iota(I32, (tm, tm), 0)
    c_i = lax.broadcasted_iota(I32, (tm, tm), 1)
    earlier = jnp.where(c_i < r_i, 1.0, 0.0).astype(BF16)
    before = _dot(earlier, chosen.astype(BF16)) + carry_scr[0:1, :]
    carry_scr[...] = carry_scr[...] + jnp.sum(chosen, axis=0, keepdims=True)
    cnt_ref[...] = carry_scr[...]

    ri = jnp.zeros((tm, LANES), F32)
    rw = jnp.zeros((tm, LANES), F32)
    for k in range(TOP_K):
        rank = jnp.sum(jnp.where(hots[k], before, 0.0), axis=-1, keepdims=True)
        ri = jnp.where(lane == k, idxs[k], ri)
        ri = jnp.where(lane == TOP_K + k, rank, ri)
        rw = jnp.where(lane == k, exps[k] / den, rw)
    ri_ref[...] = ri.astype(I32)
    rw_ref[...] = rw


def _merge(x2d, o_nsa, o_fox, sgn, sgf, wn, wf, wo, g2, wr_pair, br, tm):
    T, D = x2d.shape
    wq = o_nsa.shape[1]
    row = lambda w: pl.BlockSpec((tm, w), lambda i: (i, 0))
    full = lambda a: pl.BlockSpec(a.shape, lambda i: (0,) * a.ndim)
    return pl.pallas_call(
        _merge_kernel,
        grid=(T // tm,),
        in_specs=[row(D), row(wq), row(wq), row(D), row(D), full(wn), full(wf), full(wo), full(g2),
                  full(wr_pair), full(br)],
        out_specs=[row(D), row(D), row(LANES), row(LANES), pl.BlockSpec((SUBLANES, LANES), lambda i: (0, 0))],
        out_shape=[
            jax.ShapeDtypeStruct((T, D), F32), jax.ShapeDtypeStruct((T, D), F32),
            jax.ShapeDtypeStruct((T, LANES), I32), jax.ShapeDtypeStruct((T, LANES), F32),
            jax.ShapeDtypeStruct((SUBLANES, LANES), F32),
        ],
        scratch_shapes=[pltpu.VMEM((SUBLANES, LANES), F32)],
        compiler_params=_params("arbitrary"),
        name="merge_router",
    )(x2d, o_nsa, o_fox, sgn, sgf, wn, wf, wo, g2, wr_pair, br)


def _dest_kernel(ri_ref, cnt_ref, dest_ref, tab_ref):
    tm = ri_ref.shape[0]
    shift = MOE_ROWS.bit_length() - 1
    cnt = cnt_ref[...].astype(I32)
    padded = ((cnt + (MOE_ROWS - 1)) >> shift) << shift
    lane8 = lax.broadcasted_iota(I32, (SUBLANES, LANES), 1)
    pend = padded
    sh = 1
    while sh < LANES:
        pend = pend + jnp.where(lane8 >= sh, pltpu.roll(pend, sh, 1), 0)
        sh *= 2
    pstart = pend - padded
    row8 = lax.broadcasted_iota(I32, (SUBLANES, LANES), 0)
    tab_ref[...] = jnp.where(row8 == 0, cnt, jnp.where(row8 == 1, padded, jnp.where(row8 == 2, pstart, pend)))

    ri = ri_ref[...]
    lane = lax.broadcasted_iota(I32, (tm, LANES), 1)
    start_row = pstart[0:1, :].astype(F32)
    dest = jnp.zeros((tm, LANES), F32)
    for k in range(TOP_K):
        hot = lane == ri[:, k:k + 1]
        base = jnp.sum(jnp.where(hot, start_row, 0.0), axis=-1, keepdims=True)
        dest = jnp.where(lane == k, base + ri[:, TOP_K + k:TOP_K + k + 1].astype(F32), dest)
    dest_ref[0] = dest.T[:SUBLANES].astype(I32)


def _dest(ri, cnt, tm):
    T = ri.shape[0]
    return pl.pallas_call(
        _dest_kernel,
        grid=(T // tm,),
        in_specs=[pl.BlockSpec((tm, LANES), lambda i: (i, 0)), pl.BlockSpec((SUBLANES, LANES), lambda i: (0, 0))],
        out_specs=[pl.BlockSpec((1, SUBLANES, tm), lambda i: (i, 0, 0)),
                   pl.BlockSpec((SUBLANES, LANES), lambda i: (0, 0))],
        out_shape=[jax.ShapeDtypeStruct((T // tm, SUBLANES, tm), I32), jax.ShapeDtypeStruct((SUBLANES, LANES), I32)],
        compiler_params=_params("arbitrary"),
        name="moe_dest",
    )(ri, cnt)


def _pad_copies(tab_ref, zbuf, xin_hbm, zsem):
    out = []
    for e in range(N_EXPERTS):
        cnt = tab_ref[e]
        pad = tab_ref[LANES + e] - cnt
        base = tab_ref[2 * LANES + e] + cnt
        piece = 1
        while piece < MOE_ROWS:
            cond = (pad & piece) != 0
            if piece < SUBLANES:
                for r in range(piece):
                    out.append((cond, pltpu.make_async_copy(
                        zbuf.at[pl.ds(0, 1)], xin_hbm.at[pl.ds(base + r, 1)], zsem)))
            else:
                out.append((cond, pltpu.make_async_copy(
                    zbuf.at[pl.ds(0, piece)], xin_hbm.at[pl.ds(pl.multiple_of(base, SUBLANES), piece)], zsem)))
            base = base + jnp.where(cond, piece, 0)
            piece *= 2
    last = N_EXPERTS - 1
    n_live = (tab_ref[2 * LANES + last] + tab_ref[LANES + last]) // MOE_ROWS
    n_blocks = xin_hbm.shape[0] // MOE_ROWS
    for blk in range(n_blocks - N_EXPERTS, n_blocks):
        out.append((blk >= n_live, pltpu.make_async_copy(
            zbuf, xin_hbm.at[pl.ds(blk * MOE_ROWS, MOE_ROWS)], zsem)))
    return out


def _dispatch_kernel(tab_ref, dest_hbm, h2_hbm, xin_hbm, idx_smem, hbuf, zbuf, lsem, ssem, isem, zsem):
    i = pl.program_id(0)
    n = pl.num_programs(0)
    tm = hbuf.shape[1]

    def load(j):
        return (pltpu.make_async_copy(h2_hbm.at[pl.ds(j * tm, tm)], hbuf.at[j % 3], lsem.at[j % 3]),
                pltpu.make_async_copy(dest_hbm.at[j], idx_smem.at[pl.ds((j % 3) * (TOP_K * tm), TOP_K * tm)],
                                      isem.at[j % 3]))

    def wait_scatters(j):
        for _ in range(TOP_K):
            pltpu.make_async_copy(hbuf.at[j % 3], xin_hbm.at[pl.ds(0, tm)], ssem.at[j % 3]).wait()

    @pl.when(i == 0)
    def _():
        for cp in load(0):
            cp.start()
        zbuf[...] = jnp.zeros_like(zbuf)
        for cond, cp in _pad_copies(tab_ref, zbuf, xin_hbm, zsem):
            @pl.when(cond)
            def _():
                cp.start()

    @pl.when(i + 1 < n)
    def _():
        for cp in load(i + 1):
            cp.start()

    for cp in load(i):
        cp.wait()
    for slot in range(3):
        @pl.when(i % 3 == slot)
        def _():
            def issue(t, _):
                for k in range(TOP_K):
                    d = idx_smem[slot * (TOP_K * tm) + k * tm + t]
                    pltpu.make_async_copy(hbuf.at[slot, pl.ds(t, 1)], xin_hbm.at[pl.ds(d, 1)], ssem.at[slot]).start()
                return 0

            lax.fori_loop(0, tm, issue, 0)

    @pl.when(i > 0)
    def _():
        wait_scatters(i - 1)

    @pl.when(i == n - 1)
    def _():
        wait_scatters(i)
        for cond, cp in _pad_copies(tab_ref, zbuf, xin_hbm, zsem):
            @pl.when(cond)
            def _():
                cp.wait()


def _dispatch(tab_flat, dest_tiles, h2p, n_rows, tm):
    T, W = h2p.shape
    return pl.pallas_call(
        _dispatch_kernel,
        grid_spec=pltpu.PrefetchScalarGridSpec(
            num_scalar_prefetch=1,
            grid=(T // tm,),
            in_specs=[pl.BlockSpec(memory_space=pl.ANY), pl.BlockSpec(memory_space=pl.ANY)],
            out_specs=pl.BlockSpec(memory_space=pl.ANY),
            scratch_shapes=[
                pltpu.SMEM((3 * TOP_K * tm,), I32),
                pltpu.VMEM((3, tm, W), F32),
                pltpu.VMEM((MOE_ROWS, W), F32),
                pltpu.SemaphoreType.DMA((3,)), pltpu.SemaphoreType.DMA((3,)), pltpu.SemaphoreType.DMA((3,)),
                pltpu.SemaphoreType.DMA,
            ],
        ),
        out_shape=jax.ShapeDtypeStruct((n_rows, W), F32),
        compiler_params=_params("arbitrary"),
        name="moe_dispatch",
    )(tab_flat, dest_tiles, h2p)


def _expert_kernel(be_ref, bs_ref, nx_ref, nv_ref, x_ref, w1_hbm, b1_ref, w2_hbm, b2_ref, y_ref,
                   w1f, w2f, w1b, w2b, slot_ref, sem1, sem2, *, d_ff):
    s = pl.program_id(0)
    live = s < nv_ref[0]
    fresh = (s == 0) | (be_ref[s] != be_ref[jnp.maximum(s - 1, 0)])

    def fetch(e, p):
        return (pltpu.make_async_copy(w1_hbm.at[e], w1f.at[p], sem1.at[p]),
                pltpu.make_async_copy(w2_hbm.at[e], w2f.at[p], sem2.at[p]))

    @pl.when(s == 0)
    def _():
        slot_ref[0] = 0
        for cp in fetch(be_ref[0], 0):
            cp.start()

    @pl.when(live & fresh)
    def _():
        p = slot_ref[0]
        for cp in fetch(be_ref[s], p):
            cp.wait()
        w1b[...] = w1f[p].astype(BF16)
        w2b[...] = w2f[p].astype(BF16)

        @pl.when(nx_ref[s] >= 0)
        def _():
            for cp in fetch(nx_ref[s], 1 - p):
                cp.start()

        slot_ref[0] = 1 - p

    @pl.when(live)
    def _():
        hc = _dot(x_ref[...].astype(BF16), w1b[...]) + b1_ref[0]
        gt = jnp.minimum(hc[:, :d_ff], SWIGLU_LIMIT)
        up = jnp.clip(hc[:, d_ff:], -SWIGLU_LIMIT, SWIGLU_LIMIT)
        glu = gt * jax.nn.sigmoid(SWIGLU_ALPHA * gt)
        act = ((up + 1.0) * glu).astype(BF16)
        y_ref[...] = _dot(act, w2b[...]) + b2_ref[0]

    @pl.when(jnp.logical_not(live))
    def _():
        y_ref[...] = jnp.zeros_like(y_ref)


def _experts(blk_e, blk_src, blk_next, n_live, xin, w1, b1, w2, b2):
    n_rows, W = xin.shape
    E, D, two_f = w1.shape
    d_ff = two_f // 2
    return pl.pallas_call(
        functools.partial(_expert_kernel, d_ff=d_ff),
        grid_spec=pltpu.PrefetchScalarGridSpec(
            num_scalar_prefetch=4,
            grid=(n_rows // MOE_ROWS,),
            in_specs=[
                pl.BlockSpec((MOE_ROWS, W), lambda s, be, bs, nx, nv: (bs[s], 0)),
                pl.BlockSpec(memory_space=pl.ANY),
                pl.BlockSpec((1, 1, two_f), lambda s, be, bs, nx, nv: (be[s], 0, 0)),
                pl.BlockSpec(memory_space=pl.ANY),
                pl.BlockSpec((1, 1, D), lambda s, be, bs, nx, nv: (be[s], 0, 0)),
            ],
            out_specs=pl.BlockSpec((MOE_ROWS, W), lambda s, be, bs, nx, nv: (s, 0)),
            scratch_shapes=[
                pltpu.VMEM((2, D, two_f), F32), pltpu.VMEM((2, d_ff, D), F32),
                pltpu.VMEM((D, two_f), BF16), pltpu.VMEM((d_ff, D), BF16),
                pltpu.SMEM((1,), I32), pltpu.SemaphoreType.DMA((2,)), pltpu.SemaphoreType.DMA((2,)),
            ],
        ),
        out_shape=jax.ShapeDtypeStruct((n_rows, W), F32),
        compiler_params=_params("arbitrary"),
        name="moe_experts",
    )(blk_e, blk_src, blk_next, n_live, xin, w1, b1, w2, b2)


def _combine_kernel(dest_hbm, x1_ref, rw_ref, gf_ref, y_hbm, o_ref, idx_smem, ybuf, gsem, isem):
    i = pl.program_id(0)
    n = pl.num_programs(0)
    tm = x1_ref.shape[0]

    def idx_copy(j):
        return pltpu.make_async_copy(dest_hbm.at[j], idx_smem.at[pl.ds((j % 2) * (TOP_K * tm), TOP_K * tm)],
                                     isem.at[j % 2])

    def issue_into(slot):
        def issue(t, _):
            for k in range(TOP_K):
                d = idx_smem[slot * (TOP_K * tm) + k * tm + t]
                pltpu.make_async_copy(y_hbm.at[pl.ds(d, 1)], ybuf.at[slot, k, pl.ds(t, 1)], gsem.at[slot]).start()
            return 0

        lax.fori_loop(0, tm, issue, 0)

    def issue_gathers(j):
        for slot in range(2):
            @pl.when(j % 2 == slot)
            def _():
                issue_into(slot)

    @pl.when(i == 0)
    def _():
        idx_copy(0).start()
        idx_copy(0).wait()
        issue_into(0)

        @pl.when(n > 1)
        def _():
            idx_copy(1).start()

    @pl.when(i + 1 < n)
    def _():
        idx_copy(i + 1).wait()
        issue_gathers(i + 1)

        @pl.when(i + 2 < n)
        def _():
            idx_copy(i + 2).start()

    slot = i % 2
    for k in range(TOP_K):
        pltpu.make_async_copy(y_hbm.at[pl.ds(0, tm)], ybuf.at[slot, k], gsem.at[slot]).wait()

    rw = rw_ref[...]
    x2 = x1_ref[...]
    for k in range(TOP_K):
        x2 = x2 + rw[:, k:k + 1] * ybuf[slot, k]
    o_ref[...] = _rms(x2, gf_ref[...])


def _combine(dest_tiles, x1, rw, gf, y, tm):
    T, D = x1.shape
    W = y.shape[1]
    return pl.pallas_call(
        _combine_kernel,
        grid=(T // tm,),
        in_specs=[
            pl.BlockSpec(memory_space=pl.ANY),
            pl.BlockSpec((tm, D), lambda i: (i, 0)),
            pl.BlockSpec((tm, LANES), lambda i: (i, 0)),
            pl.BlockSpec((1, D), lambda i: (0, 0)),
            pl.BlockSpec(memory_space=pl.ANY),
        ],
        out_specs=pl.BlockSpec((tm, D), lambda i: (i, 0)),
        out_shape=jax.ShapeDtypeStruct((T, D), F32),
        scratch_shapes=[
            pltpu.SMEM((2 * TOP_K * tm,), I32),
            pltpu.VMEM((2, TOP_K, tm, W), F32),
            pltpu.SemaphoreType.DMA((2,)), pltpu.SemaphoreType.DMA((2,)),
        ],
        compiler_params=_params("arbitrary"),
        name="moe_combine",
    )(dest_tiles, x1, rw, gf, y)


def _rope_tables(positions):
    half = ROT_DIM // 2
    freqs = ROPE_THETA ** (-jnp.arange(0, ROT_DIM, 2, dtype=F32) / ROT_DIM)
    ang = positions.astype(F32)[..., None] * freqs
    cs = jnp.concatenate([jnp.cos(ang), jnp.sin(ang)], axis=-1)
    cs3 = jnp.concatenate(_split3(cs), axis=-1)
    spread = np.zeros((3 * ROT_DIM, 3 * LANES), np.float32)
    for term in range(3):
        for i in range(half):
            for head0 in range(0, LANES, HEAD_DIM):
                spread[term * ROT_DIM + i, [head0 + i, head0 + half + i]] = 1.0
                spread[term * ROT_DIM + half + i, LANES + head0 + i] = -1.0
                spread[term * ROT_DIM + half + i, 2 * LANES + head0 + half + i] = 1.0
    return cs3, jnp.asarray(spread, BF16)


def _arrange_w_in(w_in, d_model):
    nq, nkv, ng = NSA_HEADS * HEAD_DIM, NSA_KV_GROUPS * HEAD_DIM, NSA_HEADS * 3
    fw = FOX_HEADS * HEAD_DIM
    sizes = [nq] + [nkv] * 6 + [ng, fw, fw, fw, FOX_HEADS, d_model, d_model]
    offs = np.concatenate([[0], np.cumsum(sizes)])
    piece = lambda k: w_in[:, offs[k]:offs[k + 1]]
    gates, ff = piece(7), piece(11)
    per_group = HEADS_PER_GROUP * 3
    zeros = lambda n: jnp.zeros((w_in.shape[0], n), w_in.dtype)
    misc0 = jnp.concatenate([gates[:, :per_group], zeros(FGATE_LANE - per_group), ff,
                             zeros(LANES - FGATE_LANE - FOX_HEADS)], axis=1)
    misc1 = jnp.concatenate([gates[:, per_group:], zeros(LANES - per_group)], axis=1)
    cols = [piece(k) for k in range(7)] + [piece(8), piece(9), piece(10), piece(12), piece(13), misc0, misc1]
    return jnp.concatenate(cols, axis=1).astype(BF16)


def _split_bf16(a):
    hi = a.astype(BF16)
    return hi, (a - hi.astype(F32)).astype(BF16)


def kernel(x, positions, norm1_g, w_in, cmp_pos_emb, cmp_w1, cmp_w2, fox_f_bias, w_proj_nsa, w_proj_fox, w_out,
           norm2_g, router_w, router_b, expert_w1, expert_b1, expert_w2, expert_b2, norm_f_g):
    B, S, D = x.shape
    T = B * S
    assert norm1_g.shape[0] == 1, "the combine kernel fuses the output norm, so it must follow the only layer"
    assert S % KV_CHUNK == 0 and S % FOX_TILE == 0 and KV_CHUNK >= WINDOW, "attention variants assume whole chunks"
    assert S // SEL_BLOCK <= LANES - HEAD_DIM and S % (CMP_STRIDE * SUBLANES) == 0
    tm_proj = min(TOKEN_TILE, S)
    tm_tok = min(TOKEN_TILE, T)
    l = 0

    rope_cs, rope_spread = _rope_tables(positions)
    n_sb = S // SEL_BLOCK
    n_rows16 = S // CMP_STRIDE
    ci = np.arange(n_rows16)[None, :] * CMP_STRIDE
    sj = np.arange(LANES)[:, None] * SEL_BLOCK
    overlap = jnp.asarray(((ci < sj + SEL_BLOCK) & (ci + CMP_BLOCK > sj) & (np.arange(LANES)[:, None] < n_sb)
                           & (np.arange(n_rows16)[None, :] < n_rows16 - 1)).astype(np.float32), BF16)

    w_all = _arrange_w_in(w_in[l], D)
    fb_row = jnp.zeros((1, LANES), F32).at[0, FGATE_LANE:FGATE_LANE + FOX_HEADS].set(fox_f_bias[l].astype(F32))
    (qraw, qrot, kvc, ks, vs, kw, vw, fq, fk, fv, sgn, sgf, misc) = _inproj(
        x, norm1_g[l][None, :], w_all, rope_cs, rope_spread, fb_row, tm_proj)
    qrot = qrot.reshape(B, NSA_HEADS, S, LANES)
    fq = fq.reshape(B, FOX_HEADS, S, LANES)
    fv = fv.reshape(B, FOX_HEADS, S, LANES)

    kv_rows = kvc.reshape(B, 2, NSA_KV_GROUPS, n_rows16, CMP_STRIDE * HEAD_DIM)
    pe = cmp_pos_emb[l].reshape(2, 1, CMP_BLOCK * HEAD_DIM).astype(F32)
    cw1 = jnp.pad(cmp_w1[l], ((0, 0), (0, 0), (0, LANES - HEAD_DIM))).astype(BF16)
    cw2 = jnp.pad(cmp_w2[l], ((0, 0), (0, LANES - HEAD_DIM), (0, 0))).astype(BF16)
    cmp_kv = _compress(kv_rows, pe, cw1, cw2)

    o_nsa = _nsa(qraw, qrot, cmp_kv, ks, vs, kw, vw, misc, overlap)
    o_fox = _fox(fq, fk, fv)

    wr = jnp.pad(router_w[l], ((0, 0), (0, LANES - N_EXPERTS)))
    wr_pair = jnp.concatenate(_split_bf16(wr), axis=1)
    br = jnp.pad(router_b[l], (0, LANES - N_EXPERTS))[None, :].astype(F32)
    x1, h2, ri, rw, cnt = _merge(
        x.reshape(T, D), o_nsa.reshape(T, -1), o_fox.reshape(T, -1), sgn.reshape(T, D), sgf.reshape(T, D),
        w_proj_nsa[l].astype(BF16), w_proj_fox[l].astype(BF16), w_out[l].astype(BF16),
        norm2_g[l][None, :], wr_pair, br, tm_tok)

    dest, tab = _dest(ri, cnt, tm_tok)
    dest_tiles = dest[:, :TOP_K, :].reshape(T // tm_tok, TOP_K * tm_tok)
    pend = tab[3, :N_EXPERTS]
    n_blocks = (T * TOP_K) // MOE_ROWS + N_EXPERTS
    n_live = (pend[N_EXPERTS - 1] // MOE_ROWS).astype(I32)
    blk_src = jnp.minimum(jnp.arange(n_blocks, dtype=I32), n_live - 1)
    blk_e = jnp.sum((pend[None, :] <= (blk_src * MOE_ROWS)[:, None]).astype(I32), axis=1)
    blk_e = jnp.minimum(blk_e, N_EXPERTS - 1)
    after = jnp.sum(jnp.where(blk_e[:, None] == jnp.arange(N_EXPERTS)[None, :], pend[None, :], 0), axis=1)
    blk_next = jnp.minimum(jnp.sum((pend[None, :] <= after[:, None]).astype(I32), axis=1), N_EXPERTS - 1)
    blk_next = jnp.where(after < n_live * MOE_ROWS, blk_next, -1).astype(I32)

    xin = _dispatch(tab[:3].reshape(-1), dest_tiles, h2,
                    n_blocks * MOE_ROWS, tm_tok)
    y = _experts(blk_e, blk_src, blk_next, n_live[None], xin, expert_w1[l], expert_b1[l][:, None, :],
                 expert_w2[l], expert_b2[l][:, None, :])
    out = _combine(dest_tiles, x1, rw, norm_f_g[None, :], y, tm_tok)
    return out.reshape(B, S, D)
```
